```python
import math
import jax, jax.numpy as jnp
from jax import lax
import numpy as np

D_MODEL = 1024
BATCH = 2
SEQ = 8192
DEPTH = 2

N_BRANCH = 4
A_HEADS = 4
A_HEAD_DIM = 128
A_WIDTH = A_HEADS * A_HEAD_DIM
IDX_HEADS = 4
IDX_DIM = 64
TOPK_MAX = 256
CB_WIDTH = 512
CB_CONV = 31
CC_WIDTH = 512
CC_CONV = 3
SB_HEADS = 4
SB_HEAD_DIM = 128
SB_WIDTH = SB_HEADS * SB_HEAD_DIM
D_FF = 2816
FFN_CONV = 3
ROPE_THETA = 500000.0
ROT_FRACTION_DIV = 4
BLOCK_Q = 128
NORM_EPS = 1e-6

SPLIT_SIZES = (A_WIDTH, A_WIDTH, A_WIDTH, IDX_HEADS * IDX_DIM, IDX_DIM, IDX_HEADS,
               2 * CB_WIDTH, 3 * CC_WIDTH, 3 * SB_WIDTH, N_BRANCH * D_MODEL)
N_IN = sum(SPLIT_SIZES)

kernel_name = "hybrid_gated_dsa_conformer_shortconv_stickbreak"


def _split_offsets(sizes):
    offs, acc = [], 0
    for s in sizes[:-1]:
        acc += s
        offs.append(acc)
    return offs


def rms_norm(x, g):
    xf = x.astype(jnp.float32)
    y = xf * lax.rsqrt(jnp.mean(xf * xf, axis=-1, keepdims=True) + NORM_EPS)
    return (y * g.astype(jnp.float32)).astype(x.dtype)


def layer_norm(x, g, b):
    xf = x.astype(jnp.float32)
    mu = jnp.mean(xf, axis=-1, keepdims=True)
    xc = xf - mu
    y = xc * lax.rsqrt(jnp.mean(xc * xc, axis=-1, keepdims=True) + NORM_EPS)
    return (y * g.astype(jnp.float32) + b.astype(jnp.float32)).astype(x.dtype)


def partial_rope(x, pos):
    d = x.shape[-1]
    rot = d // ROT_FRACTION_DIV
    half = rot // 2
    inv_freq = ROPE_THETA ** (-(jnp.arange(half, dtype=jnp.float32) * 2.0) / rot)
    ang = pos.astype(jnp.float32)[..., None] * inv_freq
    cos = jnp.cos(ang)[:, :, None, :]
    sin = jnp.sin(ang)[:, :, None, :]
    xf = x.astype(jnp.float32)
    x1, x2 = xf[..., :half], xf[..., half:rot]
    out = jnp.concatenate([x1 * cos - x2 * sin, x2 * cos + x1 * sin, xf[..., rot:]], axis=-1)
    return out.astype(x.dtype)


def causal_dwconv(x, w, b=None):
    width = w.shape[0]
    y = lax.conv_general_dilated(
        x, w.astype(x.dtype)[:, None, :], window_strides=(1,), padding=[(width - 1, 0)],
        dimension_numbers=('NWC', 'WIO', 'NWC'), feature_group_count=x.shape[-1])
    if b is not None:
        y = y + b.astype(x.dtype)
    return y


def _to_blocks(a, nb):
    return jnp.moveaxis(a.reshape((a.shape[0], nb, BLOCK_Q) + a.shape[2:]), 1, 0)


def _from_blocks(a):
    a = jnp.moveaxis(a, 0, 1)
    return a.reshape(a.shape[0], -1, a.shape[-2] * a.shape[-1])


def dsa_attention(q, k, v, q_idx, k_idx, w_idx):
    bsz, seq = q.shape[0], q.shape[1]
    nb = seq // BLOCK_Q
    k_sel = min(TOPK_MAX, seq // 4)
    kv = jnp.stack([k, v], axis=2)
    s_idx = jnp.arange(seq)
    k_idx_f = k_idx.astype(jnp.float32)

    def block(args):
        qb, qib, wb, tb = args
        rel = jax.nn.relu(jnp.einsum('bthd,bsd->bths', qib.astype(jnp.float32), k_idx_f) * IDX_DIM ** -0.5)
        score = jnp.einsum('bths,bth->bts', rel, wb.astype(jnp.float32)) * IDX_HEADS ** -0.5
        causal = s_idx[None, :] <= tb[:, None]
        score = jnp.where(causal[None], score, -jnp.inf)
        _, sel = lax.top_k(score, k_sel)
        valid = sel <= tb[None, :, None]
        kv_sel = jax.vmap(lambda a, i: a[i])(kv, sel)
        logits = jnp.einsum('bthd,btkhd->bthk', qb.astype(jnp.float32),
                            kv_sel[:, :, :, 0].astype(jnp.float32)) * A_HEAD_DIM ** -0.5
        logits = jnp.where(valid[:, :, None, :], logits, -jnp.inf)
        p = jax.nn.softmax(logits, axis=-1)
        out = jnp.einsum('bthk,btkhd->bthd', p, kv_sel[:, :, :, 1].astype(jnp.float32))
        return out.astype(qb.dtype)

    t_blocks = jnp.arange(seq).reshape(nb, BLOCK_Q)
    out = lax.map(block, (_to_blocks(q, nb), _to_blocks(q_idx, nb), _to_blocks(w_idx, nb), t_blocks))
    return _from_blocks(out)


def stick_breaking_attention(q, k, v):
    seq = q.shape[1]
    nb = seq // BLOCK_Q
    s_idx = jnp.arange(seq)
    k_f = k.astype(jnp.float32)
    v_f = v.astype(jnp.float32)

    def block(args):
        qb, tb = args
        z = jnp.einsum('bthd,bshd->bhts', qb.astype(jnp.float32), k_f) * SB_HEAD_DIM ** -0.5
        mask = (s_idx[None, :] < tb[:, None])[None, None]
        log_keep = jnp.where(mask, jax.nn.log_sigmoid(-z), 0.0)
        between = lax.cumsum(log_keep, axis=3, reverse=True) - log_keep
        a = jnp.where(mask, jnp.exp(jax.nn.log_sigmoid(z) + between), 0.0)
        return jnp.einsum('bhts,bshd->bthd', a, v_f).astype(qb.dtype)

    t_blocks = jnp.arange(seq).reshape(nb, BLOCK_Q)
    out = lax.map(block, (_to_blocks(q, nb), t_blocks))
    return _from_blocks(out)


def setup_inputs(seed: int = 0) -> dict:
    key = jax.random.key(seed)
    ks = jax.random.split(key, 24)
    f32 = jnp.float32

    def nrm(k, shape, scale):
        return jax.random.normal(k, shape, f32) * scale

    def gain(k, shape):
        return 1.0 + 0.05 * jax.random.normal(k, shape, f32)

    L, D = DEPTH, D_MODEL
    return {
        "x": jax.random.normal(ks[0], (BATCH, SEQ, D), f32),
        "positions": jnp.broadcast_to(jnp.arange(SEQ, dtype=jnp.int32), (BATCH, SEQ)),
        "g_mix": gain(ks[1], (L, D)),
        "w_in": nrm(ks[2], (L, D, N_IN), D ** -0.5),
        "b_gate": nrm(ks[3], (L, N_BRANCH * D), 0.1),
        "g_qa": gain(ks[4], (L, A_HEAD_DIM)),
        "g_ka": gain(ks[5], (L, A_HEAD_DIM)),
        "g_kidx": gain(ks[6], (L, IDX_DIM)),
        "w_oa": nrm(ks[7], (L, A_WIDTH, D), A_WIDTH ** -0.5),
        "cb_conv_w": nrm(ks[8], (L, CB_CONV, CB_WIDTH), CB_CONV ** -0.5),
        "cb_conv_b": nrm(ks[9], (L, CB_WIDTH), 0.02),
        "cb_ln_g": gain(ks[10], (L, CB_WIDTH)),
        "cb_ln_b": nrm(ks[11], (L, CB_WIDTH), 0.02),
        "w_ob": nrm(ks[12], (L, CB_WIDTH, D), CB_WIDTH ** -0.5),
        "cc_conv_w": nrm(ks[13], (L, CC_CONV, CC_WIDTH), CC_CONV ** -0.5),
        "w_oc": nrm(ks[14], (L, CC_WIDTH, D), CC_WIDTH ** -0.5),
        "w_od": nrm(ks[15], (L, SB_WIDTH, D), SB_WIDTH ** -0.5),
        "w_merge": nrm(ks[16], (L, D, D), D ** -0.5),
        "g_ffn": gain(ks[17], (L, D)),
        "w_ffn_gate": nrm(ks[18], (L, D, D_FF), D ** -0.5),
        "w_ffn_up": nrm(ks[19], (L, D, D_FF), D ** -0.5),
        "ffn_conv_w": nrm(ks[20], (L, FFN_CONV, D_FF), FFN_CONV ** -0.5),
        "w_ffn_down": nrm(ks[21], (L, D_FF, D), D_FF ** -0.5),
    }


def reference(x, positions, g_mix, w_in, b_gate, g_qa, g_ka, g_kidx, w_oa,
              cb_conv_w, cb_conv_b, cb_ln_g, cb_ln_b, w_ob, cc_conv_w, w_oc, w_od,
              w_merge, g_ffn, w_ffn_gate, w_ffn_up, ffn_conv_w, w_ffn_down):
    bsz, seq, dm = x.shape
    offs = _split_offsets(SPLIT_SIZES)
    for l in range(DEPTH):
        h = rms_norm(x, g_mix[l])
        proj = h @ w_in[l]
        aq, ak, av, iq, ik, iw, glu_in, cc_in, sb_in, gate_in = jnp.split(proj, offs, axis=-1)

        qa = partial_rope(rms_norm(aq.reshape(bsz, seq, A_HEADS, A_HEAD_DIM), g_qa[l]), positions)
        ka = partial_rope(rms_norm(ak.reshape(bsz, seq, A_HEADS, A_HEAD_DIM), g_ka[l]), positions)
        va = av.reshape(bsz, seq, A_HEADS, A_HEAD_DIM)
        qi = partial_rope(iq.reshape(bsz, seq, IDX_HEADS, IDX_DIM), positions)
        ki = partial_rope(rms_norm(ik, g_kidx[l])[:, :, None, :], positions)[:, :, 0]
        y_a = dsa_attention(qa, ka, va, qi, ki, iw) @ w_oa[l]

        glu_a, glu_b = jnp.split(glu_in, 2, axis=-1)
        u = causal_dwconv(glu_a * jax.nn.sigmoid(glu_b), cb_conv_w[l], cb_conv_b[l])
        y_b = jax.nn.silu(layer_norm(u, cb_ln_g[l], cb_ln_b[l])) @ w_ob[l]

        gb, gc, xc = jnp.split(cc_in, 3, axis=-1)
        y_c = (gb * causal_dwconv(gc * xc, cc_conv_w[l])) @ w_oc[l]

        sq, sk, sv = jnp.split(sb_in, 3, axis=-1)
        shp = (bsz, seq, SB_HEADS, SB_HEAD_DIM)
        y_d = stick_breaking_attention(sq.reshape(shp), sk.reshape(shp), sv.reshape(shp)) @ w_od[l]

        gates = jax.nn.sigmoid(gate_in + b_gate[l]).reshape(bsz, seq, N_BRANCH, dm)
        merged = (gates[:, :, 0] * y_a + gates[:, :, 1] * y_b
                  + gates[:, :, 2] * y_c + gates[:, :, 3] * y_d)
        x = x + merged @ w_merge[l]

        h2 = rms_norm(x, g_ffn[l])
        gt = causal_dwconv(h2 @ w_ffn_gate[l], ffn_conv_w[l])
        x = x + (jax.nn.silu(gt) * (h2 @ w_ffn_up[l])) @ w_ffn_down[l]
    return x
```

```python
import functools

import jax
import jax.numpy as jnp
from jax import lax
from jax.experimental import pallas as pl
from jax.experimental.pallas import tpu as pltpu

F32 = jnp.float32
BF16 = jnp.bfloat16
I32 = jnp.int32

D_MODEL = 1024
N_BRANCH = 4
A_HEADS = 4
A_HEAD_DIM = 128
A_WIDTH = A_HEADS * A_HEAD_DIM
IDX_HEADS = 4
IDX_DIM = 64
TOPK_MAX = 256
CB_WIDTH = 512
CB_CONV = 31
CC_WIDTH = 512
CC_CONV = 3
SB_HEADS = 4
SB_HEAD_DIM = 128
SB_WIDTH = SB_HEADS * SB_HEAD_DIM
D_FF = 2816
FFN_CONV = 3
ROPE_THETA = 500000.0
ROT_FRACTION_DIV = 4
NORM_EPS = 1e-6

LANES = 128
SUBLANES = 8
VMEM_LIMIT = 56 * 1024 * 1024

INT_MIN = -(2 ** 31)
NEG_BIG = -1e30

OFF_AQ = 0
OFF_AK = 512
OFF_AV = 1024
OFF_IQ = 1536
OFF_IK = 1792
OFF_IW = 1920
OFF_GLU = 2048
OFF_CC = 3072
OFF_SB = 4608
OFF_GATE = 6144
N_PACK = 10240
PREP_W = 2048


def _cparams(sem):
    return pltpu.CompilerParams(dimension_semantics=sem, vmem_limit_bytes=VMEM_LIMIT)


def _trig_kernel(pos_ref, invf_ref, cos_ref, sin_ref):
    ang = pos_ref[...] * invf_ref[...]
    cos_ref[...] = jnp.cos(ang)
    sin_ref[...] = jnp.sin(ang)


def _rope_tables(positions, head_dim):
    rot = head_dim // ROT_FRACTION_DIV
    half = rot // 2
    inv_freq = ROPE_THETA ** (-(jnp.arange(half, dtype=F32) * 2.0) / rot)
    t = positions.size
    rows = t * half // LANES
    pos_rep = jnp.broadcast_to(positions.astype(F32).reshape(t, 1), (t, half)).reshape(rows, LANES)
    invf = jnp.tile(inv_freq, LANES // half).reshape(1, LANES)
    tr = min(rows, 512)
    cos, sin = pl.pallas_call(
        _trig_kernel,
        grid=(rows // tr,),
        in_specs=[pl.BlockSpec((tr, LANES), lambda i: (i, 0)), pl.BlockSpec((1, LANES), lambda i: (0, 0))],
        out_specs=[pl.BlockSpec((tr, LANES), lambda i: (i, 0))] * 2,
        out_shape=[jax.ShapeDtypeStruct((rows, LANES), F32)] * 2,
        compiler_params=_cparams(("parallel",)),
        name="rope_trig",
    )(pos_rep, invf)
    return cos.reshape(t, half), sin.reshape(t, half)


def _rope_lane_tables(cos, sin, head_dim):
    t, half = cos.shape
    rest = head_dim - 2 * half
    c = jnp.concatenate([cos, cos, jnp.ones((t, rest), F32)], axis=1)
    s = jnp.concatenate([sin, sin, jnp.zeros((t, rest), F32)], axis=1)
    reps = LANES // head_dim
    return jnp.tile(c, (1, reps)), jnp.tile(s, (1, reps))


def _inproj_kernel(x_ref, g_ref, w_ref, o_ref, h_ref):
    @pl.when(pl.program_id(1) == 0)
    def _():
        x = x_ref[...]
        ms = jnp.mean(x * x, axis=-1, keepdims=True)
        h_ref[...] = (x * lax.rsqrt(ms + NORM_EPS) * g_ref[...]).astype(BF16)

    o_ref[...] = jnp.dot(h_ref[...], w_ref[...], preferred_element_type=F32)


def _inproj(x2d, g, w_packed, tm, tn):
    t, d = x2d.shape
    n = w_packed.shape[1]
    return pl.pallas_call(
        _inproj_kernel,
        grid=(t // tm, n // tn),
        in_specs=[
            pl.BlockSpec((tm, d), lambda i, j: (i, 0)),
            pl.BlockSpec((1, d), lambda i, j: (0, 0)),
            pl.BlockSpec((d, tn), lambda i, j: (0, j)),
        ],
        out_specs=pl.BlockSpec((tm, tn), lambda i, j: (i, j)),
        out_shape=jax.ShapeDtypeStruct((t, n), F32),
        scratch_shapes=[pltpu.VMEM((tm, d), BF16)],
        compiler_params=_cparams(("parallel", "arbitrary")),
        name="inproj",
    )(x2d, g.reshape(1, d), w_packed)


def _pack_w_in(w):
    d = w.shape[0]
    o = 0
    aq, ak, av = w[:, 0:512], w[:, 512:1024], w[:, 1024:1536]
    iq = w[:, 1536:1792]
    ik = w[:, 1792:1856]
    iw = w[:, 1856:1860]
    rest = w[:, 1860:]
    z = lambda k: jnp.zeros((d, k), w.dtype)
    packed = jnp.concatenate([aq, ak, av, iq, ik, z(64), iw, z(124), rest], axis=1)
    return packed.astype(BF16)


def _rotate_half(x, half, period):
    lane = lax.broadcasted_iota(I32, x.shape, 1)
    up = pltpu.roll(x, LANES - half, 1)
    dn = pltpu.roll(x, half, 1)
    return jnp.where((lane % period) < half, -up, dn)


def _dsa_prep_kernel(p_ref, ca_ref, sa_ref, ci_ref, si_ref, gq_ref, gk_ref, gi_ref,
                     qa_ref, ka_ref, va_ref, qi_ref, ki_ref, wv_ref):
    ca, sa, ci, si = ca_ref[...], sa_ref[...], ci_ref[...], si_ref[...]
    a_half = A_HEAD_DIM // ROT_FRACTION_DIV // 2
    i_half = IDX_DIM // ROT_FRACTION_DIV // 2

    def head_norm(x, g, n_real):
        ms = jnp.sum(x * x, axis=-1, keepdims=True) * (1.0 / n_real)
        return x * lax.rsqrt(ms + NORM_EPS) * g

    for h in range(A_HEADS):
        sl = slice(h * A_HEAD_DIM, (h + 1) * A_HEAD_DIM)
        q = head_norm(p_ref[:, OFF_AQ + h * 128:OFF_AQ + (h + 1) * 128], gq_ref[...], A_HEAD_DIM)
        q = q * ca + _rotate_half(q, a_half, A_HEAD_DIM) * sa
        qa_ref[:, sl] = (q * (A_HEAD_DIM ** -0.5)).astype(BF16)
        k = head_norm(p_ref[:, OFF_AK + h * 128:OFF_AK + (h + 1) * 128], gk_ref[...], A_HEAD_DIM)
        k = k * ca + _rotate_half(k, a_half, A_HEAD_DIM) * sa
        ka_ref[:, sl] = k.astype(BF16)
    va_ref[...] = p_ref[:, OFF_AV:OFF_AV + A_WIDTH].astype(BF16)
    for j in range(IDX_HEADS * IDX_DIM // LANES):
        sl = slice(j * LANES, (j + 1) * LANES)
        qi = p_ref[:, OFF_IQ + j * LANES:OFF_IQ + (j + 1) * LANES]
        qi_ref[:, sl] = (qi * ci + _rotate_half(qi, i_half, IDX_DIM) * si).astype(BF16)
    ki = head_norm(p_ref[:, OFF_IK:OFF_IK + LANES], gi_ref[...], IDX_DIM)
    ki_ref[...] = (ki * ci + _rotate_half(ki, i_half, IDX_DIM) * si).astype(BF16)
    wv_ref[...] = p_ref[:, OFF_IW:OFF_IW + LANES] * (IDX_DIM ** -0.5 * IDX_HEADS ** -0.5)


def _dsa_prep(proj, ca, sa, ci, si, g_qa, g_ka, g_kidx, tm):
    t = proj.shape[0]
    row = lambda i: (i, 0)
    fixed = lambda i: (0, 0)
    gi = jnp.concatenate([g_kidx, jnp.ones((LANES - IDX_DIM,), F32)]).reshape(1, LANES)
    tbl = pl.BlockSpec((tm, LANES), row)
    gsp = pl.BlockSpec((1, LANES), fixed)
    return pl.pallas_call(
        _dsa_prep_kernel,
        grid=(t // tm,),
        in_specs=[pl.BlockSpec((tm, PREP_W), row), tbl, tbl, tbl, tbl, gsp, gsp, gsp],
        out_specs=[pl.BlockSpec((tm, A_WIDTH), row)] * 3
        + [pl.BlockSpec((tm, IDX_HEADS * IDX_DIM), row), pl.BlockSpec((tm, LANES), row), pl.BlockSpec((tm, LANES), row)],
        out_shape=[jax.ShapeDtypeStruct((t, A_WIDTH), BF16)] * 3
        + [jax.ShapeDtypeStruct((t, IDX_HEADS * IDX_DIM), BF16), jax.ShapeDtypeStruct((t, LANES), BF16),
           jax.ShapeDtypeStruct((t, LANES), F32)],
        compiler_params=_cparams(("parallel",)),
        name="dsa_prep",
    )(proj, ca, sa, ci, si, g_qa.reshape(1, LANES), g_ka.reshape(1, LANES), gi)


def _dsa_kernel(qT_ref, qiT_ref, wT_ref, ki_ref, ka_ref, vT_ref, o_ref,
                keys_ref, m_ref, l_ref, acc_ref, *, tq, ck, k_sel, idx_bits):
    i = pl.program_id(1)
    n_chunks = ((i + 1) * tq + ck - 1) // ck
    q_pos = i * tq + lax.broadcasted_iota(I32, (1, tq), 1)

    def chunk_start(c):
        return pl.multiple_of(c * ck, ck)

    def key_pos(c):
        return c * ck + lax.broadcasted_iota(I32, (ck, tq), 0)

    def score_body(c, carry):
        ks = chunk_start(c)
        kic = ki_ref[pl.ds(ks, ck), :]
        sc = jnp.zeros((ck, tq), F32)
        for h in range(IDX_HEADS):
            d = jnp.dot(kic, qiT_ref[h * LANES:(h + 1) * LANES, :], preferred_element_type=F32)
            sc = sc + jnp.maximum(d, 0.0) * wT_ref[h:h + 1, :]
        sc = jnp.where(sc == 0.0, 0.0, sc)
        bits = pltpu.bitcast(sc, I32)
        key = bits ^ ((bits >> 31) & 0x7FFFFFFF)
        keys_ref[pl.ds(ks, ck), :] = jnp.where(key_pos(c) <= q_pos, key, INT_MIN)
        return carry

    lax.fori_loop(0, n_chunks, score_body, 0)

    def count(indicator):
        def body(c, acc):
            k = keys_ref[pl.ds(chunk_start(c), ck), :]
            return acc + indicator(k, c).reshape(ck // SUBLANES, SUBLANES, tq).sum(axis=0)
        acc = lax.fori_loop(0, n_chunks, body, jnp.zeros((SUBLANES, tq), I32))
        return acc.sum(axis=0, keepdims=True)

    def count_ge(t):
        return count(lambda k, c: jnp.where(k >= t, 1, 0))

    zero = jnp.zeros((1, tq), I32)
    thr = jnp.where(count_ge(zero) >= k_sel, zero, INT_MIN)

    def bisect(b, thr):
        cand = thr + jnp.left_shift(1, 30 - b)
        return jnp.where(count_ge(cand) >= k_sel, cand, thr)

    thr = lax.fori_loop(0, 31, bisect, thr)

    n_gt = count(lambda k, c: jnp.where(k > thr, 1, 0))
    n_ge = count_ge(thr)
    room = k_sel - n_gt
    excess = jnp.where(n_ge - n_gt > room, jnp.where(thr > INT_MIN, 1, 0), 0)

    @pl.when(jnp.max(excess) > 0)
    def _():
        def tie_bisect(b, cut):
            cand = cut + jnp.left_shift(1, idx_bits - 1 - b)
            n = count(lambda k, c: jnp.where(k == thr, jnp.where(key_pos(c) < cand, 1, 0), 0))
            return jnp.where(n <= room, cand, cut)
        cut = lax.fori_loop(0, idx_bits, tie_bisect, zero)

        def demote(c, carry):
            ks = chunk_start(c)
            k = keys_ref[pl.ds(ks, ck), :]
            keys_ref[pl.ds(ks, ck), :] = jnp.where(k == thr, jnp.where(key_pos(c) >= cut, INT_MIN, k), k)
            return carry
        lax.fori_loop(0, n_chunks, demote, 0)

    thr_sel = jnp.maximum(thr, INT_MIN + 1)
    m_ref[...] = jnp.full(m_ref.shape, NEG_BIG, F32)
    l_ref[...] = jnp.zeros(l_ref.shape, F32)
    acc_ref[...] = jnp.zeros(acc_ref.shape, F32)

    def att_body(c, carry):
        ks = chunk_start(c)
        sel = keys_ref[pl.ds(ks, ck), :] >= thr_sel
        for h in range(A_HEADS):
            hs = slice(h * A_HEAD_DIM, (h + 1) * A_HEAD_DIM)
            lg = jnp.dot(ka_ref[pl.ds(ks, ck), hs], qT_ref[hs, :], preferred_element_type=F32)
            m_old = m_ref[h:h + 1, :]
            m_new = jnp.maximum(m_old, jnp.max(jnp.where(sel, lg, NEG_BIG), axis=0, keepdims=True))
            alpha = jnp.exp(m_old - m_new)
            p = jnp.where(sel, jnp.exp(lg - m_new), 0.0)
            l_ref[h:h + 1, :] = alpha * l_ref[h:h + 1, :] + jnp.sum(p, axis=0, keepdims=True)
            m_ref[h:h + 1, :] = m_new
            pv = jnp.dot(vT_ref[c, hs, :], p.astype(BF16), preferred_element_type=F32)
            acc_ref[h] = acc_ref[h] * alpha + pv
        return carry

    lax.fori_loop(0, n_chunks, att_body, 0)

    for h in range(A_HEADS):
        out_t = acc_ref[h] / l_ref[h:h + 1, :]
        o_ref[:, h * A_HEAD_DIM:(h + 1) * A_HEAD_DIM] = out_t.T.astype(o_ref.dtype)


def _dsa_attention(qa, ka, va, qi, ki, wv, bsz, seq, tq, ck):
    k_sel = min(TOPK_MAX, seq // 4)
    nck = seq // ck
    qT = qa.reshape(bsz, seq, A_WIDTH).transpose(0, 2, 1)
    qi4 = qi.reshape(bsz, seq, IDX_HEADS, IDX_DIM)
    qi4 = jnp.pad(qi4, ((0, 0), (0, 0), (0, 0), (0, LANES - IDX_DIM)))
    qiT = qi4.reshape(bsz, seq, IDX_HEADS * LANES).transpose(0, 2, 1)
    wT = wv.reshape(bsz, seq, LANES)[:, :, :SUBLANES].transpose(0, 2, 1)
    ki3 = ki.reshape(bsz, seq, LANES)
    ka3 = ka.reshape(bsz, seq, A_WIDTH)
    vT = va.reshape(bsz, nck, ck, A_WIDTH).transpose(0, 1, 3, 2)
    kern = functools.partial(_dsa_kernel, tq=tq, ck=ck, k_sel=k_sel, idx_bits=int(seq).bit_length())
    once = pl.Buffered(1)
    out = pl.pallas_call(
        kern,
        grid=(bsz, seq // tq),
        in_specs=[
            pl.BlockSpec((None, A_WIDTH, tq), lambda b, i: (b, 0, i)),
            pl.BlockSpec((None, IDX_HEADS * LANES, tq), lambda b, i: (b, 0, i)),
            pl.BlockSpec((None, SUBLANES, tq), lambda b, i: (b, 0, i)),
            pl.BlockSpec((None, seq, LANES), lambda b, i: (b, 0, 0), pipeline_mode=once),
            pl.BlockSpec((None, seq, A_WIDTH), lambda b, i: (b, 0, 0), pipeline_mode=once),
            pl.BlockSpec((None, nck, A_WIDTH, ck), lambda b, i: (b, 0, 0, 0), pipeline_mode=once),
        ],
        out_specs=pl.BlockSpec((None, tq, A_WIDTH), lambda b, i: (b, i, 0)),
        out_shape=jax.ShapeDtypeStruct((bsz, seq, A_WIDTH), BF16),
        scratch_shapes=[
            pltpu.VMEM((seq, tq), I32),
            pltpu.VMEM((SUBLANES, tq), F32),
            pltpu.VMEM((SUBLANES, tq), F32),
            pltpu.VMEM((A_HEADS, A_HEAD_DIM, tq), F32),
        ],
        compiler_params=_cparams(("parallel", "arbitrary")),
        name="dsa_attention",
    )(qT, qiT, wT, ki3, ka3, vT)
    return out.reshape(bsz * seq, A_WIDTH)


def _sb_kernel(q_ref, kT_ref, v_ref, uu_ref, o_ref, acc_ref, carry_ref, *, tq):
    i = pl.program_id(2)
    q = (q_ref[...] * (SB_HEAD_DIM ** -0.5)).astype(BF16)
    acc_ref[...] = jnp.zeros(acc_ref.shape, F32)
    carry_ref[...] = jnp.zeros(carry_ref.shape, F32)
    n_sub = tq // LANES
    row = lax.broadcasted_iota(I32, (tq, LANES), 0)
    col = lax.broadcasted_iota(I32, (tq, LANES), 1)

    def chunk(c, diag):
        z = jnp.dot(q, kT_ref[c], preferred_element_type=F32)
        a_parts = [None] * n_sub
        for j in reversed(range(n_sub)):
            zj = z[:, j * LANES:(j + 1) * LANES]
            lk = -(jnp.maximum(zj, 0.0) + jnp.log(1.0 + jnp.exp(-jnp.abs(zj))))
            if diag:
                mask = (col + j * LANES) < row
                lk = jnp.where(mask, lk, 0.0)
            hi = lk.astype(BF16)
            lo = (lk - hi.astype(F32)).astype(BF16)
            r = jnp.dot(jnp.concatenate([hi, lo], axis=1), uu_ref[...], preferred_element_type=F32)
            between = r[:, :LANES] + carry_ref[...]
            carry_ref[...] = carry_ref[...] + r[:, LANES:]
            a = jnp.exp(zj + lk + between)
            if diag:
                a = jnp.where(mask, a, 0.0)
            a_parts[j] = a.astype(BF16)
        a_all = jnp.concatenate(a_parts, axis=1)
        vs = pl.multiple_of(c * tq, tq)
        acc_ref[...] += jnp.dot(a_all, v_ref[pl.ds(vs, tq), :], preferred_element_type=F32)

    chunk(i, True)

    def body(s, carry):
        chunk(i - 1 - s, False)
        return carry

    lax.fori_loop(0, i, body, 0)
    o_ref[...] = acc_ref[...].astype(o_ref.dtype)


def _cumsum_matrix():
    j = jnp.arange(LANES)[:, None]
    s = jnp.arange(LANES)[None, :]
    u = (j > s).astype(BF16)
    half = jnp.concatenate([u, jnp.ones((LANES, LANES), BF16)], axis=1)
    return jnp.concatenate([half, half], axis=0)


def _sb_attention(proj, bsz, seq, tq):
    t = bsz * seq
    nq = seq // tq
    sk = proj[:, OFF_SB + SB_WIDTH:OFF_SB + 2 * SB_WIDTH].astype(BF16)
    sv = proj[:, OFF_SB + 2 * SB_WIDTH:OFF_SB + 3 * SB_WIDTH].astype(BF16)
    kT = sk.reshape(bsz, nq, tq, SB_HEADS, SB_HEAD_DIM).transpose(0, 3, 1, 4, 2)
    v3 = sv.reshape(bsz, seq, SB_WIDTH)
    q_col0 = OFF_SB // SB_HEAD_DIM
    out = pl.pallas_call(
        functools.partial(_sb_kernel, tq=tq),
        grid=(bsz, SB_HEADS, nq),
        in_specs=[
            pl.BlockSpec((tq, SB_HEAD_DIM), lambda b, h, i: (b * nq + i, q_col0 + h)),
            pl.BlockSpec((None, None, nq, SB_HEAD_DIM, tq), lambda b, h, i: (b, h, 0, 0, 0)),
            pl.BlockSpec((None, seq, SB_HEAD_DIM), lambda b, h, i: (b, 0, h)),
            pl.BlockSpec((2 * LANES, 2 * LANES), lambda b, h, i: (0, 0)),
        ],
        out_specs=pl.BlockSpec((tq, SB_HEAD_DIM), lambda b, h, i: (b * nq + i, h)),
        out_shape=jax.ShapeDtypeStruct((t, SB_WIDTH), BF16),
        scratch_shapes=[pltpu.VMEM((tq, SB_HEAD_DIM), F32), pltpu.VMEM((tq, LANES), F32)],
        compiler_params=_cparams(("parallel", "parallel", "arbitrary")),
        name="stickbreak",
    )(proj, kT, v3, _cumsum_matrix())
    return out


CB_HALO = 32
CC_HALO = 8
CONV_ROWS = 64


def _conv_kernel(ga_ref, gb_ref, hga_ref, hgb_ref, cb_ref, cc_ref, cx_ref, hcc_ref, hcx_ref,
                 wb_ref, bb_ref, lng_ref, lnb_ref, wc_ref, ob_ref, oc_ref, buf_ref, u_ref, buf2_ref, *, ts, seq):
    i = pl.program_id(0)
    keep = jnp.where((i * ts) % seq == 0, 0.0, 1.0)

    buf_ref[0:CB_HALO, :] = hga_ref[...] * jax.nn.sigmoid(hgb_ref[...]) * keep
    buf_ref[CB_HALO:, :] = ga_ref[...] * jax.nn.sigmoid(gb_ref[...])
    for r in range(ts // CONV_ROWS):
        for g in range(CB_WIDTH // LANES):
            ls = slice(g * LANES, (g + 1) * LANES)
            acc = jnp.broadcast_to(bb_ref[:, ls], (CONV_ROWS, LANES))
            for k in range(CB_CONV):
                base = r * CONV_ROWS + CB_HALO - (CB_CONV - 1) + k
                acc = acc + wb_ref[k:k + 1, ls] * buf_ref[base:base + CONV_ROWS, ls]
            u_ref[r * CONV_ROWS:(r + 1) * CONV_ROWS, ls] = acc
    u = u_ref[...]
    mu = jnp.mean(u, axis=-1, keepdims=True)
    uc = u - mu
    y = uc * lax.rsqrt(jnp.mean(uc * uc, axis=-1, keepdims=True) + NORM_EPS) * lng_ref[...] + lnb_ref[...]
    ob_ref[...] = (y * jax.nn.sigmoid(y)).astype(ob_ref.dtype)

    buf2_ref[0:CC_HALO, :] = hcc_ref[...] * hcx_ref[...] * keep
    buf2_ref[CC_HALO:, :] = cc_ref[...] * cx_ref[...]
    conv = jnp.zeros((ts, CC_WIDTH), F32)
    for k in range(CC_CONV):
        base = CC_HALO - (CC_CONV - 1) + k
        conv = conv + wc_ref[k:k + 1, :] * buf2_ref[base:base + ts, :]
    oc_ref[...] = (cb_ref[...] * conv).astype(oc_ref.dtype)


def _conv_mixers(proj, wb, bb, lng, lnb, wc, seq, ts):
    t = proj.shape[0]
    w = CB_WIDTH
    c0 = OFF_GLU // w
    c1 = OFF_CC // w
    cur = lambda col: pl.BlockSpec((ts, w), lambda i: (i, col))
    halo = lambda rows, col: pl.BlockSpec((rows, w), lambda i: (jnp.maximum(i * (ts // rows) - 1, 0), col))
    full = lambda r: pl.BlockSpec((r, w), lambda i: (0, 0))
    wb_p = jnp.pad(wb, ((0, CB_HALO - CB_CONV), (0, 0)))
    wc_p = jnp.pad(wc, ((0, SUBLANES - CC_CONV), (0, 0)))
    return pl.pallas_call(
        functools.partial(_conv_kernel, ts=ts, seq=seq),
        grid=(t // ts,),
        in_specs=[cur(c0), cur(c0 + 1), halo(CB_HALO, c0), halo(CB_HALO, c0 + 1),
                  cur(c1), cur(c1 + 1), cur(c1 + 2), halo(CC_HALO, c1 + 1), halo(CC_HALO, c1 + 2),
                  full(CB_HALO), full(1), full(1), full(1), full(SUBLANES)],
        out_specs=[pl.BlockSpec((ts, w), lambda i: (i, 0))] * 2,
        out_shape=[jax.ShapeDtypeStruct((t, w), BF16)] * 2,
        scratch_shapes=[pltpu.VMEM((CB_HALO + ts, w), F32), pltpu.VMEM((ts, w), F32),
                        pltpu.VMEM((CC_HALO + ts, w), F32)],
        compiler_params=_cparams(("parallel",)),
        name="conv_mixers",
    )(proj, proj, proj, proj, proj, proj, proj, proj, proj,
      wb_p, bb.reshape(1, w), lng.reshape(1, w), lnb.reshape(1, w), wc_p)


def _merge_kernel(x_ref, oa_ref, ob_ref, oc_ref, od_ref, g0_ref, g1_ref, g2_ref, g3_ref, bg_ref,
                  wa_ref, wb_ref, wc_ref, wd_ref, wm_ref, o_ref):
    merged = None
    branches = ((oa_ref, wa_ref, g0_ref), (ob_ref, wb_ref, g1_ref), (oc_ref, wc_ref, g2_ref), (od_ref, wd_ref, g3_ref))
    for n, (b_ref, w_ref, g_ref) in enumerate(branches):
        y = jnp.dot(b_ref[...], w_ref[...], preferred_element_type=F32)
        term = jax.nn.sigmoid(g_ref[...] + bg_ref[n:n + 1, :]) * y
        merged = term if merged is None else merged + term
    o_ref[...] = x_ref[...] + jnp.dot(merged.astype(BF16), wm_ref[...], preferred_element_type=F32)


def _merge(x2d, oa, ob, oc, od, proj, b_gate, w_oa, w_ob, w_oc, w_od, w_merge, tm):
    t, d = x2d.shape
    g0 = OFF_GATE // d
    row = lambda i: (i, 0)
    fixed = lambda i: (0, 0)
    br = pl.BlockSpec((tm, A_WIDTH), row)
    gate = lambda n: pl.BlockSpec((tm, d), lambda i: (i, g0 + n))
    wsp = pl.BlockSpec((A_WIDTH, d), fixed)
    bf = lambda w: w.astype(BF16)
    return pl.pallas_call(
        _merge_kernel,
        grid=(t // tm,),
        in_specs=[pl.BlockSpec((tm, d), row), br, br, br, br, gate(0), gate(1), gate(2), gate(3),
                  pl.BlockSpec((N_BRANCH, d), fixed), wsp, wsp, wsp, wsp, pl.BlockSpec((d, d), fixed)],
        out_specs=pl.BlockSpec((tm, d), row),
        out_shape=jax.ShapeDtypeStruct((t, d), F32),
        compiler_params=_cparams(("parallel",)),
        name="merge",
    )(x2d, oa, ob, oc, od, proj, proj, proj, proj, b_gate.reshape(N_BRANCH, d),
      bf(w_oa), bf(w_ob), bf(w_oc), bf(w_od), bf(w_merge))


FFN_HALO = 8


def _ffn_kernel(x_ref, hx_ref, g_ref, wg_ref, wu_ref, cw_ref, wd_ref, o_ref, h_ref, gp_ref, acc_ref, *, tm, seq):
    i = pl.program_id(0)
    f = pl.program_id(1)

    def norm(x):
        ms = jnp.mean(x * x, axis=-1, keepdims=True)
        return (x * lax.rsqrt(ms + NORM_EPS) * g_ref[...]).astype(BF16)

    @pl.when(f == 0)
    def _():
        h_ref[0:FFN_HALO, :] = norm(hx_ref[...])
        h_ref[FFN_HALO:, :] = norm(x_ref[...])
        acc_ref[...] = x_ref[...]

    keep = jnp.where((i * tm) % seq == 0, 0.0, 1.0)
    gp_ref[0:FFN_HALO, :] = jnp.dot(h_ref[0:FFN_HALO, :], wg_ref[...], preferred_element_type=F32) * keep
    gp_ref[FFN_HALO:, :] = jnp.dot(h_ref[FFN_HALO:, :], wg_ref[...], preferred_element_type=F32)
    gt = jnp.zeros((tm, gp_ref.shape[1]), F32)
    for k in range(FFN_CONV):
        base = FFN_HALO - (FFN_CONV - 1) + k
        gt = gt + cw_ref[k:k + 1, :] * gp_ref[base:base + tm, :]
    up = jnp.dot(h_ref[FFN_HALO:, :], wu_ref[...], preferred_element_type=F32)
    act = (gt * jax.nn.sigmoid(gt) * up).astype(BF16)
    acc_ref[...] += jnp.dot(act, wd_ref[...], preferred_element_type=F32)

    @pl.when(f == pl.num_programs(1) - 1)
    def _():
        o_ref[...] = acc_ref[...]


def _ffn(x2d, g, wg, wu, cw, wd, seq, tm, tf):
    t, d = x2d.shape
    ff = wg.shape[1]
    cw_p = jnp.pad(cw, ((0, SUBLANES - FFN_CONV), (0, 0)))
    return pl.pallas_call(
        functools.partial(_ffn_kernel, tm=tm, seq=seq),
        grid=(t // tm, ff // tf),
        in_specs=[
            pl.BlockSpec((tm, d), lambda i, f: (i, 0)),
            pl.BlockSpec((FFN_HALO, d), lambda i, f: (jnp.maximum(i * (tm // FFN_HALO) - 1, 0), 0)),
            pl.BlockSpec((1, d), lambda i, f: (0, 0)),
            pl.BlockSpec((d, tf), lambda i, f: (0, f)),
            pl.BlockSpec((d, tf), lambda i, f: (0, f)),
            pl.BlockSpec((SUBLANES, tf), lambda i, f: (0, f)),
            pl.BlockSpec((tf, d), lambda i, f: (f, 0)),
        ],
        out_specs=pl.BlockSpec((tm, d), lambda i, f: (i, 0)),
        out_shape=jax.ShapeDtypeStruct((t, d), F32),
        scratch_shapes=[pltpu.VMEM((FFN_HALO + tm, d), BF16), pltpu.VMEM((FFN_HALO + tm, tf), F32),
                        pltpu.VMEM((tm, d), F32)],
        compiler_params=_cparams(("parallel", "arbitrary")),
        name="ffn",
    )(x2d, x2d, g.reshape(1, d), wg.astype(BF16), wu.astype(BF16), cw_p, wd.astype(BF16))


def _tiles(seq):
    big = min(seq, 512)
    return dict(
        inproj_tm=min(seq, 1024), inproj_tn=1024,
        prep_tm=big, conv_ts=big, merge_tm=big, ffn_tm=big, ffn_tf=D_FF // 2,
        dsa_tq=LANES, dsa_ck=min(seq, 512), sb_tq=min(seq, 256),
    )


def kernel(x, positions, g_mix, w_in, b_gate, g_qa, g_ka, g_kidx, w_oa, cb_conv_w, cb_conv_b, cb_ln_g, cb_ln_b,
           w_ob, cc_conv_w, w_oc, w_od, w_merge, g_ffn, w_ffn_gate, w_ffn_up, ffn_conv_w, w_ffn_down):
    bsz, seq, dm = x.shape
    depth = w_in.shape[0]
    assert dm == D_MODEL and seq % LANES == 0
    tl = _tiles(seq)
    t = bsz * seq

    cos_a, sin_a = _rope_tables(positions, A_HEAD_DIM)
    cos_i, sin_i = _rope_tables(positions, IDX_DIM)
    ca, sa = _rope_lane_tables(cos_a, sin_a, A_HEAD_DIM)
    ci, si = _rope_lane_tables(cos_i, sin_i, IDX_DIM)

    x2d = x.reshape(t, dm)
    for l in range(depth):
        proj = _inproj(x2d, g_mix[l], _pack_w_in(w_in[l]), tl["inproj_tm"], tl["inproj_tn"])
        qa, ka, va, qi, ki, wv = _dsa_prep(proj, ca, sa, ci, si, g_qa[l], g_ka[l], g_kidx[l], tl["prep_tm"])
        o_a = _dsa_attention(qa, ka, va, qi, ki, wv, bsz, seq, tl["dsa_tq"], tl["dsa_ck"])
        o_d = _sb_attention(proj, bsz, seq, tl["sb_tq"])
        o_b, o_c = _conv_mixers(proj, cb_conv_w[l], cb_conv_b[l], cb_ln_g[l], cb_ln_b[l], cc_conv_w[l],
                                seq, tl["conv_ts"])
        x2d = _merge(x2d, o_a, o_b, o_c, o_d, proj, b_gate[l], w_oa[l], w_ob[l], w_oc[l], w_od[l], w_merge[l],
                     tl["merge_tm"])
        x2d = _ffn(x2d, g_ffn[l], w_ffn_gate[l], w_ffn_up[l], ffn_conv_w[l], w_ffn_down[l], seq,
                   tl["ffn_tm"], tl["ffn_tf"])
    return x2d.reshape(bsz, seq, dm)
```

```python
import functools
import math

import jax
import jax.numpy as jnp
from jax import lax
from jax.experimental import pallas as pl
from jax.experimental.pallas import tpu as pltpu

F32 = jnp.float32
BF16 = jnp.bfloat16
I32 = jnp.int32

D_MODEL = 1024
N_BRANCH = 4
A_HEADS = 4
A_HEAD_DIM = 128
A_WIDTH = A_HEADS * A_HEAD_DIM
IDX_HEADS = 4
IDX_DIM = 64
TOPK_MAX = 256
CB_WIDTH = 512
CB_CONV = 31
CC_WIDTH = 512
CC_CONV = 3
SB_HEADS = 4
SB_HEAD_DIM = 128
SB_WIDTH = SB_HEADS * SB_HEAD_DIM
D_FF = 2816
FFN_CONV = 3
ROPE_THETA = 500000.0
ROT_FRACTION_DIV = 4
NORM_EPS = 1e-6

LANES = 128
SUBLANES = 8
VMEM_LIMIT = 56 * 1024 * 1024

LOG2E = math.log2(math.e)
NEG_BIG = -1e30
M_INIT = -1e20
F32_LOWEST = float(jnp.finfo(jnp.float32).min)

OFF_AQ = 0
OFF_AK = 512
OFF_AV = 1024
OFF_IQ = 1536
OFF_IK = 1792
OFF_IW = 1920
OFF_GLU = 2048
OFF_CC = 3072
OFF_SB = 4608
OFF_GATE = 6144
N_PACK = 10240
PREP_W = 2048
CHUNK = 512


def _cparams(sem):
    return pltpu.CompilerParams(dimension_semantics=sem, vmem_limit_bytes=VMEM_LIMIT)


def _trig_kernel(pos_ref, invf_ref, cos_ref, sin_ref):
    ang = pos_ref[...] * invf_ref[...]
    cos_ref[...] = jnp.cos(ang)
    sin_ref[...] = jnp.sin(ang)


def _rope_tables(positions, head_dim):
    rot = head_dim // ROT_FRACTION_DIV
    half = rot // 2
    inv_freq = ROPE_THETA ** (-(jnp.arange(half, dtype=F32) * 2.0) / rot)
    t = positions.size
    rows = t * half // LANES
    pos_rep = jnp.broadcast_to(positions.astype(F32).reshape(t, 1), (t, half)).reshape(rows, LANES)
    invf = jnp.tile(inv_freq, LANES // half).reshape(1, LANES)
    tr = min(rows, 512)
    cos, sin = pl.pallas_call(
        _trig_kernel,
        grid=(rows // tr,),
        in_specs=[pl.BlockSpec((tr, LANES), lambda i: (i, 0)), pl.BlockSpec((1, LANES), lambda i: (0, 0))],
        out_specs=[pl.BlockSpec((tr, LANES), lambda i: (i, 0))] * 2,
        out_shape=[jax.ShapeDtypeStruct((rows, LANES), F32)] * 2,
        compiler_params=_cparams(("parallel",)),
        name="rope_trig",
    )(pos_rep, invf)
    return cos.reshape(t, half), sin.reshape(t, half)


def _rope_lane_tables(cos, sin, head_dim):
    t, half = cos.shape
    rest = head_dim - 2 * half
    c = jnp.concatenate([cos, cos, jnp.ones((t, rest), F32)], axis=1)
    s = jnp.concatenate([sin, sin, jnp.zeros((t, rest), F32)], axis=1)
    reps = LANES // head_dim
    return jnp.tile(c, (1, reps)), jnp.tile(s, (1, reps))


def _inproj_kernel(x_ref, g_ref, w_ref, o_ref, h_ref):
    @pl.when(pl.program_id(1) == 0)
    def _():
        x = x_ref[...]
        ms = jnp.mean(x * x, axis=-1, keepdims=True)
        h_ref[...] = (x * lax.rsqrt(ms + NORM_EPS) * g_ref[...]).astype(BF16)

    o_ref[...] = jnp.dot(h_ref[...], w_ref[...], preferred_element_type=F32)


def _inproj(x2d, g, w_packed, tm, tn):
    t, d = x2d.shape
    n = w_packed.shape[1]
    return pl.pallas_call(
        _inproj_kernel,
        grid=(t // tm, n // tn),
        in_specs=[
            pl.BlockSpec((tm, d), lambda i, j: (i, 0)),
            pl.BlockSpec((1, d), lambda i, j: (0, 0)),
            pl.BlockSpec((d, tn), lambda i, j: (0, j)),
        ],
        out_specs=pl.BlockSpec((tm, tn), lambda i, j: (i, j)),
        out_shape=jax.ShapeDtypeStruct((t, n), F32),
        scratch_shapes=[pltpu.VMEM((tm, d), BF16)],
        compiler_params=_cparams(("parallel", "arbitrary")),
        name="inproj",
    )(x2d, g.reshape(1, d), w_packed)


def _pack_w_in(w):
    d = w.shape[0]
    aq, ak, av = w[:, 0:512], w[:, 512:1024], w[:, 1024:1536]
    iq = w[:, 1536:1792]
    ik = w[:, 1792:1856]
    iw = w[:, 1856:1860]
    rest = w[:, 1860:]
    z = lambda k: jnp.zeros((d, k), w.dtype)
    packed = jnp.concatenate([aq, ak, av, iq, ik, z(64), iw, z(124), rest], axis=1)
    return packed.astype(BF16)


def _rotate_half(x, half, period):
    lane = lax.broadcasted_iota(I32, x.shape, 1)
    up = pltpu.roll(x, LANES - half, 1)
    dn = pltpu.roll(x, half, 1)
    return jnp.where((lane % period) < half, -up, dn)


def _prep_kernel(p_ref, s_ref, ca_ref, sa_ref, ci_ref, si_ref, gq_ref, gk_ref, gi_ref,
                 qT_ref, qiT_ref, wT_ref, ki_ref, ka_ref, vT_ref, sq_ref, skT_ref, sv_ref):
    ca, sa, ci, si = ca_ref[...], sa_ref[...], ci_ref[...], si_ref[...]
    a_half = A_HEAD_DIM // ROT_FRACTION_DIV // 2
    i_half = IDX_DIM // ROT_FRACTION_DIV // 2

    def head_norm(x, g, n_real):
        ms = jnp.sum(x * x, axis=-1, keepdims=True) * (1.0 / n_real)
        return x * lax.rsqrt(ms + NORM_EPS) * g

    for h in range(A_HEADS):
        sl = slice(h * A_HEAD_DIM, (h + 1) * A_HEAD_DIM)
        q = head_norm(p_ref[:, OFF_AQ + h * 128:OFF_AQ + (h + 1) * 128], gq_ref[...], A_HEAD_DIM)
        q = q * ca + _rotate_half(q, a_half, A_HEAD_DIM) * sa
        qT_ref[sl, :] = (q * (A_HEAD_DIM ** -0.5 * LOG2E)).T.astype(BF16)
        k = head_norm(p_ref[:, OFF_AK + h * 128:OFF_AK + (h + 1) * 128], gk_ref[...], A_HEAD_DIM)
        ka_ref[:, sl] = (k * ca + _rotate_half(k, a_half, A_HEAD_DIM) * sa).astype(BF16)
        vT_ref[sl, :] = p_ref[:, OFF_AV + h * 128:OFF_AV + (h + 1) * 128].T.astype(BF16)

    row = lax.broadcasted_iota(I32, (LANES, p_ref.shape[0]), 0)
    for j in range(IDX_HEADS * IDX_DIM // LANES):
        qi = p_ref[:, OFF_IQ + j * LANES:OFF_IQ + (j + 1) * LANES]
        qt = (qi * ci + _rotate_half(qi, i_half, IDX_DIM) * si).T
        qiT_ref[(2 * j) * LANES:(2 * j + 1) * LANES, :] = jnp.where(row < IDX_DIM, qt, 0.0).astype(BF16)
        qiT_ref[(2 * j + 1) * LANES:(2 * j + 2) * LANES, :] = jnp.where(row >= IDX_DIM, qt, 0.0).astype(BF16)
    ki = head_norm(p_ref[:, OFF_IK:OFF_IK + LANES], gi_ref[...], IDX_DIM)
    ki = ki * ci + _rotate_half(ki, i_half, IDX_DIM) * si
    ki_ref[...] = (ki + pltpu.roll(ki, IDX_DIM, 1)).astype(BF16)
    w = p_ref[:, OFF_IW:OFF_IW + LANES] * (IDX_DIM ** -0.5 * IDX_HEADS ** -0.5)
    wT_ref[...] = w.T[0:SUBLANES, :]

    sq_ref[...] = (s_ref[:, 0:SB_WIDTH] * (SB_HEAD_DIM ** -0.5 * LOG2E)).astype(BF16)
    for h in range(SB_HEADS):
        skT_ref[h] = s_ref[:, SB_WIDTH + h * 128:SB_WIDTH + (h + 1) * 128].T.astype(BF16)
    sv_ref[...] = s_ref[:, 2 * SB_WIDTH:3 * SB_WIDTH].astype(BF16)


def _prep(proj, ca, sa, ci, si, g_qa, g_ka, g_kidx, bsz, seq):
    t = proj.shape[0]
    ck = min(CHUNK, seq)
    nck = seq // ck
    gi = jnp.concatenate([g_kidx, jnp.ones((LANES - IDX_DIM,), F32)]).reshape(1, LANES)
    tok = lambda b, i: (b * nck + i, 0)
    fixed = lambda b, i: (0, 0)
    tbl = pl.BlockSpec((ck, LANES), tok)
    gsp = pl.BlockSpec((1, LANES), fixed)
    sb_col = OFF_SB // (3 * SB_WIDTH)
    return pl.pallas_call(
        _prep_kernel,
        grid=(bsz, nck),
        in_specs=[pl.BlockSpec((ck, PREP_W), tok), pl.BlockSpec((ck, 3 * SB_WIDTH), lambda b, i: (b * nck + i, sb_col)),
                  tbl, tbl, tbl, tbl, gsp, gsp, gsp],
        out_specs=[
            pl.BlockSpec((None, A_WIDTH, ck), lambda b, i: (b, 0, i)),
            pl.BlockSpec((None, IDX_HEADS * LANES, ck), lambda b, i: (b, 0, i)),
            pl.BlockSpec((None, SUBLANES, ck), lambda b, i: (b, 0, i)),
            pl.BlockSpec((None, ck, LANES), lambda b, i: (b, i, 0)),
            pl.BlockSpec((None, ck, A_WIDTH), lambda b, i: (b, i, 0)),
            pl.BlockSpec((None, None, A_WIDTH, ck), lambda b, i: (b, i, 0, 0)),
            pl.BlockSpec((ck, SB_WIDTH), tok),
            pl.BlockSpec((None, SB_HEADS, None, SB_HEAD_DIM, ck), lambda b, i: (b, 0, i, 0, 0)),
            pl.BlockSpec((None, ck, SB_WIDTH), lambda b, i: (b, i, 0)),
        ],
        out_shape=[
            jax.ShapeDtypeStruct((bsz, A_WIDTH, seq), BF16),
            jax.ShapeDtypeStruct((bsz, IDX_HEADS * LANES, seq), BF16),
            jax.ShapeDtypeStruct((bsz, SUBLANES, seq), F32),
            jax.ShapeDtypeStruct((bsz, seq, LANES), BF16),
            jax.ShapeDtypeStruct((bsz, seq, A_WIDTH), BF16),
            jax.ShapeDtypeStruct((bsz, nck, A_WIDTH, ck), BF16),
            jax.ShapeDtypeStruct((t, SB_WIDTH), BF16),
            jax.ShapeDtypeStruct((bsz, SB_HEADS, nck, SB_HEAD_DIM, ck), BF16),
            jax.ShapeDtypeStruct((bsz, seq, SB_WIDTH), BF16),
        ],
        compiler_params=_cparams(("parallel", "parallel")),
        name="mixer_prep",
    )(proj, proj, ca, sa, ci, si, g_qa.reshape(1, LANES), g_ka.reshape(1, LANES), gi)


def _order_key(f):
    b = pltpu.bitcast(f, I32)
    return b ^ ((b >> 31) & 0x7FFFFFFF)


def _key_float(k):
    return pltpu.bitcast(k ^ ((k >> 31) & 0x7FFFFFFF), F32)


def _dsa_kernel(qT_ref, qiT_ref, wT_ref, ki_ref, ka_ref, vT_ref, o_ref, sc_ref, red_ref, m_ref, l_ref, acc_ref,
                *, tq, ck, grp, k_sel, idx_bits):
    i = pl.program_id(1)
    n_keys = (i + 1) * tq
    n_chunks = (n_keys + ck - 1) // ck
    n_groups = (n_keys + grp - 1) // grp
    q_pos = i * tq + lax.broadcasted_iota(I32, (1, tq), 1)
    neg_inf = jnp.full((ck, tq), -jnp.inf, F32)

    def chunk_start(c):
        return pl.multiple_of(c * ck, ck)

    def key_pos(c):
        return c * ck + lax.broadcasted_iota(I32, (ck, tq), 0)

    def score_chunk(c, diag):
        ks = chunk_start(c)
        kic = ki_ref[pl.ds(ks, ck), :]
        sc = None
        for h in range(IDX_HEADS):
            d = jnp.dot(kic, qiT_ref[h * LANES:(h + 1) * LANES, :], preferred_element_type=F32)
            term = jnp.maximum(d, 0.0) * wT_ref[h:h + 1, :]
            sc = term if sc is None else sc + term
        if diag:
            causal = key_pos(c) <= q_pos
            sc_hi = jnp.where(causal, sc, -jnp.inf)
            sc_lo = jnp.where(causal, sc, jnp.inf)
        else:
            sc_hi = sc_lo = sc
        sc_ref[pl.ds(ks, ck), :] = sc_hi
        fold = lambda a, op: op(a.reshape(ck // SUBLANES, SUBLANES, tq), axis=0)
        return fold(sc_hi, jnp.max), fold(sc_lo, jnp.min)

    def score_body(c, carry):
        mx, mn = score_chunk(c, False)
        return jnp.maximum(carry[0], mx), jnp.minimum(carry[1], mn)

    init = (jnp.full((SUBLANES, tq), -jnp.inf, F32), jnp.full((SUBLANES, tq), jnp.inf, F32))
    mx, mn = lax.fori_loop(0, n_chunks - 1, score_body, init)
    mx_d, mn_d = score_chunk(n_chunks - 1, True)
    def fold_rows(x, op):
        red_ref[...] = x
        r = red_ref[0:1, :]
        for j in range(1, SUBLANES):
            r = op(r, red_ref[j:j + 1, :])
        return r

    row_max = fold_rows(jnp.maximum(mx, mx_d), jnp.maximum)
    row_min = fold_rows(jnp.minimum(mn, mn_d), jnp.minimum)

    def pad_body(c, carry):
        sc_ref[pl.ds(chunk_start(c), ck), :] = neg_inf
        return carry
    lax.fori_loop(n_chunks, n_groups * (grp // ck), pad_body, 0)

    ones = jnp.ones((SUBLANES, grp), BF16)
    max_groups = sc_ref.shape[0] // grp

    def count_ge(t):
        for ng in range(1, max_groups + 1):
            @pl.when(n_groups == ng)
            def _(ng=ng):
                acc = None
                for g in range(ng):
                    ind = jnp.where(sc_ref[g * grp:(g + 1) * grp, :] >= t, 1.0, 0.0).astype(BF16)
                    part = jnp.dot(ones, ind, preferred_element_type=F32)
                    acc = part if acc is None else acc + part
                red_ref[...] = acc
        return red_ref[0:1, :].astype(I32)

    n_causal = q_pos + 1
    take_all = n_causal <= k_sel
    lo0 = _order_key(row_min)
    hi0 = _order_key(row_max) + 1
    thr0 = jnp.full((1, tq), F32_LOWEST, F32)
    done0 = jnp.where(take_all, 1, 0)
    zero = jnp.zeros((1, tq), I32)

    def search_cond(st):
        it, _, _, _, _, done, _ = st
        return jnp.logical_and(it < 40, jnp.min(done) == 0)

    def search_body(st):
        it, lo, hi, c_hi, thr, done, tied = st
        cross = jnp.logical_and(lo < 0, hi > 0)
        gap = hi - lo
        adjacent = jnp.logical_and(gap == 1, jnp.logical_not(cross))
        mid = jnp.where(cross, 0, lo + (gap >> 1))
        pivot = _key_float(mid)
        cnt = count_ge(pivot)
        live = done == 0
        hit = jnp.logical_and(live, jnp.logical_and(cnt == k_sel, jnp.logical_not(adjacent)))
        end = jnp.logical_and(live, adjacent)
        move = jnp.logical_and(live, jnp.logical_not(jnp.logical_or(hit, end)))
        up = jnp.logical_and(move, cnt > k_sel)
        dn = jnp.logical_and(move, cnt < k_sel)
        thr = jnp.where(hit, pivot, jnp.where(end, _key_float(lo), thr))
        tied = jnp.where(end, 1, tied)
        done = jnp.where(jnp.logical_or(hit, end), 1, done)
        lo = jnp.where(up, mid, lo)
        hi = jnp.where(dn, mid, hi)
        c_hi = jnp.where(dn, cnt, c_hi)
        return it + 1, lo, hi, c_hi, thr, done, tied

    st = lax.while_loop(search_cond, search_body, (jnp.int32(0), lo0, hi0, zero, thr0, done0, zero))
    _, _, _, c_hi, thr, _, tied = st

    room = k_sel - c_hi

    @pl.when(jnp.max(tied) > 0)
    def _():
        def count_tied_below(cand):
            def body(c, acc):
                s = sc_ref[pl.ds(chunk_start(c), ck), :]
                one = jnp.where(s == thr, jnp.where(key_pos(c) < cand, 1, 0), 0)
                return acc + one.reshape(ck // SUBLANES, SUBLANES, tq).sum(axis=0)
            acc = lax.fori_loop(0, n_chunks, body, jnp.zeros((SUBLANES, tq), I32))
            return acc.sum(axis=0, keepdims=True)

        def tie_bisect(b, cut):
            cand = cut + jnp.left_shift(1, idx_bits - 1 - b)
            return jnp.where(count_tied_below(cand) <= room, cand, cut)
        cut = lax.fori_loop(0, idx_bits, tie_bisect, zero)
        cut = jnp.where(tied > 0, cut, jnp.int32(2 ** 30))

        def drop(c, carry):
            ks = chunk_start(c)
            s = sc_ref[pl.ds(ks, ck), :]
            sc_ref[pl.ds(ks, ck), :] = jnp.where(s == thr, jnp.where(key_pos(c) >= cut, -jnp.inf, s), s)
            return carry
        lax.fori_loop(0, n_chunks, drop, 0)

    m_ref[...] = jnp.full(m_ref.shape, M_INIT, F32)
    l_ref[...] = jnp.zeros(l_ref.shape, F32)
    acc_ref[...] = jnp.zeros(acc_ref.shape, F32)
    heads = [slice(h * A_HEAD_DIM, (h + 1) * A_HEAD_DIM) for h in range(A_HEADS)]

    def att_body(c, carry):
        ks = chunk_start(c)
        sel = sc_ref[pl.ds(ks, ck), :] >= thr
        logits = [jnp.dot(ka_ref[pl.ds(ks, ck), hs], qT_ref[hs, :], preferred_element_type=F32) for hs in heads]
        for h, hs in enumerate(heads):
            lg = jnp.where(sel, logits[h], NEG_BIG)
            m_old = m_ref[h, 0:1, :]
            m_new = jnp.maximum(m_old, jnp.max(lg, axis=0, keepdims=True))
            alpha = jnp.exp2(m_old - m_new)
            p = jnp.exp2(lg - m_new)
            l_ref[h, 0:1, :] = alpha * l_ref[h, 0:1, :] + jnp.sum(p, axis=0, keepdims=True)
            m_ref[h, 0:1, :] = m_new
            pv = jnp.dot(vT_ref[c, hs, :], p.astype(BF16), preferred_element_type=F32)
            acc_ref[h] = acc_ref[h] * alpha + pv
        return carry

    lax.fori_loop(0, n_chunks, att_body, 0)
    for h, hs in enumerate(heads):
        o_ref[:, hs] = (acc_ref[h] / l_ref[h, 0:1, :]).T.astype(o_ref.dtype)


def _dsa_attention(qT, qiT, wT, ki, ka, vT, bsz, seq, tq):
    ck = min(CHUNK, seq)
    nck = seq // ck
    grp = min(4 * ck, seq)
    k_sel = min(TOPK_MAX, seq // 4)
    kern = functools.partial(_dsa_kernel, tq=tq, ck=ck, grp=grp, k_sel=k_sel, idx_bits=int(seq).bit_length())
    once = pl.Buffered(1)
    out = pl.pallas_call(
        kern,
        grid=(bsz, seq // tq),
        in_specs=[
            pl.BlockSpec((None, A_WIDTH, tq), lambda b, i: (b, 0, i)),
            pl.BlockSpec((None, IDX_HEADS * LANES, tq), lambda b, i: (b, 0, i)),
            pl.BlockSpec((None, SUBLANES, tq), lambda b, i: (b, 0, i)),
            pl.BlockSpec((None, seq, LANES), lambda b, i: (b, 0, 0), pipeline_mode=once),
            pl.BlockSpec((None, seq, A_WIDTH), lambda b, i: (b, 0, 0), pipeline_mode=once),
            pl.BlockSpec((None, nck, A_WIDTH, ck), lambda b, i: (b, 0, 0, 0), pipeline_mode=once),
        ],
        out_specs=pl.BlockSpec((None, tq, A_WIDTH), lambda b, i: (b, i, 0)),
        out_shape=jax.ShapeDtypeStruct((bsz, seq, A_WIDTH), BF16),
        scratch_shapes=[pltpu.VMEM((seq, tq), F32), pltpu.VMEM((SUBLANES, tq), F32),
                        pltpu.VMEM((A_HEADS, SUBLANES, tq), F32), pltpu.VMEM((A_HEADS, SUBLANES, tq), F32),
                        pltpu.VMEM((A_HEADS, A_HEAD_DIM, tq), F32)],
        compiler_params=_cparams(("parallel", "arbitrary")),
        name="dsa_attention",
    )(qT, qiT, wT, ki, ka, vT)
    return out.reshape(bsz * seq, A_WIDTH)


def _sb_kernel(q_ref, kT_ref, v_ref, uu_ref, o_ref, acc_ref, carry_ref, *, tq):
    i = pl.program_id(2)
    q = q_ref[...]
    acc_ref[...] = jnp.zeros(acc_ref.shape, F32)
    carry_ref[...] = jnp.zeros(carry_ref.shape, F32)
    n_sub = tq // LANES
    row = lax.broadcasted_iota(I32, (tq, LANES), 0)
    col = lax.broadcasted_iota(I32, (tq, LANES), 1)

    def chunk(c, diag):
        z = jnp.dot(q, kT_ref[c], preferred_element_type=F32)
        lz, cum, tot, masks = [], [], [], []
        for j in range(n_sub):
            zj = z[:, j * LANES:(j + 1) * LANES]
            neg_abs = pltpu.bitcast(pltpu.bitcast(zj, I32) | jnp.int32(-2 ** 31), F32)
            neg_soft = jnp.log(1.0 + jnp.exp2(neg_abs)) * (-LOG2E)
            lk = neg_soft - jnp.maximum(zj, 0.0)
            mask = None
            if diag:
                mask = (col + j * LANES) < row
                lk = jnp.where(mask, lk, 0.0)
            r = jnp.dot(lk.astype(BF16), uu_ref[...], preferred_element_type=F32)
            lz.append(zj + lk)
            cum.append(r[:, :LANES])
            tot.append(r[:, LANES:])
            masks.append(mask)
        run = carry_ref[...]
        a_parts = [None] * n_sub
        for j in reversed(range(n_sub)):
            a = jnp.exp2(lz[j] + (cum[j] + run))
            if diag:
                a = jnp.where(masks[j], a, 0.0)
            a_parts[j] = a.astype(BF16)
            run = run + tot[j]
        carry_ref[...] = run
        vs = pl.multiple_of(c * tq, tq)
        acc_ref[...] += jnp.dot(jnp.concatenate(a_parts, axis=1), v_ref[pl.ds(vs, tq), :],
                                preferred_element_type=F32)

    chunk(i, True)

    def body(s, carry):
        chunk(i - 1 - s, False)
        return carry

    lax.fori_loop(0, i, body, 0)
    o_ref[...] = acc_ref[...].astype(o_ref.dtype)


def _cumsum_matrix():
    j = jnp.arange(LANES)[:, None]
    s = jnp.arange(LANES)[None, :]
    u = (j > s).astype(BF16)
    return jnp.concatenate([u, jnp.ones((LANES, LANES), BF16)], axis=1)


def _sb_attention(sq, skT, sv, bsz, seq):
    t = bsz * seq
    tq = min(CHUNK, seq)
    nq = seq // tq
    return pl.pallas_call(
        functools.partial(_sb_kernel, tq=tq),
        grid=(bsz, SB_HEADS, nq),
        in_specs=[
            pl.BlockSpec((tq, SB_HEAD_DIM), lambda b, h, i: (b * nq + i, h)),
            pl.BlockSpec((None, None, nq, SB_HEAD_DIM, tq), lambda b, h, i: (b, h, 0, 0, 0)),
            pl.BlockSpec((None, seq, SB_HEAD_DIM), lambda b, h, i: (b, 0, h)),
            pl.BlockSpec((LANES, 2 * LANES), lambda b, h, i: (0, 0)),
        ],
        out_specs=pl.BlockSpec((tq, SB_HEAD_DIM), lambda b, h, i: (b * nq + i, h)),
        out_shape=jax.ShapeDtypeStruct((t, SB_WIDTH), BF16),
        scratch_shapes=[pltpu.VMEM((tq, SB_HEAD_DIM), F32), pltpu.VMEM((tq, LANES), F32)],
        compiler_params=_cparams(("parallel", "parallel", "arbitrary")),
        name="stickbreak",
    )(sq, skT, sv, _cumsum_matrix())


CB_HALO = 32
CC_HALO = 8
CONV_ROWS = 64


def _conv_kernel(ga_ref, gb_ref, hga_ref, hgb_ref, cb_ref, cc_ref, cx_ref, hcc_ref, hcx_ref,
                 wb_ref, bb_ref, lng_ref, lnb_ref, wc_ref, ob_ref, oc_ref, buf_ref, u_ref, buf2_ref, *, ts, seq):
    i = pl.program_id(0)
    keep = jnp.where((i * ts) % seq == 0, 0.0, 1.0)

    buf_ref[0:CB_HALO, :] = hga_ref[...] * jax.nn.sigmoid(hgb_ref[...]) * keep
    buf_ref[CB_HALO:, :] = ga_ref[...] * jax.nn.sigmoid(gb_ref[...])
    for r in range(ts // CONV_ROWS):
        for g in range(CB_WIDTH // LANES):
            ls = slice(g * LANES, (g + 1) * LANES)
            acc = jnp.broadcast_to(bb_ref[:, ls], (CONV_ROWS, LANES))
            for k in range(CB_CONV):
                base = r * CONV_ROWS + CB_HALO - (CB_CONV - 1) + k
                acc = acc + wb_ref[k:k + 1, ls] * buf_ref[base:base + CONV_ROWS, ls]
            u_ref[r * CONV_ROWS:(r + 1) * CONV_ROWS, ls] = acc
    u = u_ref[...]
    mu = jnp.mean(u, axis=-1, keepdims=True)
    uc = u - mu
    y = uc * lax.rsqrt(jnp.mean(uc * uc, axis=-1, keepdims=True) + NORM_EPS) * lng_ref[...] + lnb_ref[...]
    ob_ref[...] = (y * jax.nn.sigmoid(y)).astype(ob_ref.dtype)

    buf2_ref[0:CC_HALO, :] = hcc_ref[...] * hcx_ref[...] * keep
    buf2_ref[CC_HALO:, :] = cc_ref[...] * cx_ref[...]
    conv = jnp.zeros((ts, CC_WIDTH), F32)
    for k in range(CC_CONV):
        base = CC_HALO - (CC_CONV - 1) + k
        conv = conv + wc_ref[k:k + 1, :] * buf2_ref[base:base + ts, :]
    oc_ref[...] = (cb_ref[...] * conv).astype(oc_ref.dtype)


def _conv_mixers(proj, wb, bb, lng, lnb, wc, seq, ts):
    t = proj.shape[0]
    w = CB_WIDTH
    c0 = OFF_GLU // w
    c1 = OFF_CC // w
    cur = lambda col: pl.BlockSpec((ts, w), lambda i: (i, col))
    halo = lambda rows, col: pl.BlockSpec((rows, w), lambda i: (jnp.maximum(i * (ts // rows) - 1, 0), col))
    full = lambda r: pl.BlockSpec((r, w), lambda i: (0, 0))
    wb_p = jnp.pad(wb, ((0, CB_HALO - CB_CONV), (0, 0)))
    wc_p = jnp.pad(wc, ((0, SUBLANES - CC_CONV), (0, 0)))
    return pl.pallas_call(
        functools.partial(_conv_kernel, ts=ts, seq=seq),
        grid=(t // ts,),
        in_specs=[cur(c0), cur(c0 + 1), halo(CB_HALO, c0), halo(CB_HALO, c0 + 1),
                  cur(c1), cur(c1 + 1), cur(c1 + 2), halo(CC_HALO, c1 + 1), halo(CC_HALO, c1 + 2),
                  full(CB_HALO), full(1), full(1), full(1), full(SUBLANES)],
        out_specs=[pl.BlockSpec((ts, w), lambda i: (i, 0))] * 2,
        out_shape=[jax.ShapeDtypeStruct((t, w), BF16)] * 2,
        scratch_shapes=[pltpu.VMEM((CB_HALO + ts, w), F32), pltpu.VMEM((ts, w), F32),
                        pltpu.VMEM((CC_HALO + ts, w), F32)],
        compiler_params=_cparams(("parallel",)),
        name="conv_mixers",
    )(proj, proj, proj, proj, proj, proj, proj, proj, proj,
      wb_p, bb.reshape(1, w), lng.reshape(1, w), lnb.reshape(1, w), wc_p)


def _merge_kernel(x_ref, oa_ref, ob_ref, oc_ref, od_ref, g0_ref, g1_ref, g2_ref, g3_ref, bg_ref,
                  wa_ref, wb_ref, wc_ref, wd_ref, wm_ref, o_ref):
    merged = None
    branches = ((oa_ref, wa_ref, g0_ref), (ob_ref, wb_ref, g1_ref), (oc_ref, wc_ref, g2_ref), (od_ref, wd_ref, g3_ref))
    for n, (b_ref, w_ref, g_ref) in enumerate(branches):
        y = jnp.dot(b_ref[...], w_ref[...], preferred_element_type=F32)
        term = jax.nn.sigmoid(g_ref[...] + bg_ref[n:n + 1, :]) * y
        merged = term if merged is None else merged + term
    o_ref[...] = x_ref[...] + jnp.dot(merged.astype(BF16), wm_ref[...], preferred_element_type=F32)


def _merge(x2d, oa, ob, oc, od, proj, b_gate, w_oa, w_ob, w_oc, w_od, w_merge, tm):
    t, d = x2d.shape
    g0 = OFF_GATE // d
    row = lambda i: (i, 0)
    fixed = lambda i: (0, 0)
    br = pl.BlockSpec((tm, A_WIDTH), row)
    gate = lambda n: pl.BlockSpec((tm, d), lambda i: (i, g0 + n))
    wsp = pl.BlockSpec((A_WIDTH, d), fixed)
    bf = lambda w: w.astype(BF16)
    return pl.pallas_call(
        _merge_kernel,
        grid=(t // tm,),
        in_specs=[pl.BlockSpec((tm, d), row), br, br, br, br, gate(0), gate(1), gate(2), gate(3),
                  pl.BlockSpec((N_BRANCH, d), fixed), wsp, wsp, wsp, wsp, pl.BlockSpec((d, d), fixed)],
        out_specs=pl.BlockSpec((tm, d), row),
        out_shape=jax.ShapeDtypeStruct((t, d), F32),
        compiler_params=_cparams(("parallel",)),
        name="merge",
    )(x2d, oa, ob, oc, od, proj, proj, proj, proj, b_gate.reshape(N_BRANCH, d),
      bf(w_oa), bf(w_ob), bf(w_oc), bf(w_od), bf(w_merge))


FFN_HALO = 8


def _ffn_kernel(x_ref, hx_ref, g_ref, wg_ref, wu_ref, cw_ref, wd_ref, o_ref, h_ref, gp_ref, acc_ref, *, tm, seq):
    i = pl.program_id(0)
    f = pl.program_id(1)

    def norm(x):
        ms = jnp.mean(x * x, axis=-1, keepdims=True)
        return (x * lax.rsqrt(ms + NORM_EPS) * g_ref[...]).astype(BF16)

    @pl.when(f == 0)
    def _():
        h_ref[0:FFN_HALO, :] = norm(hx_ref[...])
        h_ref[FFN_HALO:, :] = norm(x_ref[...])
        acc_ref[...] = x_ref[...]

    keep = jnp.where((i * tm) % seq == 0, 0.0, 1.0)
    gp_ref[0:FFN_HALO, :] = jnp.dot(h_ref[0:FFN_HALO, :], wg_ref[...], preferred_element_type=F32) * keep
    gp_ref[FFN_HALO:, :] = jnp.dot(h_ref[FFN_HALO:, :], wg_ref[...], preferred_element_type=F32)
    gt = jnp.zeros((tm, gp_ref.shape[1]), F32)
    for k in range(FFN_CONV):
        base = FFN_HALO - (FFN_CONV - 1) + k
        gt = gt + cw_ref[k:k + 1, :] * gp_ref[base:base + tm, :]
    up = jnp.dot(h_ref[FFN_HALO:, :], wu_ref[...], preferred_element_type=F32)
    act = (gt * jax.nn.sigmoid(gt) * up).astype(BF16)
    acc_ref[...] += jnp.dot(act, wd_ref[...], preferred_element_type=F32)

    @pl.when(f == pl.num_programs(1) - 1)
    def _():
        o_ref[...] = acc_ref[...]


def _ffn(x2d, g, wg, wu, cw, wd, seq, tm, tf):
    t, d = x2d.shape
    ff = wg.shape[1]
    cw_p = jnp.pad(cw, ((0, SUBLANES - FFN_CONV), (0, 0)))
    return pl.pallas_call(
        functools.partial(_ffn_kernel, tm=tm, seq=seq),
        grid=(t // tm, ff // tf),
        in_specs=[
            pl.BlockSpec((tm, d), lambda i, f: (i, 0)),
            pl.BlockSpec((FFN_HALO, d), lambda i, f: (jnp.maximum(i * (tm // FFN_HALO) - 1, 0), 0)),
            pl.BlockSpec((1, d), lambda i, f: (0, 0)),
            pl.BlockSpec((d, tf), lambda i, f: (0, f)),
            pl.BlockSpec((d, tf), lambda i, f: (0, f)),
            pl.BlockSpec((SUBLANES, tf), lambda i, f: (0, f)),
            pl.BlockSpec((tf, d), lambda i, f: (f, 0)),
        ],
        out_specs=pl.BlockSpec((tm, d), lambda i, f: (i, 0)),
        out_shape=jax.ShapeDtypeStruct((t, d), F32),
        scratch_shapes=[pltpu.VMEM((FFN_HALO + tm, d), BF16), pltpu.VMEM((FFN_HALO + tm, tf), F32),
                        pltpu.VMEM((tm, d), F32)],
        compiler_params=_cparams(("parallel", "arbitrary")),
        name="ffn",
    )(x2d, x2d, g.reshape(1, d), wg.astype(BF16), wu.astype(BF16), cw_p, wd.astype(BF16))


def _tiles(seq):
    big = min(seq, 512)
    return dict(
        inproj_tm=min(seq, 1024), inproj_tn=1024,
        conv_ts=big, merge_tm=big, ffn_tm=big, ffn_tf=D_FF // 2,
        dsa_tq=min(seq, 256),
    )


def kernel(x, positions, g_mix, w_in, b_gate, g_qa, g_ka, g_kidx, w_oa, cb_conv_w, cb_conv_b, cb_ln_g, cb_ln_b,
           w_ob, cc_conv_w, w_oc, w_od, w_merge, g_ffn, w_ffn_gate, w_ffn_up, ffn_conv_w, w_ffn_down):
    bsz, seq, dm = x.shape
    depth = w_in.shape[0]
    assert dm == D_MODEL and seq % min(CHUNK, seq) == 0 and seq % LANES == 0
    tl = _tiles(seq)
    t = bsz * seq

    cos_a, sin_a = _rope_tables(positions, A_HEAD_DIM)
    cos_i, sin_i = _rope_tables(positions, IDX_DIM)
    ca, sa = _rope_lane_tables(cos_a, sin_a, A_HEAD_DIM)
    ci, si = _rope_lane_tables(cos_i, sin_i, IDX_DIM)

    x2d = x.reshape(t, dm)
    for l in range(depth):
        proj = _inproj(x2d, g_mix[l], _pack_w_in(w_in[l]), tl["inproj_tm"], tl["inproj_tn"])
        qT, qiT, wT, ki, ka, vT, sq, skT, sv = _prep(proj, ca, sa, ci, si, g_qa[l], g_ka[l], g_kidx[l], bsz, seq)
        o_a = _dsa_attention(qT, qiT, wT, ki, ka, vT, bsz, seq, tl["dsa_tq"])
        o_d = _sb_attention(sq, skT, sv, bsz, seq)
        o_b, o_c = _conv_mixers(proj, cb_conv_w[l], cb_conv_b[l], cb_ln_g[l], cb_ln_b[l], cc_conv_w[l],
                                seq, tl["conv_ts"])
        x2d = _merge(x2d, o_a, o_b, o_c, o_d, proj, b_gate[l], w_oa[l], w_ob[l], w_oc[l], w_od[l], w_merge[l],
                     tl["merge_tm"])
        x2d = _ffn(x2d, g_ffn[l], w_ffn_gate[l], w_ffn_up[l], ffn_conv_w[l], w_ffn_down[l], seq,
                   tl["ffn_tm"], tl["ffn_tf"])
    return x2d.reshape(bsz, seq, dm)
```

```python
import functools
import math

import jax
import jax.numpy as jnp
from jax import lax
from jax.experimental import pallas as pl
from jax.experimental.pallas import tpu as pltpu

F32 = jnp.float32
BF16 = jnp.bfloat16
I32 = jnp.int32

D_MODEL = 1024
N_BRANCH = 4
A_HEADS = 4
A_HEAD_DIM = 128
A_WIDTH = A_HEADS * A_HEAD_DIM
IDX_HEADS = 4
IDX_DIM = 64
TOPK_MAX = 256
CB_WIDTH = 512
CB_CONV = 31
CC_WIDTH = 512
CC_CONV = 3
SB_HEADS = 4
SB_HEAD_DIM = 128
SB_WIDTH = SB_HEADS * SB_HEAD_DIM
D_FF = 2816
FFN_CONV = 3
ROPE_THETA = 500000.0
ROT_FRACTION_DIV = 4
NORM_EPS = 1e-6

LANES = 128
SUBLANES = 8
VMEM_LIMIT = 56 * 1024 * 1024

LOG2E = math.log2(math.e)
NEG_BIG = -1e30
M_INIT = -1e20
F32_LOWEST = float(jnp.finfo(jnp.float32).min)
KEY_MIN_NORMAL = 0x00800000
TRI_PAD = 16

OFF_AQ = 0
OFF_AK = 512
OFF_AV = 1024
OFF_IQ = 1536
OFF_IK = 1792
OFF_IW = 1920
OFF_GLU = 2048
OFF_CC = 3072
OFF_SB = 4608
OFF_GATE = 6144
N_PACK = 10240
PREP_W = 2048
CHUNK = 512


def _cparams(sem):
    return pltpu.CompilerParams(dimension_semantics=sem, vmem_limit_bytes=VMEM_LIMIT)


def _trig_kernel(pos_ref, invf_ref, cos_ref, sin_ref):
    ang = pos_ref[...] * invf_ref[...]
    cos_ref[...] = jnp.cos(ang)
    sin_ref[...] = jnp.sin(ang)


def _rope_tables(positions, head_dim):
    rot = head_dim // ROT_FRACTION_DIV
    half = rot // 2
    inv_freq = ROPE_THETA ** (-(jnp.arange(half, dtype=F32) * 2.0) / rot)
    t = positions.size
    rows = t * half // LANES
    pos_rep = jnp.broadcast_to(positions.astype(F32).reshape(t, 1), (t, half)).reshape(rows, LANES)
    invf = jnp.tile(inv_freq, LANES // half).reshape(1, LANES)
    tr = min(rows, 512)
    cos, sin = pl.pallas_call(
        _trig_kernel,
        grid=(rows // tr,),
        in_specs=[pl.BlockSpec((tr, LANES), lambda i: (i, 0)), pl.BlockSpec((1, LANES), lambda i: (0, 0))],
        out_specs=[pl.BlockSpec((tr, LANES), lambda i: (i, 0))] * 2,
        out_shape=[jax.ShapeDtypeStruct((rows, LANES), F32)] * 2,
        compiler_params=_cparams(("parallel",)),
        name="rope_trig",
    )(pos_rep, invf)
    return cos.reshape(t, half), sin.reshape(t, half)


def _rope_lane_tables(cos, sin, head_dim):
    t, half = cos.shape
    rest = head_dim - 2 * half
    c = jnp.concatenate([cos, cos, jnp.ones((t, rest), F32)], axis=1)
    s = jnp.concatenate([sin, sin, jnp.zeros((t, rest), F32)], axis=1)
    reps = LANES // head_dim
    return jnp.tile(c, (1, reps)), jnp.tile(s, (1, reps))


def _inproj_kernel(x_ref, g_ref, w_ref, o_ref, h_ref):
    @pl.when(pl.program_id(1) == 0)
    def _():
        x = x_ref[...]
        ms = jnp.mean(x * x, axis=-1, keepdims=True)
        h_ref[...] = (x * lax.rsqrt(ms + NORM_EPS) * g_ref[...]).astype(BF16)

    o_ref[...] = jnp.dot(h_ref[...], w_ref[...], preferred_element_type=F32)


def _inproj(x2d, g, w_packed, tm, tn):
    t, d = x2d.shape
    n = w_packed.shape[1]
    return pl.pallas_call(
        _inproj_kernel,
        grid=(t // tm, n // tn),
        in_specs=[
            pl.BlockSpec((tm, d), lambda i, j: (i, 0)),
            pl.BlockSpec((1, d), lambda i, j: (0, 0)),
            pl.BlockSpec((d, tn), lambda i, j: (0, j)),
        ],
        out_specs=pl.BlockSpec((tm, tn), lambda i, j: (i, j)),
        out_shape=jax.ShapeDtypeStruct((t, n), F32),
        scratch_shapes=[pltpu.VMEM((tm, d), BF16)],
        compiler_params=_cparams(("parallel", "arbitrary")),
        name="inproj",
    )(x2d, g.reshape(1, d), w_packed)


N_IN = 10052
OFF_IW_SRC = 1856
OFF_REST_SRC = 1860
PACK_FIRST = OFF_REST_SRC // LANES
PACK_ROT = (OFF_GLU - OFF_REST_SRC) % LANES


def _pack_kernel(a_ref, b_ref, o_ref):
    n = pl.program_id(0)
    row = lax.broadcasted_iota(I32, a_ref.shape, 0)
    a = a_ref[...]

    @pl.when(n < OFF_IK // LANES)
    def _():
        o_ref[...] = a.T.astype(BF16)

    @pl.when(n == OFF_IK // LANES)
    def _():
        o_ref[...] = jnp.where(row < IDX_DIM, a, 0.0).T.astype(BF16)

    @pl.when(n == OFF_IW // LANES)
    def _():
        iw = pltpu.roll(a, LANES - (OFF_IW_SRC - OFF_IK), 0)
        o_ref[...] = jnp.where(row < IDX_HEADS, iw, 0.0).T.astype(BF16)

    @pl.when(n >= OFF_GLU // LANES)
    def _():
        out = jnp.where(row < PACK_ROT, pltpu.roll(a, PACK_ROT, 0), pltpu.roll(b_ref[...], PACK_ROT, 0))
        o_ref[...] = out.T.astype(BF16)


def _pack_w_in(w_t):
    d = w_t.shape[1]
    first_rest = OFF_GLU // LANES

    def a_idx(n):
        return (jnp.where(n < first_rest, jnp.minimum(n, PACK_FIRST), n - first_rest + PACK_FIRST), 0)

    def b_idx(n):
        return (jnp.where(n < first_rest, PACK_FIRST, n - first_rest + PACK_FIRST + 1), 0)

    return pl.pallas_call(
        _pack_kernel,
        grid=(N_PACK // LANES,),
        in_specs=[pl.BlockSpec((LANES, d), a_idx), pl.BlockSpec((LANES, d), b_idx)],
        out_specs=pl.BlockSpec((d, LANES), lambda n: (0, n)),
        out_shape=jax.ShapeDtypeStruct((d, N_PACK), BF16),
        compiler_params=_cparams(("parallel",)),
        name="pack_w_in",
    )(w_t, w_t)


def _rotate_half(x, half, period):
    lane = lax.broadcasted_iota(I32, x.shape, 1)
    up = pltpu.roll(x, LANES - half, 1)
    dn = pltpu.roll(x, half, 1)
    return jnp.where((lane % period) < half, -up, dn)


def _prep_kernel(p_ref, s_ref, ca_ref, sa_ref, ci_ref, si_ref, gq_ref, gk_ref, gi_ref,
                 qT_ref, qiT_ref, wT_ref, ki_ref, ka_ref, vT_ref, sq_ref, skT_ref, sv_ref):
    ca, sa, ci, si = ca_ref[...], sa_ref[...], ci_ref[...], si_ref[...]
    a_half = A_HEAD_DIM // ROT_FRACTION_DIV // 2
    i_half = IDX_DIM // ROT_FRACTION_DIV // 2

    def head_norm(x, g, n_real):
        ms = jnp.sum(x * x, axis=-1, keepdims=True) * (1.0 / n_real)
        return x * lax.rsqrt(ms + NORM_EPS) * g

    for h in range(A_HEADS):
        sl = slice(h * A_HEAD_DIM, (h + 1) * A_HEAD_DIM)
        q = head_norm(p_ref[:, OFF_AQ + h * 128:OFF_AQ + (h + 1) * 128], gq_ref[...], A_HEAD_DIM)
        q = q * ca + _rotate_half(q, a_half, A_HEAD_DIM) * sa
        qT_ref[sl, :] = (q * (A_HEAD_DIM ** -0.5 * LOG2E)).T.astype(BF16)
        k = head_norm(p_ref[:, OFF_AK + h * 128:OFF_AK + (h + 1) * 128], gk_ref[...], A_HEAD_DIM)
        ka_ref[:, sl] = (k * ca + _rotate_half(k, a_half, A_HEAD_DIM) * sa).astype(BF16)
        vT_ref[sl, :] = p_ref[:, OFF_AV + h * 128:OFF_AV + (h + 1) * 128].T.astype(BF16)

    row = lax.broadcasted_iota(I32, (LANES, p_ref.shape[0]), 0)
    for j in range(IDX_HEADS * IDX_DIM // LANES):
        qi = p_ref[:, OFF_IQ + j * LANES:OFF_IQ + (j + 1) * LANES]
        qt = (qi * ci + _rotate_half(qi, i_half, IDX_DIM) * si).T
        qiT_ref[(2 * j) * LANES:(2 * j + 1) * LANES, :] = jnp.where(row < IDX_DIM, qt, 0.0).astype(BF16)
        qiT_ref[(2 * j + 1) * LANES:(2 * j + 2) * LANES, :] = jnp.where(row >= IDX_DIM, qt, 0.0).astype(BF16)
    ki = head_norm(p_ref[:, OFF_IK:OFF_IK + LANES], gi_ref[...], IDX_DIM)
    ki = ki * ci + _rotate_half(ki, i_half, IDX_DIM) * si
    ki_ref[...] = (ki + pltpu.roll(ki, IDX_DIM, 1)).astype(BF16)
    w = p_ref[:, OFF_IW:OFF_IW + LANES] * (IDX_DIM ** -0.5 * IDX_HEADS ** -0.5)
    wT_ref[...] = w.T[0:SUBLANES, :]

    sq_ref[...] = (s_ref[:, 0:SB_WIDTH] * (SB_HEAD_DIM ** -0.5 * LOG2E)).astype(BF16)
    for h in range(SB_HEADS):
        skT_ref[h] = s_ref[:, SB_WIDTH + h * 128:SB_WIDTH + (h + 1) * 128].T.astype(BF16)
    sv_ref[...] = s_ref[:, 2 * SB_WIDTH:3 * SB_WIDTH].astype(BF16)


def _prep(proj, ca, sa, ci, si, g_qa, g_ka, g_kidx, bsz, seq):
    t = proj.shape[0]
    ck = min(CHUNK, seq)
    nck = seq // ck
    gi = jnp.concatenate([g_kidx, jnp.ones((LANES - IDX_DIM,), F32)]).reshape(1, LANES)
    tok = lambda b, i: (b * nck + i, 0)
    fixed = lambda b, i: (0, 0)
    tbl = pl.BlockSpec((ck, LANES), tok)
    gsp = pl.BlockSpec((1, LANES), fixed)
    sb_col = OFF_SB // (3 * SB_WIDTH)
    return pl.pallas_call(
        _prep_kernel,
        grid=(bsz, nck),
        in_specs=[pl.BlockSpec((ck, PREP_W), tok), pl.BlockSpec((ck, 3 * SB_WIDTH), lambda b, i: (b * nck + i, sb_col)),
                  tbl, tbl, tbl, tbl, gsp, gsp, gsp],
        out_specs=[
            pl.BlockSpec((None, A_WIDTH, ck), lambda b, i: (b, 0, i)),
            pl.BlockSpec((None, IDX_HEADS * LANES, ck), lambda b, i: (b, 0, i)),
            pl.BlockSpec((None, SUBLANES, ck), lambda b, i: (b, 0, i)),
            pl.BlockSpec((None, ck, LANES), lambda b, i: (b, i, 0)),
            pl.BlockSpec((None, ck, A_WIDTH), lambda b, i: (b, i, 0)),
            pl.BlockSpec((None, None, A_WIDTH, ck), lambda b, i: (b, i, 0, 0)),
            pl.BlockSpec((ck, SB_WIDTH), tok),
            pl.BlockSpec((None, SB_HEADS, None, SB_HEAD_DIM, ck), lambda b, i: (b, 0, i, 0, 0)),
            pl.BlockSpec((None, ck, SB_WIDTH), lambda b, i: (b, i, 0)),
        ],
        out_shape=[
            jax.ShapeDtypeStruct((bsz, A_WIDTH, seq), BF16),
            jax.ShapeDtypeStruct((bsz, IDX_HEADS * LANES, seq), BF16),
            jax.ShapeDtypeStruct((bsz, SUBLANES, seq), F32),
            jax.ShapeDtypeStruct((bsz, seq, LANES), BF16),
            jax.ShapeDtypeStruct((bsz, seq, A_WIDTH), BF16),
            jax.ShapeDtypeStruct((bsz, nck, A_WIDTH, ck), BF16),
            jax.ShapeDtypeStruct((t, SB_WIDTH), BF16),
            jax.ShapeDtypeStruct((bsz, SB_HEADS, nck, SB_HEAD_DIM, ck), BF16),
            jax.ShapeDtypeStruct((bsz, seq, SB_WIDTH), BF16),
        ],
        compiler_params=_cparams(("parallel", "parallel")),
        name="mixer_prep",
    )(proj, proj, ca, sa, ci, si, g_qa.reshape(1, LANES), g_ka.reshape(1, LANES), gi)


def _order_key(f):
    b = pltpu.bitcast(f, I32)
    return b ^ ((b >> 31) & 0x7FFFFFFF)


def _key_float(k):
    return pltpu.bitcast(k ^ ((k >> 31) & 0x7FFFFFFF), F32)


def _normal_key(k):
    k = jnp.where(jnp.logical_and(k > 0, k < KEY_MIN_NORMAL), KEY_MIN_NORMAL, k)
    return jnp.where(jnp.logical_and(k < 0, k >= -KEY_MIN_NORMAL), 0, k)


def _dsa_kernel(qT_ref, qiT_ref, wT_ref, ki_ref, ka_ref, vT_ref, tri_ref, o_ref, sc_ref, red_ref, m_ref, l_ref,
                acc_ref, *, tq, ck, grp, k_sel):
    i = pl.program_id(1)
    n_keys = (i + 1) * tq
    n_chunks = (n_keys + ck - 1) // ck
    n_groups = (n_keys + grp - 1) // grp
    q_pos = i * tq + lax.broadcasted_iota(I32, (1, tq), 1)
    neg_inf = jnp.full((ck, tq), -jnp.inf, F32)

    def chunk_start(c):
        return pl.multiple_of(c * ck, ck)

    def key_pos(c):
        return c * ck + lax.broadcasted_iota(I32, (ck, tq), 0)

    def score_chunk(c, diag):
        ks = chunk_start(c)
        kic = ki_ref[pl.ds(ks, ck), :]
        sc = None
        for h in range(IDX_HEADS):
            d = jnp.dot(kic, qiT_ref[h * LANES:(h + 1) * LANES, :], preferred_element_type=F32)
            term = jnp.maximum(d, 0.0) * wT_ref[h:h + 1, :]
            sc = term if sc is None else sc + term
        if diag:
            causal = key_pos(c) <= q_pos
            sc_hi = jnp.where(causal, sc, -jnp.inf)
            sc_lo = jnp.where(causal, sc, jnp.inf)
        else:
            sc_hi = sc_lo = sc
        sc_ref[pl.ds(ks, ck), :] = sc_hi
        fold = lambda a, op: op(a.reshape(ck // SUBLANES, SUBLANES, tq), axis=0)
        return fold(sc_hi, jnp.max), fold(sc_lo, jnp.min)

    def score_body(c, carry):
        mx, mn = score_chunk(c, False)
        return jnp.maximum(carry[0], mx), jnp.minimum(carry[1], mn)

    init = (jnp.full((SUBLANES, tq), -jnp.inf, F32), jnp.full((SUBLANES, tq), jnp.inf, F32))
    mx, mn = lax.fori_loop(0, n_chunks - 1, score_body, init)
    mx_d, mn_d = score_chunk(n_chunks - 1, True)
    def fold_rows(x, op):
        red_ref[...] = x
        r = red_ref[0:1, :]
        for j in range(1, SUBLANES):
            r = op(r, red_ref[j:j + 1, :])
        return r

    row_max = fold_rows(jnp.maximum(mx, mx_d), jnp.maximum)
    row_min = fold_rows(jnp.minimum(mn, mn_d), jnp.minimum)

    def pad_body(c, carry):
        sc_ref[pl.ds(chunk_start(c), ck), :] = neg_inf
        return carry
    lax.fori_loop(n_chunks, n_groups * (grp // ck), pad_body, 0)

    ones = jnp.ones((SUBLANES, grp), BF16)
    max_groups = sc_ref.shape[0] // grp

    def count_ge(t):
        for ng in range(1, max_groups + 1):
            @pl.when(n_groups == ng)
            def _(ng=ng):
                acc = None
                for g in range(ng):
                    ind = jnp.where(sc_ref[g * grp:(g + 1) * grp, :] >= t, 1.0, 0.0).astype(BF16)
                    part = jnp.dot(ones, ind, preferred_element_type=F32)
                    acc = part if acc is None else acc + part
                red_ref[...] = acc
        return red_ref[0:1, :].astype(I32)

    n_causal = q_pos + 1
    take_all = n_causal <= k_sel
    lo0 = _normal_key(_order_key(row_min))
    hi0 = _normal_key(_order_key(row_max)) + 1
    thr0 = jnp.full((1, tq), F32_LOWEST, F32)
    done0 = jnp.where(take_all, 1, 0)
    zero = jnp.zeros((1, tq), I32)

    def search_cond(st):
        it, _, _, _, _, done, _ = st
        return jnp.logical_and(it < 40, jnp.min(done) == 0)

    def search_body(st):
        it, lo, hi, c_hi, thr, done, tied = st
        cross = jnp.logical_and(lo < 0, hi > 0)
        gap = hi - lo
        zero_probe = jnp.logical_and(it == 1, lo == 0)
        mid = _normal_key(jnp.where(cross, 0, jnp.where(zero_probe, KEY_MIN_NORMAL, lo + (gap >> 1))))
        no_pivot = jnp.logical_or(gap == 1, jnp.logical_or(mid >= hi, mid <= lo))
        adjacent = jnp.logical_and(no_pivot, jnp.logical_not(cross))
        pivot = _key_float(mid)
        cnt = count_ge(pivot)
        live = done == 0
        hit = jnp.logical_and(live, jnp.logical_and(cnt == k_sel, jnp.logical_not(adjacent)))
        end = jnp.logical_and(live, adjacent)
        move = jnp.logical_and(live, jnp.logical_not(jnp.logical_or(hit, end)))
        up = jnp.logical_and(move, cnt > k_sel)
        dn = jnp.logical_and(move, cnt < k_sel)
        thr = jnp.where(hit, pivot, jnp.where(end, _key_float(lo), thr))
        tied = jnp.where(end, 1, tied)
        done = jnp.where(jnp.logical_or(hit, end), 1, done)
        lo = jnp.where(up, mid, lo)
        hi = jnp.where(dn, mid, hi)
        c_hi = jnp.where(dn, cnt, c_hi)
        return it + 1, lo, hi, c_hi, thr, done, tied

    st = lax.while_loop(search_cond, search_body, (jnp.int32(0), lo0, hi0, zero, thr0, done0, zero))
    _, _, _, c_hi, thr, _, tied = st

    room = jnp.where(tied > 0, k_sel - c_hi, jnp.int32(2 ** 30)).astype(F32)

    @pl.when(jnp.max(tied) > 0)
    def _():
        red_ref[0:1, :] = jnp.zeros((1, tq), F32)

        def drop_excess(c, carry):
            ks = chunk_start(c)
            s = sc_ref[pl.ds(ks, ck), :]
            is_tied = s == thr
            ind = jnp.where(is_tied, 1.0, 0.0).astype(BF16)
            pre = jnp.dot(tri_ref[...], ind, preferred_element_type=F32)
            before = red_ref[0:1, :]
            rank = pre[0:ck, :] + before
            sc_ref[pl.ds(ks, ck), :] = jnp.where(is_tied, jnp.where(rank >= room, -jnp.inf, s), s)
            red_ref[0:1, :] = before + pre[ck:ck + 1, :]
            return carry
        lax.fori_loop(0, n_chunks, drop_excess, 0)

    m_ref[...] = jnp.full(m_ref.shape, M_INIT, F32)
    l_ref[...] = jnp.zeros(l_ref.shape, F32)
    acc_ref[...] = jnp.zeros(acc_ref.shape, F32)
    heads = [slice(h * A_HEAD_DIM, (h + 1) * A_HEAD_DIM) for h in range(A_HEADS)]

    def att_body(c, carry):
        ks = chunk_start(c)
        sel = sc_ref[pl.ds(ks, ck), :] >= thr
        logits = [jnp.dot(ka_ref[pl.ds(ks, ck), hs], qT_ref[hs, :], preferred_element_type=F32) for hs in heads]
        for h, hs in enumerate(heads):
            lg = jnp.where(sel, logits[h], NEG_BIG)
            m_old = m_ref[h, 0:1, :]
            m_new = jnp.maximum(m_old, jnp.max(lg, axis=0, keepdims=True))
            alpha = jnp.exp2(m_old - m_new)
            p = jnp.exp2(lg - m_new)
            l_ref[h, 0:1, :] = alpha * l_ref[h, 0:1, :] + jnp.sum(p, axis=0, keepdims=True)
            m_ref[h, 0:1, :] = m_new
            pv = jnp.dot(vT_ref[c, hs, :], p.astype(BF16), preferred_element_type=F32)
            acc_ref[h] = acc_ref[h] * alpha + pv
        return carry

    lax.fori_loop(0, n_chunks, att_body, 0)
    for h, hs in enumerate(heads):
        o_ref[:, hs] = (acc_ref[h] / l_ref[h, 0:1, :]).T.astype(o_ref.dtype)


def _dsa_attention(qT, qiT, wT, ki, ka, vT, bsz, seq, tq):
    ck = min(CHUNK, seq)
    nck = seq // ck
    grp = min(4 * ck, seq)
    k_sel = min(TOPK_MAX, seq // 4)
    kern = functools.partial(_dsa_kernel, tq=tq, ck=ck, grp=grp, k_sel=k_sel)
    once = pl.Buffered(1)
    out = pl.pallas_call(
        kern,
        grid=(bsz, seq // tq),
        in_specs=[
            pl.BlockSpec((None, A_WIDTH, tq), lambda b, i: (b, 0, i)),
            pl.BlockSpec((None, IDX_HEADS * LANES, tq), lambda b, i: (b, 0, i)),
            pl.BlockSpec((None, SUBLANES, tq), lambda b, i: (b, 0, i)),
            pl.BlockSpec((None, seq, LANES), lambda b, i: (b, 0, 0), pipeline_mode=once),
            pl.BlockSpec((None, seq, A_WIDTH), lambda b, i: (b, 0, 0), pipeline_mode=once),
            pl.BlockSpec((None, nck, A_WIDTH, ck), lambda b, i: (b, 0, 0, 0), pipeline_mode=once),
            pl.BlockSpec((ck + TRI_PAD, ck), lambda b, i: (0, 0), pipeline_mode=once),
        ],
        out_specs=pl.BlockSpec((None, tq, A_WIDTH), lambda b, i: (b, i, 0)),
        out_shape=jax.ShapeDtypeStruct((bsz, seq, A_WIDTH), BF16),
        scratch_shapes=[pltpu.VMEM((seq, tq), F32), pltpu.VMEM((SUBLANES, tq), F32),
                        pltpu.VMEM((A_HEADS, SUBLANES, tq), F32), pltpu.VMEM((A_HEADS, SUBLANES, tq), F32),
                        pltpu.VMEM((A_HEADS, A_HEAD_DIM, tq), F32)],
        compiler_params=_cparams(("parallel", "arbitrary")),
        name="dsa_attention",
    )(qT, qiT, wT, ki, ka, vT, _prefix_matrix(ck))
    return out.reshape(bsz * seq, A_WIDTH)


def _prefix_matrix(n):
    r = jnp.arange(n + TRI_PAD)[:, None]
    c = jnp.arange(n)[None, :]
    return jnp.logical_or(c < r, r >= n).astype(BF16)


def _sb_kernel(q_ref, kT_ref, v_ref, uu_ref, o_ref, acc_ref, carry_ref, *, tq):
    i = pl.program_id(2)
    q = q_ref[...]
    acc_ref[...] = jnp.zeros(acc_ref.shape, F32)
    carry_ref[...] = jnp.zeros(carry_ref.shape, F32)
    n_sub = tq // LANES
    row = lax.broadcasted_iota(I32, (tq, LANES), 0)
    col = lax.broadcasted_iota(I32, (tq, LANES), 1)

    def chunk(c, diag):
        z = jnp.dot(q, kT_ref[c], preferred_element_type=F32)
        lz, cum, tot, masks = [], [], [], []
        for j in range(n_sub):
            zj = z[:, j * LANES:(j + 1) * LANES]
            neg_abs = pltpu.bitcast(pltpu.bitcast(zj, I32) | jnp.int32(-2 ** 31), F32)
            neg_soft = jnp.log(1.0 + jnp.exp2(neg_abs)) * (-LOG2E)
            lk = neg_soft - jnp.maximum(zj, 0.0)
            mask = None
            if diag:
                mask = (col + j * LANES) < row
                lk = jnp.where(mask, lk, 0.0)
            r = jnp.dot(lk.astype(BF16), uu_ref[...], preferred_element_type=F32)
            lz.append(zj + lk)
            cum.append(r[:, :LANES])
            tot.append(r[:, LANES:])
            masks.append(mask)
        run = carry_ref[...]
        a_parts = [None] * n_sub
        for j in reversed(range(n_sub)):
            a = jnp.exp2(lz[j] + (cum[j] + run))
            if diag:
                a = jnp.where(masks[j], a, 0.0)
            a_parts[j] = a.astype(BF16)
            run = run + tot[j]
        carry_ref[...] = run
        vs = pl.multiple_of(c * tq, tq)
        acc_ref[...] += jnp.dot(jnp.concatenate(a_parts, axis=1), v_ref[pl.ds(vs, tq), :],
                                preferred_element_type=F32)

    chunk(i, True)

    def body(s, carry):
        chunk(i - 1 - s, False)
        return carry

    lax.fori_loop(0, i, body, 0)
    o_ref[...] = acc_ref[...].astype(o_ref.dtype)


def _cumsum_matrix():
    j = jnp.arange(LANES)[:, None]
    s = jnp.arange(LANES)[None, :]
    u = (j > s).astype(BF16)
    return jnp.concatenate([u, jnp.ones((LANES, LANES), BF16)], axis=1)


def _sb_attention(sq, skT, sv, bsz, seq):
    t = bsz * seq
    tq = min(CHUNK, seq)
    nq = seq // tq
    return pl.pallas_call(
        functools.partial(_sb_kernel, tq=tq),
        grid=(bsz, SB_HEADS, nq),
        in_specs=[
            pl.BlockSpec((tq, SB_HEAD_DIM), lambda b, h, i: (b * nq + i, h)),
            pl.BlockSpec((None, None, nq, SB_HEAD_DIM, tq), lambda b, h, i: (b, h, 0, 0, 0)),
            pl.BlockSpec((None, seq, SB_HEAD_DIM), lambda b, h, i: (b, 0, h)),
            pl.BlockSpec((LANES, 2 * LANES), lambda b, h, i: (0, 0)),
        ],
        out_specs=pl.BlockSpec((tq, SB_HEAD_DIM), lambda b, h, i: (b * nq + i, h)),
        out_shape=jax.ShapeDtypeStruct((t, SB_WIDTH), BF16),
        scratch_shapes=[pltpu.VMEM((tq, SB_HEAD_DIM), F32), pltpu.VMEM((tq, LANES), F32)],
        compiler_params=_cparams(("parallel", "parallel", "arbitrary")),
        name="stickbreak",
    )(sq, skT, sv, _cumsum_matrix())


CB_HALO = 32
CC_HALO = 8
CONV_ROWS = 64


def _conv_kernel(ga_ref, gb_ref, hga_ref, hgb_ref, cb_ref, cc_ref, cx_ref, hcc_ref, hcx_ref,
                 wb_ref, bb_ref, lng_ref, lnb_ref, wc_ref, ob_ref, oc_ref, buf_ref, u_ref, buf2_ref, *, ts, seq):
    i = pl.program_id(0)
    keep = jnp.where((i * ts) % seq == 0, 0.0, 1.0)

    buf_ref[0:CB_HALO, :] = hga_ref[...] * jax.nn.sigmoid(hgb_ref[...]) * keep
    buf_ref[CB_HALO:, :] = ga_ref[...] * jax.nn.sigmoid(gb_ref[...])
    for r in range(ts // CONV_ROWS):
        for g in range(CB_WIDTH // LANES):
            ls = slice(g * LANES, (g + 1) * LANES)
            acc = jnp.broadcast_to(bb_ref[:, ls], (CONV_ROWS, LANES))
            for k in range(CB_CONV):
                base = r * CONV_ROWS + CB_HALO - (CB_CONV - 1) + k
                acc = acc + wb_ref[k:k + 1, ls] * buf_ref[base:base + CONV_ROWS, ls]
            u_ref[r * CONV_ROWS:(r + 1) * CONV_ROWS, ls] = acc
    u = u_ref[...]
    mu = jnp.mean(u, axis=-1, keepdims=True)
    uc = u - mu
    y = uc * lax.rsqrt(jnp.mean(uc * uc, axis=-1, keepdims=True) + NORM_EPS) * lng_ref[...] + lnb_ref[...]
    ob_ref[...] = (y * jax.nn.sigmoid(y)).astype(ob_ref.dtype)

    buf2_ref[0:CC_HALO, :] = hcc_ref[...] * hcx_ref[...] * keep
    buf2_ref[CC_HALO:, :] = cc_ref[...] * cx_ref[...]
    conv = jnp.zeros((ts, CC_WIDTH), F32)
    for k in range(CC_CONV):
        base = CC_HALO - (CC_CONV - 1) + k
        conv = conv + wc_ref[k:k + 1, :] * buf2_ref[base:base + ts, :]
    oc_ref[...] = (cb_ref[...] * conv).astype(oc_ref.dtype)


def _conv_mixers(proj, wb, bb, lng, lnb, wc, seq, ts):
    t = proj.shape[0]
    w = CB_WIDTH
    c0 = OFF_GLU // w
    c1 = OFF_CC // w
    cur = lambda col: pl.BlockSpec((ts, w), lambda i: (i, col))
    halo = lambda rows, col: pl.BlockSpec((rows, w), lambda i: (jnp.maximum(i * (ts // rows) - 1, 0), col))
    full = lambda r: pl.BlockSpec((r, w), lambda i: (0, 0))
    wb_p = jnp.pad(wb, ((0, CB_HALO - CB_CONV), (0, 0)))
    wc_p = jnp.pad(wc, ((0, SUBLANES - CC_CONV), (0, 0)))
    return pl.pallas_call(
        functools.partial(_conv_kernel, ts=ts, seq=seq),
        grid=(t // ts,),
        in_specs=[cur(c0), cur(c0 + 1), halo(CB_HALO, c0), halo(CB_HALO, c0 + 1),
                  cur(c1), cur(c1 + 1), cur(c1 + 2), halo(CC_HALO, c1 + 1), halo(CC_HALO, c1 + 2),
                  full(CB_HALO), full(1), full(1), full(1), full(SUBLANES)],
        out_specs=[pl.BlockSpec((ts, w), lambda i: (i, 0))] * 2,
        out_shape=[jax.ShapeDtypeStruct((t, w), BF16)] * 2,
        scratch_shapes=[pltpu.VMEM((CB_HALO + ts, w), F32), pltpu.VMEM((ts, w), F32),
                        pltpu.VMEM((CC_HALO + ts, w), F32)],
        compiler_params=_cparams(("parallel",)),
        name="conv_mixers",
    )(proj, proj, proj, proj, proj, proj, proj, proj, proj,
      wb_p, bb.reshape(1, w), lng.reshape(1, w), lnb.reshape(1, w), wc_p)


def _merge_kernel(x_ref, oa_ref, ob_ref, oc_ref, od_ref, g0_ref, g1_ref, g2_ref, g3_ref, bg_ref,
                  wa_ref, wb_ref, wc_ref, wd_ref, wm_ref, o_ref):
    merged = None
    branches = ((oa_ref, wa_ref, g0_ref), (ob_ref, wb_ref, g1_ref), (oc_ref, wc_ref, g2_ref), (od_ref, wd_ref, g3_ref))
    for n, (b_ref, w_ref, g_ref) in enumerate(branches):
        y = jnp.dot(b_ref[...], w_ref[...], preferred_element_type=F32)
        term = jax.nn.sigmoid(g_ref[...] + bg_ref[n:n + 1, :]) * y
        merged = term if merged is None else merged + term
    o_ref[...] = x_ref[...] + jnp.dot(merged.astype(BF16), wm_ref[...], preferred_element_type=F32)


def _merge(x2d, oa, ob, oc, od, proj, b_gate, w_oa, w_ob, w_oc, w_od, w_merge, tm):
    t, d = x2d.shape
    g0 = OFF_GATE // d
    row = lambda i: (i, 0)
    fixed = lambda i: (0, 0)
    br = pl.BlockSpec((tm, A_WIDTH), row)
    gate = lambda n: pl.BlockSpec((tm, d), lambda i: (i, g0 + n))
    wsp = pl.BlockSpec((A_WIDTH, d), fixed)
    bf = lambda w: w.astype(BF16)
    return pl.pallas_call(
        _merge_kernel,
        grid=(t // tm,),
        in_specs=[pl.BlockSpec((tm, d), row), br, br, br, br, gate(0), gate(1), gate(2), gate(3),
                  pl.BlockSpec((N_BRANCH, d), fixed), wsp, wsp, wsp, wsp, pl.BlockSpec((d, d), fixed)],
        out_specs=pl.BlockSpec((tm, d), row),
        out_shape=jax.ShapeDtypeStruct((t, d), F32),
        compiler_params=_cparams(("parallel",)),
        name="merge",
    )(x2d, oa, ob, oc, od, proj, proj, proj, proj, b_gate.reshape(N_BRANCH, d),
      bf(w_oa), bf(w_ob), bf(w_oc), bf(w_od), bf(w_merge))


FFN_HALO = 8


def _ffn_kernel(x_ref, hx_ref, g_ref, wg_ref, wu_ref, cw_ref, wd_ref, o_ref, h_ref, gp_ref, acc_ref, *, tm, seq):
    i = pl.program_id(0)
    f = pl.program_id(1)

    def norm(x):
        ms = jnp.mean(x * x, axis=-1, keepdims=True)
        return (x * lax.rsqrt(ms + NORM_EPS) * g_ref[...]).astype(BF16)

    @pl.when(f == 0)
    def _():
        h_ref[0:FFN_HALO, :] = norm(hx_ref[...])
        h_ref[FFN_HALO:, :] = norm(x_ref[...])
        acc_ref[...] = x_ref[...]

    keep = jnp.where((i * tm) % seq == 0, 0.0, 1.0)
    gp_ref[0:FFN_HALO, :] = jnp.dot(h_ref[0:FFN_HALO, :], wg_ref[...], preferred_element_type=F32) * keep
    gp_ref[FFN_HALO:, :] = jnp.dot(h_ref[FFN_HALO:, :], wg_ref[...], preferred_element_type=F32)
    gt = jnp.zeros((tm, gp_ref.shape[1]), F32)
    for k in range(FFN_CONV):
        base = FFN_HALO - (FFN_CONV - 1) + k
        gt = gt + cw_ref[k:k + 1, :] * gp_ref[base:base + tm, :]
    up = jnp.dot(h_ref[FFN_HALO:, :], wu_ref[...], preferred_element_type=F32)
    act = (gt * jax.nn.sigmoid(gt) * up).astype(BF16)
    acc_ref[...] += jnp.dot(act, wd_ref[...], preferred_element_type=F32)

    @pl.when(f == pl.num_programs(1) - 1)
    def _():
        o_ref[...] = acc_ref[...]


def _ffn(x2d, g, wg, wu, cw, wd, seq, tm, tf):
    t, d = x2d.shape
    ff = wg.shape[1]
    cw_p = jnp.pad(cw, ((0, SUBLANES - FFN_CONV), (0, 0)))
    return pl.pallas_call(
        functools.partial(_ffn_kernel, tm=tm, seq=seq),
        grid=(t // tm, ff // tf),
        in_specs=[
            pl.BlockSpec((tm, d), lambda i, f: (i, 0)),
            pl.BlockSpec((FFN_HALO, d), lambda i, f: (jnp.maximum(i * (tm // FFN_HALO) - 1, 0), 0)),
            pl.BlockSpec((1, d), lambda i, f: (0, 0)),
            pl.BlockSpec((d, tf), lambda i, f: (0, f)),
            pl.BlockSpec((d, tf), lambda i, f: (0, f)),
            pl.BlockSpec((SUBLANES, tf), lambda i, f: (0, f)),
            pl.BlockSpec((tf, d), lambda i, f: (f, 0)),
        ],
        out_specs=pl.BlockSpec((tm, d), lambda i, f: (i, 0)),
        out_shape=jax.ShapeDtypeStruct((t, d), F32),
        scratch_shapes=[pltpu.VMEM((FFN_HALO + tm, d), BF16), pltpu.VMEM((FFN_HALO + tm, tf), F32),
                        pltpu.VMEM((tm, d), F32)],
        compiler_params=_cparams(("parallel", "arbitrary")),
        name="ffn",
    )(x2d, x2d, g.reshape(1, d), wg.astype(BF16), wu.astype(BF16), cw_p, wd.astype(BF16))


def _tiles(seq):
    big = min(seq, 512)
    return dict(
        inproj_tm=min(seq, 1024), inproj_tn=1024,
        conv_ts=big, merge_tm=big, ffn_tm=big, ffn_tf=D_FF // 2,
        dsa_tq=min(seq, 256),
    )


def kernel(x, positions, g_mix, w_in, b_gate, g_qa, g_ka, g_kidx, w_oa, cb_conv_w, cb_conv_b, cb_ln_g, cb_ln_b,
           w_ob, cc_conv_w, w_oc, w_od, w_merge, g_ffn, w_ffn_gate, w_ffn_up, ffn_conv_w, w_ffn_down):
    bsz, seq, dm = x.shape
    depth = w_in.shape[0]
    assert dm == D_MODEL and seq % min(CHUNK, seq) == 0 and seq % LANES == 0
    tl = _tiles(seq)
    t = bsz * seq

    cos_a, sin_a = _rope_tables(positions, A_HEAD_DIM)
    cos_i, sin_i = _rope_tables(positions, IDX_DIM)
    ca, sa = _rope_lane_tables(cos_a, sin_a, A_HEAD_DIM)
    ci, si = _rope_lane_tables(cos_i, sin_i, IDX_DIM)

    x2d = x.reshape(t, dm)
    for l in range(depth):
        w_packed = _pack_w_in(jnp.swapaxes(w_in[l], 0, 1))
        proj = _inproj(x2d, g_mix[l], w_packed, tl["inproj_tm"], tl["inproj_tn"])
        qT, qiT, wT, ki, ka, vT, sq, skT, sv = _prep(proj, ca, sa, ci, si, g_qa[l], g_ka[l], g_kidx[l], bsz, seq)
        o_a = _dsa_attention(qT, qiT, wT, ki, ka, vT, bsz, seq, tl["dsa_tq"])
        o_d = _sb_attention(sq, skT, sv, bsz, seq)
        o_b, o_c = _conv_mixers(proj, cb_conv_w[l], cb_conv_b[l], cb_ln_g[l], cb_ln_b[l], cc_conv_w[l],
                                seq, tl["conv_ts"])
        x2d = _merge(x2d, o_a, o_b, o_c, o_d, proj, b_gate[l], w_oa[l], w_ob[l], w_oc[l], w_od[l], w_merge[l],
                     tl["merge_tm"])
        x2d = _ffn(x2d, g_ffn[l], w_ffn_gate[l], w_ffn_up[l], ffn_conv_w[l], w_ffn_down[l], seq,
                   tl["ffn_tm"], tl["ffn_tf"])
    return x2d.reshape(bsz, seq, dm)
```

```python
import functools
import math

import jax
import jax.numpy as jnp
from jax import lax
from jax.experimental import pallas as pl
from jax.experimental.pallas import tpu as pltpu

F32 = jnp.float32
BF16 = jnp.bfloat16
I32 = jnp.int32

D_MODEL = 1024
N_BRANCH = 4
A_HEADS = 4
A_HEAD_DIM = 128
A_WIDTH = A_HEADS * A_HEAD_DIM
IDX_HEADS = 4
IDX_DIM = 64
TOPK_MAX = 256
CB_WIDTH = 512
CB_CONV = 31
CC_WIDTH = 512
CC_CONV = 3
SB_HEADS = 4
SB_HEAD_DIM = 128
SB_WIDTH = SB_HEADS * SB_HEAD_DIM
D_FF = 2816
FFN_CONV = 3
ROPE_THETA = 500000.0
ROT_FRACTION_DIV = 4
NORM_EPS = 1e-6

LANES = 128
SUBLANES = 8
VMEM_LIMIT = 56 * 1024 * 1024

LOG2E = math.log2(math.e)
NEG_BIG = -1e30
M_INIT = -1e20
F32_LOWEST = float(jnp.finfo(jnp.float32).min)
KEY_MIN_NORMAL = 0x00800000
TRI_PAD = 16
VALUE_PASSES = 12
SEARCH_PASS_CAP = VALUE_PASSES + 36

OFF_AQ = 0
OFF_AK = 512
OFF_AV = 1024
OFF_IQ = 1536
OFF_IK = 1792
OFF_IW = 1920
OFF_GLU = 2048
OFF_CC = 3072
OFF_SB = 4608
OFF_GATE = 6144
N_PACK = 10240
SPLIT_COL = 2048
CHUNK = 512


def _cparams(sem):
    return pltpu.CompilerParams(dimension_semantics=sem, vmem_limit_bytes=VMEM_LIMIT)


def _trig_kernel(pos_ref, invf_ref, cos_ref, sin_ref):
    ang = pos_ref[...] * invf_ref[...]
    cos_ref[...] = jnp.cos(ang)
    sin_ref[...] = jnp.sin(ang)


def _rope_tables(positions, head_dim):
    rot = head_dim // ROT_FRACTION_DIV
    half = rot // 2
    inv_freq = ROPE_THETA ** (-(jnp.arange(half, dtype=F32) * 2.0) / rot)
    t = positions.size
    rows = t * half // LANES
    pos_rep = jnp.broadcast_to(positions.astype(F32).reshape(t, 1), (t, half)).reshape(rows, LANES)
    invf = jnp.tile(inv_freq, LANES // half).reshape(1, LANES)
    tr = min(rows, 512)
    cos, sin = pl.pallas_call(
        _trig_kernel,
        grid=(rows // tr,),
        in_specs=[pl.BlockSpec((tr, LANES), lambda i: (i, 0)), pl.BlockSpec((1, LANES), lambda i: (0, 0))],
        out_specs=[pl.BlockSpec((tr, LANES), lambda i: (i, 0))] * 2,
        out_shape=[jax.ShapeDtypeStruct((rows, LANES), F32)] * 2,
        compiler_params=_cparams(("parallel",)),
        name="rope_trig",
    )(pos_rep, invf)
    return cos.reshape(t, half), sin.reshape(t, half)


def _rope_lane_tables(cos, sin, head_dim):
    t, half = cos.shape
    rest = head_dim - 2 * half
    c = jnp.concatenate([cos, cos, jnp.ones((t, rest), F32)], axis=1)
    s = jnp.concatenate([sin, sin, jnp.zeros((t, rest), F32)], axis=1)
    reps = LANES // head_dim
    return jnp.tile(c, (1, reps)), jnp.tile(s, (1, reps))


def _inproj_kernel(x_ref, g_ref, w_ref, o_ref, h_ref):
    @pl.when(pl.program_id(1) == 0)
    def _():
        x = x_ref[...]
        ms = jnp.mean(x * x, axis=-1, keepdims=True)
        h_ref[...] = (x * lax.rsqrt(ms + NORM_EPS) * g_ref[...]).astype(BF16)

    o_ref[...] = jnp.dot(h_ref[...], w_ref[...], preferred_element_type=F32).astype(o_ref.dtype)


def _inproj(x2d, g, w_packed, tm, tn, col0, ncols, out_dtype):
    t, d = x2d.shape
    j0 = col0 // tn
    return pl.pallas_call(
        _inproj_kernel,
        grid=(t // tm, ncols // tn),
        in_specs=[
            pl.BlockSpec((tm, d), lambda i, j: (i, 0)),
            pl.BlockSpec((1, d), lambda i, j: (0, 0)),
            pl.BlockSpec((d, tn), lambda i, j: (0, j + j0)),
        ],
        out_specs=pl.BlockSpec((tm, tn), lambda i, j: (i, j)),
        out_shape=jax.ShapeDtypeStruct((t, ncols), out_dtype),
        scratch_shapes=[pltpu.VMEM((tm, d), BF16)],
        compiler_params=_cparams(("parallel", "arbitrary")),
        name="inproj",
    )(x2d, g.reshape(1, d), w_packed)


N_IN = 10052
OFF_IW_SRC = 1856
OFF_REST_SRC = 1860
PACK_FIRST = OFF_REST_SRC // LANES
PACK_ROT = (OFF_GLU - OFF_REST_SRC) % LANES


def _pack_kernel(a_ref, b_ref, o_ref):
    n = pl.program_id(0)
    row = lax.broadcasted_iota(I32, a_ref.shape, 0)
    a = a_ref[...]

    @pl.when(n < OFF_IK // LANES)
    def _():
        o_ref[...] = a.T.astype(BF16)

    @pl.when(n == OFF_IK // LANES)
    def _():
        o_ref[...] = jnp.where(row < IDX_DIM, a, 0.0).T.astype(BF16)

    @pl.when(n == OFF_IW // LANES)
    def _():
        iw = pltpu.roll(a, LANES - (OFF_IW_SRC - OFF_IK), 0)
        o_ref[...] = jnp.where(row < IDX_HEADS, iw, 0.0).T.astype(BF16)

    @pl.when(n >= OFF_GLU // LANES)
    def _():
        out = jnp.where(row < PACK_ROT, pltpu.roll(a, PACK_ROT, 0), pltpu.roll(b_ref[...], PACK_ROT, 0))
        o_ref[...] = out.T.astype(BF16)


def _pack_w_in(w_t):
    d = w_t.shape[1]
    first_rest = OFF_GLU // LANES

    def a_idx(n):
        return (jnp.where(n < first_rest, jnp.minimum(n, PACK_FIRST), n - first_rest + PACK_FIRST), 0)

    def b_idx(n):
        return (jnp.where(n < first_rest, PACK_FIRST, n - first_rest + PACK_FIRST + 1), 0)

    return pl.pallas_call(
        _pack_kernel,
        grid=(N_PACK // LANES,),
        in_specs=[pl.BlockSpec((LANES, d), a_idx), pl.BlockSpec((LANES, d), b_idx)],
        out_specs=pl.BlockSpec((d, LANES), lambda n: (0, n)),
        out_shape=jax.ShapeDtypeStruct((d, N_PACK), BF16),
        compiler_params=_cparams(("parallel",)),
        name="pack_w_in",
    )(w_t, w_t)


def _rotate_half(x, half, period):
    lane = lax.broadcasted_iota(I32, x.shape, 1)
    up = pltpu.roll(x, LANES - half, 1)
    dn = pltpu.roll(x, half, 1)
    return jnp.where((lane % period) < half, -up, dn)


def _prep_kernel(p_ref, sqi_ref, ski_ref, svi_ref, ca_ref, sa_ref, ci_ref, si_ref, gq_ref, gk_ref, gi_ref,
                 qT_ref, qiT_ref, wT_ref, ki_ref, ka_ref, vT_ref, sq_ref, skT_ref, sv_ref):
    ca, sa, ci, si = ca_ref[...], sa_ref[...], ci_ref[...], si_ref[...]
    a_half = A_HEAD_DIM // ROT_FRACTION_DIV // 2
    i_half = IDX_DIM // ROT_FRACTION_DIV // 2

    def head_norm(x, g, n_real):
        ms = jnp.sum(x * x, axis=-1, keepdims=True) * (1.0 / n_real)
        return x * lax.rsqrt(ms + NORM_EPS) * g

    for h in range(A_HEADS):
        sl = slice(h * A_HEAD_DIM, (h + 1) * A_HEAD_DIM)
        q = head_norm(p_ref[:, OFF_AQ + h * 128:OFF_AQ + (h + 1) * 128], gq_ref[...], A_HEAD_DIM)
        q = q * ca + _rotate_half(q, a_half, A_HEAD_DIM) * sa
        qT_ref[sl, :] = (q * (A_HEAD_DIM ** -0.5 * LOG2E)).T.astype(BF16)
        k = head_norm(p_ref[:, OFF_AK + h * 128:OFF_AK + (h + 1) * 128], gk_ref[...], A_HEAD_DIM)
        ka_ref[:, sl] = (k * ca + _rotate_half(k, a_half, A_HEAD_DIM) * sa).astype(BF16)
        vT_ref[sl, :] = p_ref[:, OFF_AV + h * 128:OFF_AV + (h + 1) * 128].T.astype(BF16)

    row = lax.broadcasted_iota(I32, (LANES, p_ref.shape[0]), 0)
    for j in range(IDX_HEADS * IDX_DIM // LANES):
        qi = p_ref[:, OFF_IQ + j * LANES:OFF_IQ + (j + 1) * LANES]
        qt = (qi * ci + _rotate_half(qi, i_half, IDX_DIM) * si).T
        qiT_ref[(2 * j) * LANES:(2 * j + 1) * LANES, :] = jnp.where(row < IDX_DIM, qt, 0.0).astype(BF16)
        qiT_ref[(2 * j + 1) * LANES:(2 * j + 2) * LANES, :] = jnp.where(row >= IDX_DIM, qt, 0.0).astype(BF16)
    ki = head_norm(p_ref[:, OFF_IK:OFF_IK + LANES], gi_ref[...], IDX_DIM)
    ki = ki * ci + _rotate_half(ki, i_half, IDX_DIM) * si
    ki_ref[...] = (ki + pltpu.roll(ki, IDX_DIM, 1)).astype(BF16)
    w = p_ref[:, OFF_IW:OFF_IW + LANES] * (IDX_DIM ** -0.5 * IDX_HEADS ** -0.5)
    wT_ref[...] = w.T[0:SUBLANES, :]

    sq_ref[...] = (sqi_ref[...].astype(F32) * (SB_HEAD_DIM ** -0.5 * LOG2E)).astype(BF16)
    for h in range(SB_HEADS):
        skT_ref[h] = ski_ref[:, h * 128:(h + 1) * 128].astype(F32).T.astype(BF16)
    sv_ref[...] = svi_ref[...]


def _prep(proj_a, proj_b, ca, sa, ci, si, g_qa, g_ka, g_kidx, bsz, seq):
    t = proj_a.shape[0]
    ck = min(CHUNK, seq)
    nck = seq // ck
    gi = jnp.concatenate([g_kidx, jnp.ones((LANES - IDX_DIM,), F32)]).reshape(1, LANES)
    tok = lambda b, i: (b * nck + i, 0)
    fixed = lambda b, i: (0, 0)
    tbl = pl.BlockSpec((ck, LANES), tok)
    gsp = pl.BlockSpec((1, LANES), fixed)
    sb_col = (OFF_SB - SPLIT_COL) // SB_WIDTH
    sb_in = lambda n: pl.BlockSpec((ck, SB_WIDTH), lambda b, i: (b * nck + i, sb_col + n))
    return pl.pallas_call(
        _prep_kernel,
        grid=(bsz, nck),
        in_specs=[pl.BlockSpec((ck, SPLIT_COL), tok), sb_in(0), sb_in(1), sb_in(2),
                  tbl, tbl, tbl, tbl, gsp, gsp, gsp],
        out_specs=[
            pl.BlockSpec((None, A_WIDTH, ck), lambda b, i: (b, 0, i)),
            pl.BlockSpec((None, IDX_HEADS * LANES, ck), lambda b, i: (b, 0, i)),
            pl.BlockSpec((None, SUBLANES, ck), lambda b, i: (b, 0, i)),
            pl.BlockSpec((None, ck, LANES), lambda b, i: (b, i, 0)),
            pl.BlockSpec((None, ck, A_WIDTH), lambda b, i: (b, i, 0)),
            pl.BlockSpec((None, None, A_WIDTH, ck), lambda b, i: (b, i, 0, 0)),
            pl.BlockSpec((ck, SB_WIDTH), tok),
            pl.BlockSpec((None, SB_HEADS, None, SB_HEAD_DIM, ck), lambda b, i: (b, 0, i, 0, 0)),
            pl.BlockSpec((None, ck, SB_WIDTH), lambda b, i: (b, i, 0)),
        ],
        out_shape=[
            jax.ShapeDtypeStruct((bsz, A_WIDTH, seq), BF16),
            jax.ShapeDtypeStruct((bsz, IDX_HEADS * LANES, seq), BF16),
            jax.ShapeDtypeStruct((bsz, SUBLANES, seq), F32),
            jax.ShapeDtypeStruct((bsz, seq, LANES), BF16),
            jax.ShapeDtypeStruct((bsz, seq, A_WIDTH), BF16),
            jax.ShapeDtypeStruct((bsz, nck, A_WIDTH, ck), BF16),
            jax.ShapeDtypeStruct((t, SB_WIDTH), BF16),
            jax.ShapeDtypeStruct((bsz, SB_HEADS, nck, SB_HEAD_DIM, ck), BF16),
            jax.ShapeDtypeStruct((bsz, seq, SB_WIDTH), BF16),
        ],
        compiler_params=_cparams(("parallel", "parallel")),
        name="mixer_prep",
    )(proj_a, proj_b, proj_b, proj_b, ca, sa, ci, si, g_qa.reshape(1, LANES), g_ka.reshape(1, LANES), gi)


def _order_key(f):
    b = pltpu.bitcast(f, I32)
    return b ^ ((b >> 31) & 0x7FFFFFFF)


def _key_float(k):
    return pltpu.bitcast(k ^ ((k >> 31) & 0x7FFFFFFF), F32)


def _normal_key(k):
    k = jnp.where(jnp.logical_and(k > 0, k < KEY_MIN_NORMAL), KEY_MIN_NORMAL, k)
    return jnp.where(jnp.logical_and(k < 0, k >= -KEY_MIN_NORMAL), 0, k)


def _dsa_kernel(qT_ref, qiT_ref, wT_ref, ki_ref, ka_ref, vT_ref, tri_ref, o_ref, sc_ref, red_ref, m_ref, l_ref,
                acc_ref, *, tq, ck, grp, k_sel):
    i = pl.program_id(1)
    n_keys = (i + 1) * tq
    n_chunks = (n_keys + ck - 1) // ck
    n_groups = (n_keys + grp - 1) // grp
    q_pos = i * tq + lax.broadcasted_iota(I32, (1, tq), 1)
    neg_inf = jnp.full((ck, tq), -jnp.inf, F32)

    def chunk_start(c):
        return pl.multiple_of(c * ck, ck)

    def key_pos(c):
        return c * ck + lax.broadcasted_iota(I32, (ck, tq), 0)

    def score_chunk(c, diag):
        ks = chunk_start(c)
        kic = ki_ref[pl.ds(ks, ck), :]
        sc = None
        for h in range(IDX_HEADS):
            d = jnp.dot(kic, qiT_ref[h * LANES:(h + 1) * LANES, :], preferred_element_type=F32)
            term = jnp.maximum(d, 0.0) * wT_ref[h:h + 1, :]
            sc = term if sc is None else sc + term
        if diag:
            causal = key_pos(c) <= q_pos
            sc_hi = jnp.where(causal, sc, -jnp.inf)
            sc_lo = jnp.where(causal, sc, jnp.inf)
        else:
            sc_hi = sc_lo = sc
        sc_ref[pl.ds(ks, ck), :] = sc_hi
        fold = lambda a, op: op(a.reshape(ck // SUBLANES, SUBLANES, tq), axis=0)
        return fold(sc_hi, jnp.max), fold(sc_lo, jnp.min)

    def score_body(c, carry):
        mx, mn = score_chunk(c, False)
        return jnp.maximum(carry[0], mx), jnp.minimum(carry[1], mn)

    init = (jnp.full((SUBLANES, tq), -jnp.inf, F32), jnp.full((SUBLANES, tq), jnp.inf, F32))
    mx, mn = lax.fori_loop(0, n_chunks - 1, score_body, init)
    mx_d, mn_d = score_chunk(n_chunks - 1, True)
    def fold_rows(x, op):
        red_ref[...] = x
        r = red_ref[0:1, :]
        for j in range(1, SUBLANES):
            r = op(r, red_ref[j:j + 1, :])
        return r

    row_max = fold_rows(jnp.maximum(mx, mx_d), jnp.maximum)
    row_min = fold_rows(jnp.minimum(mn, mn_d), jnp.minimum)

    def pad_body(c, carry):
        sc_ref[pl.ds(chunk_start(c), ck), :] = neg_inf
        return carry
    lax.fori_loop(n_chunks, n_groups * (grp // ck), pad_body, 0)

    ones = jnp.ones((SUBLANES, grp), BF16)
    max_groups = sc_ref.shape[0] // grp

    def count_ge(t):
        for ng in range(1, max_groups + 1):
            @pl.when(n_groups == ng)
            def _(ng=ng):
                acc = None
                for g in range(ng):
                    ind = jnp.where(sc_ref[g * grp:(g + 1) * grp, :] >= t, 1.0, 0.0).astype(BF16)
                    part = jnp.dot(ones, ind, preferred_element_type=F32)
                    acc = part if acc is None else acc + part
                red_ref[...] = acc
        return red_ref[0:1, :].astype(I32)

    n_causal = q_pos + 1
    take_all = n_causal <= k_sel
    lo0 = _normal_key(_order_key(row_min))
    hi0 = _normal_key(_order_key(row_max)) + 1
    thr0 = jnp.full((1, tq), F32_LOWEST, F32)
    done0 = jnp.where(take_all, 1, 0)
    zero = jnp.zeros((1, tq), I32)

    def search_cond(st):
        it, _, _, _, _, done, _ = st
        return jnp.logical_and(it < SEARCH_PASS_CAP, jnp.min(done) == 0)

    def search_body(st):
        it, lo, hi, c_hi, thr, done, tied = st
        cross = jnp.logical_and(lo < 0, hi > 0)
        gap = hi - lo
        zero_probe = jnp.logical_and(it == 1, lo == 0)
        mid = _normal_key(jnp.where(cross, 0, jnp.where(zero_probe, KEY_MIN_NORMAL, lo + (gap >> 1))))
        v_mid = _normal_key(_order_key(0.5 * _key_float(lo) + 0.5 * _key_float(hi)))
        by_value = jnp.logical_and(jnp.logical_and(it < VALUE_PASSES, jnp.logical_not(jnp.logical_or(cross, zero_probe))),
                                   jnp.logical_and(v_mid > lo, v_mid < hi))
        mid = jnp.where(by_value, v_mid, mid)
        no_pivot = jnp.logical_or(gap == 1, jnp.logical_or(mid >= hi, mid <= lo))
        adjacent = jnp.logical_and(no_pivot, jnp.logical_not(cross))
        pivot = _key_float(mid)
        cnt = count_ge(pivot)
        live = done == 0
        hit = jnp.logical_and(live, jnp.logical_and(cnt == k_sel, jnp.logical_not(adjacent)))
        end = jnp.logical_and(live, adjacent)
        move = jnp.logical_and(live, jnp.logical_not(jnp.logical_or(hit, end)))
        up = jnp.logical_and(move, cnt > k_sel)
        dn = jnp.logical_and(move, cnt < k_sel)
        thr = jnp.where(hit, pivot, jnp.where(end, _key_float(lo), thr))
        tied = jnp.where(end, 1, tied)
        done = jnp.where(jnp.logical_or(hit, end), 1, done)
        lo = jnp.where(up, mid, lo)
        hi = jnp.where(dn, mid, hi)
        c_hi = jnp.where(dn, cnt, c_hi)
        return it + 1, lo, hi, c_hi, thr, done, tied

    st = lax.while_loop(search_cond, search_body, (jnp.int32(0), lo0, hi0, zero, thr0, done0, zero))
    _, _, _, c_hi, thr, _, tied = st

    room = jnp.where(tied > 0, k_sel - c_hi, jnp.int32(2 ** 30)).astype(F32)

    @pl.when(jnp.max(tied) > 0)
    def _():
        red_ref[0:1, :] = jnp.zeros((1, tq), F32)

        def drop_excess(c, carry):
            ks = chunk_start(c)
            s = sc_ref[pl.ds(ks, ck), :]
            is_tied = s == thr
            ind = jnp.where(is_tied, 1.0, 0.0).astype(BF16)
            pre = jnp.dot(tri_ref[...], ind, preferred_element_type=F32)
            before = red_ref[0:1, :]
            rank = pre[0:ck, :] + before
            sc_ref[pl.ds(ks, ck), :] = jnp.where(is_tied, jnp.where(rank >= room, -jnp.inf, s), s)
            red_ref[0:1, :] = before + pre[ck:ck + 1, :]
            return carry
        lax.fori_loop(0, n_chunks, drop_excess, 0)

    m_ref[...] = jnp.full(m_ref.shape, M_INIT, F32)
    l_ref[...] = jnp.zeros(l_ref.shape, F32)
    acc_ref[...] = jnp.zeros(acc_ref.shape, F32)
    heads = [slice(h * A_HEAD_DIM, (h + 1) * A_HEAD_DIM) for h in range(A_HEADS)]

    def att_body(c, carry):
        ks = chunk_start(c)
        sel = sc_ref[pl.ds(ks, ck), :] >= thr
        logits = [jnp.dot(ka_ref[pl.ds(ks, ck), hs], qT_ref[hs, :], preferred_element_type=F32) for hs in heads]
        for h, hs in enumerate(heads):
            lg = jnp.where(sel, logits[h], NEG_BIG)
            m_old = m_ref[h, 0:1, :]
            m_new = jnp.maximum(m_old, jnp.max(lg, axis=0, keepdims=True))
            alpha = jnp.exp2(m_old - m_new)
            p = jnp.exp2(lg - m_new)
            l_ref[h, 0:1, :] = alpha * l_ref[h, 0:1, :] + jnp.sum(p, axis=0, keepdims=True)
            m_ref[h, 0:1, :] = m_new
            pv = jnp.dot(vT_ref[c, hs, :], p.astype(BF16), preferred_element_type=F32)
            acc_ref[h] = acc_ref[h] * alpha + pv
        return carry

    lax.fori_loop(0, n_chunks, att_body, 0)
    for h, hs in enumerate(heads):
        o_ref[:, hs] = (acc_ref[h] / l_ref[h, 0:1, :]).T.astype(o_ref.dtype)


def _dsa_attention(qT, qiT, wT, ki, ka, vT, bsz, seq, tq):
    ck = min(CHUNK, seq)
    nck = seq // ck
    grp = min(2 * ck, seq)
    k_sel = min(TOPK_MAX, seq // 4)
    kern = functools.partial(_dsa_kernel, tq=tq, ck=ck, grp=grp, k_sel=k_sel)
    once = pl.Buffered(1)
    out = pl.pallas_call(
        kern,
        grid=(bsz, seq // tq),
        in_specs=[
            pl.BlockSpec((None, A_WIDTH, tq), lambda b, i: (b, 0, i)),
            pl.BlockSpec((None, IDX_HEADS * LANES, tq), lambda b, i: (b, 0, i)),
            pl.BlockSpec((None, SUBLANES, tq), lambda b, i: (b, 0, i)),
            pl.BlockSpec((None, seq, LANES), lambda b, i: (b, 0, 0), pipeline_mode=once),
            pl.BlockSpec((None, seq, A_WIDTH), lambda b, i: (b, 0, 0), pipeline_mode=once),
            pl.BlockSpec((None, nck, A_WIDTH, ck), lambda b, i: (b, 0, 0, 0), pipeline_mode=once),
            pl.BlockSpec((ck + TRI_PAD, ck), lambda b, i: (0, 0), pipeline_mode=once),
        ],
        out_specs=pl.BlockSpec((None, tq, A_WIDTH), lambda b, i: (b, i, 0)),
        out_shape=jax.ShapeDtypeStruct((bsz, seq, A_WIDTH), BF16),
        scratch_shapes=[pltpu.VMEM((seq, tq), F32), pltpu.VMEM((SUBLANES, tq), F32),
                        pltpu.VMEM((A_HEADS, SUBLANES, tq), F32), pltpu.VMEM((A_HEADS, SUBLANES, tq), F32),
                        pltpu.VMEM((A_HEADS, A_HEAD_DIM, tq), F32)],
        compiler_params=_cparams(("parallel", "arbitrary")),
        name="dsa_attention",
    )(qT, qiT, wT, ki, ka, vT, _prefix_matrix(ck))
    return out.reshape(bsz * seq, A_WIDTH)


def _prefix_matrix(n):
    r = jnp.arange(n + TRI_PAD)[:, None]
    c = jnp.arange(n)[None, :]
    return jnp.logical_or(c < r, r >= n).astype(BF16)


def _sb_kernel(q_ref, kT_ref, v_ref, uu_ref, o_ref, acc_ref, carry_ref, *, tq):
    i = pl.program_id(2)
    q = q_ref[...]
    acc_ref[...] = jnp.zeros(acc_ref.shape, F32)
    carry_ref[...] = jnp.zeros(carry_ref.shape, F32)
    n_sub = tq // LANES
    row = lax.broadcasted_iota(I32, (tq, LANES), 0)
    col = lax.broadcasted_iota(I32, (tq, LANES), 1)

    def chunk(c, diag):
        z = jnp.dot(q, kT_ref[c], preferred_element_type=F32)
        lz, cum, tot, masks = [], [], [], []
        for j in range(n_sub):
            zj = z[:, j * LANES:(j + 1) * LANES]
            neg_abs = pltpu.bitcast(pltpu.bitcast(zj, I32) | jnp.int32(-2 ** 31), F32)
            neg_soft = jnp.log(1.0 + jnp.exp2(neg_abs)) * (-LOG2E)
            lk = neg_soft - jnp.maximum(zj, 0.0)
            mask = None
            if diag:
                mask = (col + j * LANES) < row
                lk = jnp.where(mask, lk, 0.0)
            r = jnp.dot(lk.astype(BF16), uu_ref[...], preferred_element_type=F32)
            lz.append(zj + lk)
            cum.append(r[:, :LANES])
            tot.append(r[:, LANES:])
            masks.append(mask)
        run = carry_ref[...]
        a_parts = [None] * n_sub
        for j in reversed(range(n_sub)):
            a = jnp.exp2(lz[j] + (cum[j] + run))
            if diag:
                a = jnp.where(masks[j], a, 0.0)
            a_parts[j] = a.astype(BF16)
            run = run + tot[j]
        carry_ref[...] = run
        vs = pl.multiple_of(c * tq, tq)
        acc_ref[...] += jnp.dot(jnp.concatenate(a_parts, axis=1), v_ref[pl.ds(vs, tq), :],
                                preferred_element_type=F32)

    chunk(i, True)

    def body(s, carry):
        chunk(i - 1 - s, False)
        return carry

    lax.fori_loop(0, i, body, 0)
    o_ref[...] = acc_ref[...].astype(o_ref.dtype)


def _cumsum_matrix():
    j = jnp.arange(LANES)[:, None]
    s = jnp.arange(LANES)[None, :]
    u = (j > s).astype(BF16)
    return jnp.concatenate([u, jnp.ones((LANES, LANES), BF16)], axis=1)


def _sb_attention(sq, skT, sv, bsz, seq):
    t = bsz * seq
    tq = min(CHUNK, seq)
    nq = seq // tq
    return pl.pallas_call(
        functools.partial(_sb_kernel, tq=tq),
        grid=(bsz, SB_HEADS, nq),
        in_specs=[
            pl.BlockSpec((tq, SB_HEAD_DIM), lambda b, h, i: (b * nq + i, h)),
            pl.BlockSpec((None, None, nq, SB_HEAD_DIM, tq), lambda b, h, i: (b, h, 0, 0, 0)),
            pl.BlockSpec((None, seq, SB_HEAD_DIM), lambda b, h, i: (b, 0, h)),
            pl.BlockSpec((LANES, 2 * LANES), lambda b, h, i: (0, 0)),
        ],
        out_specs=pl.BlockSpec((tq, SB_HEAD_DIM), lambda b, h, i: (b * nq + i, h)),
        out_shape=jax.ShapeDtypeStruct((t, SB_WIDTH), BF16),
        scratch_shapes=[pltpu.VMEM((tq, SB_HEAD_DIM), F32), pltpu.VMEM((tq, LANES), F32)],
        compiler_params=_cparams(("parallel", "parallel", "arbitrary")),
        name="stickbreak",
    )(sq, skT, sv, _cumsum_matrix())


CB_HALO = 32
CC_HALO = 16
CONV_ROWS = 64


def _conv_kernel(ga_ref, gb_ref, hga_ref, hgb_ref, cb_ref, cc_ref, cx_ref, hcc_ref, hcx_ref,
                 wb_ref, bb_ref, lng_ref, lnb_ref, wc_ref, ob_ref, oc_ref, buf_ref, u_ref, buf2_ref, *, ts, seq):
    i = pl.program_id(0)
    keep = jnp.where((i * ts) % seq == 0, 0.0, 1.0)

    f32 = lambda ref: ref[...].astype(F32)
    buf_ref[0:CB_HALO, :] = f32(hga_ref) * jax.nn.sigmoid(f32(hgb_ref)) * keep
    buf_ref[CB_HALO:, :] = f32(ga_ref) * jax.nn.sigmoid(f32(gb_ref))
    for r in range(ts // CONV_ROWS):
        for g in range(CB_WIDTH // LANES):
            ls = slice(g * LANES, (g + 1) * LANES)
            acc = jnp.broadcast_to(bb_ref[:, ls], (CONV_ROWS, LANES))
            for k in range(CB_CONV):
                base = r * CONV_ROWS + CB_HALO - (CB_CONV - 1) + k
                acc = acc + wb_ref[k:k + 1, ls] * buf_ref[base:base + CONV_ROWS, ls]
            u_ref[r * CONV_ROWS:(r + 1) * CONV_ROWS, ls] = acc
    u = u_ref[...]
    mu = jnp.mean(u, axis=-1, keepdims=True)
    uc = u - mu
    y = uc * lax.rsqrt(jnp.mean(uc * uc, axis=-1, keepdims=True) + NORM_EPS) * lng_ref[...] + lnb_ref[...]
    ob_ref[...] = (y * jax.nn.sigmoid(y)).astype(ob_ref.dtype)

    buf2_ref[0:CC_HALO, :] = f32(hcc_ref) * f32(hcx_ref) * keep
    buf2_ref[CC_HALO:, :] = f32(cc_ref) * f32(cx_ref)
    conv = jnp.zeros((ts, CC_WIDTH), F32)
    for k in range(CC_CONV):
        base = CC_HALO - (CC_CONV - 1) + k
        conv = conv + wc_ref[k:k + 1, :] * buf2_ref[base:base + ts, :]
    oc_ref[...] = (f32(cb_ref) * conv).astype(oc_ref.dtype)


def _conv_mixers(proj, wb, bb, lng, lnb, wc, seq, ts):
    t = proj.shape[0]
    w = CB_WIDTH
    c0 = (OFF_GLU - SPLIT_COL) // w
    c1 = (OFF_CC - SPLIT_COL) // w
    cur = lambda col: pl.BlockSpec((ts, w), lambda i: (i, col))
    halo = lambda rows, col: pl.BlockSpec((rows, w), lambda i: (jnp.maximum(i * (ts // rows) - 1, 0), col))
    full = lambda r: pl.BlockSpec((r, w), lambda i: (0, 0))
    wb_p = jnp.pad(wb, ((0, CB_HALO - CB_CONV), (0, 0)))
    wc_p = jnp.pad(wc, ((0, SUBLANES - CC_CONV), (0, 0)))
    return pl.pallas_call(
        functools.partial(_conv_kernel, ts=ts, seq=seq),
        grid=(t // ts,),
        in_specs=[cur(c0), cur(c0 + 1), halo(CB_HALO, c0), halo(CB_HALO, c0 + 1),
                  cur(c1), cur(c1 + 1), cur(c1 + 2), halo(CC_HALO, c1 + 1), halo(CC_HALO, c1 + 2),
                  full(CB_HALO), full(1), full(1), full(1), full(SUBLANES)],
        out_specs=[pl.BlockSpec((ts, w), lambda i: (i, 0))] * 2,
        out_shape=[jax.ShapeDtypeStruct((t, w), BF16)] * 2,
        scratch_shapes=[pltpu.VMEM((CB_HALO + ts, w), F32), pltpu.VMEM((ts, w), F32),
                        pltpu.VMEM((CC_HALO + ts, w), F32)],
        compiler_params=_cparams(("parallel",)),
        name="conv_mixers",
    )(proj, proj, proj, proj, proj, proj, proj, proj, proj,
      wb_p, bb.reshape(1, w), lng.reshape(1, w), lnb.reshape(1, w), wc_p)


def _merge_kernel(x_ref, oa_ref, ob_ref, oc_ref, od_ref, g0_ref, g1_ref, g2_ref, g3_ref, bg_ref,
                  wa_ref, wb_ref, wc_ref, wd_ref, wm_ref, o_ref):
    merged = None
    branches = ((oa_ref, wa_ref, g0_ref), (ob_ref, wb_ref, g1_ref), (oc_ref, wc_ref, g2_ref), (od_ref, wd_ref, g3_ref))
    for n, (b_ref, w_ref, g_ref) in enumerate(branches):
        y = jnp.dot(b_ref[...], w_ref[...], preferred_element_type=F32)
        term = jax.nn.sigmoid(g_ref[...].astype(F32) + bg_ref[n:n + 1, :]) * y
        merged = term if merged is None else merged + term
    o_ref[...] = x_ref[...] + jnp.dot(merged.astype(BF16), wm_ref[...], preferred_element_type=F32)


def _merge(x2d, oa, ob, oc, od, proj, b_gate, w_oa, w_ob, w_oc, w_od, w_merge, tm):
    t, d = x2d.shape
    g0 = (OFF_GATE - SPLIT_COL) // d
    row = lambda i: (i, 0)
    fixed = lambda i: (0, 0)
    br = pl.BlockSpec((tm, A_WIDTH), row)
    gate = lambda n: pl.BlockSpec((tm, d), lambda i: (i, g0 + n))
    wsp = pl.BlockSpec((A_WIDTH, d), fixed)
    bf = lambda w: w.astype(BF16)
    return pl.pallas_call(
        _merge_kernel,
        grid=(t // tm,),
        in_specs=[pl.BlockSpec((tm, d), row), br, br, br, br, gate(0), gate(1), gate(2), gate(3),
                  pl.BlockSpec((N_BRANCH, d), fixed), wsp, wsp, wsp, wsp, pl.BlockSpec((d, d), fixed)],
        out_specs=pl.BlockSpec((tm, d), row),
        out_shape=jax.ShapeDtypeStruct((t, d), F32),
        compiler_params=_cparams(("parallel",)),
        name="merge",
    )(x2d, oa, ob, oc, od, proj, proj, proj, proj, b_gate.reshape(N_BRANCH, d),
      bf(w_oa), bf(w_ob), bf(w_oc), bf(w_od), bf(w_merge))


FFN_HALO = 8


def _ffn_kernel(x_ref, hx_ref, g_ref, wg_ref, wu_ref, cw_ref, wd_ref, o_ref, h_ref, gp_ref, acc_ref, *, tm, seq):
    i = pl.program_id(0)
    f = pl.program_id(1)

    def norm(x):
        ms = jnp.mean(x * x, axis=-1, keepdims=True)
        return (x * lax.rsqrt(ms + NORM_EPS) * g_ref[...]).astype(BF16)

    @pl.when(f == 0)
    def _():
        h_ref[0:FFN_HALO, :] = norm(hx_ref[...])
        h_ref[FFN_HALO:, :] = norm(x_ref[...])
        acc_ref[...] = x_ref[...]

    keep = jnp.where((i * tm) % seq == 0, 0.0, 1.0)
    gp_ref[0:FFN_HALO, :] = jnp.dot(h_ref[0:FFN_HALO, :], wg_ref[...], preferred_element_type=F32) * keep
    gp_ref[FFN_HALO:, :] = jnp.dot(h_ref[FFN_HALO:, :], wg_ref[...], preferred_element_type=F32)
    gt = jnp.zeros((tm, gp_ref.shape[1]), F32)
    for k in range(FFN_CONV):
        base = FFN_HALO - (FFN_CONV - 1) + k
        gt = gt + cw_ref[k:k + 1, :] * gp_ref[base:base + tm, :]
    up = jnp.dot(h_ref[FFN_HALO:, :], wu_ref[...], preferred_element_type=F32)
    act = (gt * jax.nn.sigmoid(gt) * up).astype(BF16)
    acc_ref[...] += jnp.dot(act, wd_ref[...], preferred_element_type=F32)

    @pl.when(f == pl.num_programs(1) - 1)
    def _():
        o_ref[...] = acc_ref[...]


def _ffn(x2d, g, wg, wu, cw, wd, seq, tm, tf):
    t, d = x2d.shape
    ff = wg.shape[1]
    cw_p = jnp.pad(cw, ((0, SUBLANES - FFN_CONV), (0, 0)))
    return pl.pallas_call(
        functools.partial(_ffn_kernel, tm=tm, seq=seq),
        grid=(t // tm, ff // tf),
        in_specs=[
            pl.BlockSpec((tm, d), lambda i, f: (i, 0)),
            pl.BlockSpec((FFN_HALO, d), lambda i, f: (jnp.maximum(i * (tm // FFN_HALO) - 1, 0), 0)),
            pl.BlockSpec((1, d), lambda i, f: (0, 0)),
            pl.BlockSpec((d, tf), lambda i, f: (0, f)),
            pl.BlockSpec((d, tf), lambda i, f: (0, f)),
            pl.BlockSpec((SUBLANES, tf), lambda i, f: (0, f)),
            pl.BlockSpec((tf, d), lambda i, f: (f, 0)),
        ],
        out_specs=pl.BlockSpec((tm, d), lambda i, f: (i, 0)),
        out_shape=jax.ShapeDtypeStruct((t, d), F32),
        scratch_shapes=[pltpu.VMEM((FFN_HALO + tm, d), BF16), pltpu.VMEM((FFN_HALO + tm, tf), F32),
                        pltpu.VMEM((tm, d), F32)],
        compiler_params=_cparams(("parallel", "arbitrary")),
        name="ffn",
    )(x2d, x2d, g.reshape(1, d), wg.astype(BF16), wu.astype(BF16), cw_p, wd.astype(BF16))


def _tiles(seq):
    big = min(seq, 512)
    return dict(
        inproj_tm=min(seq, 1024), inproj_tn=1024,
        conv_ts=big, merge_tm=big, ffn_tm=big, ffn_tf=D_FF // 2,
        dsa_tq=min(seq, 512),
    )


def kernel(x, positions, g_mix, w_in, b_gate, g_qa, g_ka, g_kidx, w_oa, cb_conv_w, cb_conv_b, cb_ln_g, cb_ln_b,
           w_ob, cc_conv_w, w_oc, w_od, w_merge, g_ffn, w_ffn_gate, w_ffn_up, ffn_conv_w, w_ffn_down):
    bsz, seq, dm = x.shape
    depth = w_in.shape[0]
    assert dm == D_MODEL and seq % min(CHUNK, seq) == 0 and seq % LANES == 0
    tl = _tiles(seq)
    t = bsz * seq

    cos_a, sin_a = _rope_tables(positions, A_HEAD_DIM)
    cos_i, sin_i = _rope_tables(positions, IDX_DIM)
    ca, sa = _rope_lane_tables(cos_a, sin_a, A_HEAD_DIM)
    ci, si = _rope_lane_tables(cos_i, sin_i, IDX_DIM)

    x2d = x.reshape(t, dm)
    for l in range(depth):
        w_packed = _pack_w_in(jnp.swapaxes(w_in[l], 0, 1))
        tm, tn = tl["inproj_tm"], tl["inproj_tn"]
        proj_a = _inproj(x2d, g_mix[l], w_packed, tm, tn, 0, SPLIT_COL, F32)
        proj = _inproj(x2d, g_mix[l], w_packed, tm, tn, SPLIT_COL, N_PACK - SPLIT_COL, BF16)
        qT, qiT, wT, ki, ka, vT, sq, skT, sv = _prep(proj_a, proj, ca, sa, ci, si, g_qa[l], g_ka[l], g_kidx[l],
                                                     bsz, seq)
        o_a = _dsa_attention(qT, qiT, wT, ki, ka, vT, bsz, seq, tl["dsa_tq"])
        o_d = _sb_attention(sq, skT, sv, bsz, seq)
        o_b, o_c = _conv_mixers(proj, cb_conv_w[l], cb_conv_b[l], cb_ln_g[l], cb_ln_b[l], cc_conv_w[l],
                                seq, tl["conv_ts"])
        x2d = _merge(x2d, o_a, o_b, o_c, o_d, proj, b_gate[l], w_oa[l], w_ob[l], w_oc[l], w_od[l], w_merge[l],
                     tl["merge_tm"])
        x2d = _ffn(x2d, g_ffn[l], w_ffn_gate[l], w_ffn_up[l], ffn_conv_w[l], w_ffn_down[l], seq,
                   tl["ffn_tm"], tl["ffn_tf"])
    return x2d.reshape(bsz, seq, dm)
```

```python
import functools
import math

import jax
import jax.numpy as jnp
from jax import lax
from jax.experimental import pallas as pl
from jax.experimental.pallas import tpu as pltpu

F32 = jnp.float32
BF16 = jnp.bfloat16
I32 = jnp.int32

D_MODEL = 1024
N_BRANCH = 4
A_HEADS = 4
A_HEAD_DIM = 128
A_WIDTH = A_HEADS * A_HEAD_DIM
IDX_HEADS = 4
IDX_DIM = 64
TOPK_MAX = 256
CB_WIDTH = 512
CB_CONV = 31
CC_WIDTH = 512
CC_CONV = 3
SB_HEADS = 4
SB_HEAD_DIM = 128
SB_WIDTH = SB_HEADS * SB_HEAD_DIM
D_FF = 2816
FFN_CONV = 3
ROPE_THETA = 500000.0
ROT_FRACTION_DIV = 4
NORM_EPS = 1e-6

LANES = 128
SUBLANES = 8
VMEM_LIMIT = 56 * 1024 * 1024

LOG2E = math.log2(math.e)
NEG_BIG = -1e30
M_INIT = -1e20
F32_LOWEST = float(jnp.finfo(jnp.float32).min)
KEY_MIN_NORMAL = 0x00800000
TRI_PAD = 16
VALUE_PASSES = 12
SEARCH_PASS_CAP = VALUE_PASSES + 36

OFF_AQ = 0
OFF_AK = 512
OFF_AV = 1024
OFF_IQ = 1536
OFF_IK = 1792
OFF_IW = 1920
OFF_GLU = 2048
OFF_CC = 3072
OFF_SB = 4608
OFF_GATE = 6144
N_PACK = 10240
SPLIT_COL = 2048
CHUNK = 512


def _cparams(sem):
    return pltpu.CompilerParams(dimension_semantics=sem, vmem_limit_bytes=VMEM_LIMIT)


def _trig_kernel(pos_ref, invf_ref, cos_ref, sin_ref):
    ang = pos_ref[...] * invf_ref[...]
    cos_ref[...] = jnp.cos(ang)
    sin_ref[...] = jnp.sin(ang)


def _rope_tables(positions, head_dim):
    rot = head_dim // ROT_FRACTION_DIV
    half = rot // 2
    inv_freq = ROPE_THETA ** (-(jnp.arange(half, dtype=F32) * 2.0) / rot)
    t = positions.size
    rows = t * half // LANES
    pos_rep = jnp.broadcast_to(positions.astype(F32).reshape(t, 1), (t, half)).reshape(rows, LANES)
    invf = jnp.tile(inv_freq, LANES // half).reshape(1, LANES)
    tr = min(rows, 512)
    cos, sin = pl.pallas_call(
        _trig_kernel,
        grid=(rows // tr,),
        in_specs=[pl.BlockSpec((tr, LANES), lambda i: (i, 0)), pl.BlockSpec((1, LANES), lambda i: (0, 0))],
        out_specs=[pl.BlockSpec((tr, LANES), lambda i: (i, 0))] * 2,
        out_shape=[jax.ShapeDtypeStruct((rows, LANES), F32)] * 2,
        compiler_params=_cparams(("parallel",)),
        name="rope_trig",
    )(pos_rep, invf)
    return cos.reshape(t, half), sin.reshape(t, half)


def _rope_lane_tables(cos, sin, head_dim):
    t, half = cos.shape
    rest = head_dim - 2 * half
    c = jnp.concatenate([cos, cos, jnp.ones((t, rest), F32)], axis=1)
    s = jnp.concatenate([sin, sin, jnp.zeros((t, rest), F32)], axis=1)
    reps = LANES // head_dim
    return jnp.tile(c, (1, reps)), jnp.tile(s, (1, reps))


def _inproj_kernel(x_ref, g_ref, w_ref, o_ref, h_ref):
    @pl.when(pl.program_id(1) == 0)
    def _():
        x = x_ref[...]
        ms = jnp.mean(x * x, axis=-1, keepdims=True)
        h_ref[...] = (x * lax.rsqrt(ms + NORM_EPS) * g_ref[...]).astype(BF16)

    o_ref[...] = jnp.dot(h_ref[...], w_ref[...], preferred_element_type=F32).astype(o_ref.dtype)


def _inproj(x2d, g, w_packed, tm, tn, col0, ncols, out_dtype):
    t, d = x2d.shape
    j0 = col0 // tn
    return pl.pallas_call(
        _inproj_kernel,
        grid=(t // tm, ncols // tn),
        in_specs=[
            pl.BlockSpec((tm, d), lambda i, j: (i, 0)),
            pl.BlockSpec((1, d), lambda i, j: (0, 0)),
            pl.BlockSpec((d, tn), lambda i, j: (0, j + j0)),
        ],
        out_specs=pl.BlockSpec((tm, tn), lambda i, j: (i, j)),
        out_shape=jax.ShapeDtypeStruct((t, ncols), out_dtype),
        scratch_shapes=[pltpu.VMEM((tm, d), BF16)],
        compiler_params=_cparams(("parallel", "arbitrary")),
        name="inproj",
    )(x2d, g.reshape(1, d), w_packed)


N_IN = 10052
OFF_IW_SRC = 1856
OFF_REST_SRC = 1860
PACK_FIRST = OFF_REST_SRC // LANES
PACK_ROT = (OFF_GLU - OFF_REST_SRC) % LANES


def _pack_kernel(a_ref, b_ref, o_ref):
    n = pl.program_id(0)
    row = lax.broadcasted_iota(I32, a_ref.shape, 0)
    a = a_ref[...]

    @pl.when(n < OFF_IK // LANES)
    def _():
        o_ref[...] = a.T.astype(BF16)

    @pl.when(n == OFF_IK // LANES)
    def _():
        o_ref[...] = jnp.where(row < IDX_DIM, a, 0.0).T.astype(BF16)

    @pl.when(n == OFF_IW // LANES)
    def _():
        iw = pltpu.roll(a, LANES - (OFF_IW_SRC - OFF_IK), 0)
        o_ref[...] = jnp.where(row < IDX_HEADS, iw, 0.0).T.astype(BF16)

    @pl.when(n >= OFF_GLU // LANES)
    def _():
        out = jnp.where(row < PACK_ROT, pltpu.roll(a, PACK_ROT, 0), pltpu.roll(b_ref[...], PACK_ROT, 0))
        o_ref[...] = out.T.astype(BF16)


def _pack_w_in(w_t):
    d = w_t.shape[1]
    first_rest = OFF_GLU // LANES

    def a_idx(n):
        return (jnp.where(n < first_rest, jnp.minimum(n, PACK_FIRST), n - first_rest + PACK_FIRST), 0)

    def b_idx(n):
        return (jnp.where(n < first_rest, PACK_FIRST, n - first_rest + PACK_FIRST + 1), 0)

    return pl.pallas_call(
        _pack_kernel,
        grid=(N_PACK // LANES,),
        in_specs=[pl.BlockSpec((LANES, d), a_idx), pl.BlockSpec((LANES, d), b_idx)],
        out_specs=pl.BlockSpec((d, LANES), lambda n: (0, n)),
        out_shape=jax.ShapeDtypeStruct((d, N_PACK), BF16),
        compiler_params=_cparams(("parallel",)),
        name="pack_w_in",
    )(w_t, w_t)


def _rotate_half(x, half, period):
    lane = lax.broadcasted_iota(I32, x.shape, 1)
    up = pltpu.roll(x, LANES - half, 1)
    dn = pltpu.roll(x, half, 1)
    return jnp.where((lane % period) < half, -up, dn)


def _prep_kernel(p_ref, sqi_ref, ski_ref, svi_ref, ca_ref, sa_ref, ci_ref, si_ref, gq_ref, gk_ref, gi_ref,
                 qT_ref, qiT_ref, wT_ref, ki_ref, ka_ref, vT_ref, sq_ref, skT_ref, sv_ref):
    ca, sa, ci, si = ca_ref[...], sa_ref[...], ci_ref[...], si_ref[...]
    a_half = A_HEAD_DIM // ROT_FRACTION_DIV // 2
    i_half = IDX_DIM // ROT_FRACTION_DIV // 2

    def head_norm(x, g, n_real):
        ms = jnp.sum(x * x, axis=-1, keepdims=True) * (1.0 / n_real)
        return x * lax.rsqrt(ms + NORM_EPS) * g

    for h in range(A_HEADS):
        sl = slice(h * A_HEAD_DIM, (h + 1) * A_HEAD_DIM)
        q = head_norm(p_ref[:, OFF_AQ + h * 128:OFF_AQ + (h + 1) * 128], gq_ref[...], A_HEAD_DIM)
        q = q * ca + _rotate_half(q, a_half, A_HEAD_DIM) * sa
        qT_ref[sl, :] = (q * (A_HEAD_DIM ** -0.5 * LOG2E)).T.astype(BF16)
        k = head_norm(p_ref[:, OFF_AK + h * 128:OFF_AK + (h + 1) * 128], gk_ref[...], A_HEAD_DIM)
        ka_ref[:, sl] = (k * ca + _rotate_half(k, a_half, A_HEAD_DIM) * sa).astype(BF16)
        vT_ref[sl, :] = p_ref[:, OFF_AV + h * 128:OFF_AV + (h + 1) * 128].T.astype(BF16)

    row = lax.broadcasted_iota(I32, (LANES, p_ref.shape[0]), 0)
    for j in range(IDX_HEADS * IDX_DIM // LANES):
        qi = p_ref[:, OFF_IQ + j * LANES:OFF_IQ + (j + 1) * LANES]
        qt = (qi * ci + _rotate_half(qi, i_half, IDX_DIM) * si).T
        qiT_ref[(2 * j) * LANES:(2 * j + 1) * LANES, :] = jnp.where(row < IDX_DIM, qt, 0.0).astype(BF16)
        qiT_ref[(2 * j + 1) * LANES:(2 * j + 2) * LANES, :] = jnp.where(row >= IDX_DIM, qt, 0.0).astype(BF16)
    ki = head_norm(p_ref[:, OFF_IK:OFF_IK + LANES], gi_ref[...], IDX_DIM)
    ki = ki * ci + _rotate_half(ki, i_half, IDX_DIM) * si
    ki_ref[...] = (ki + pltpu.roll(ki, IDX_DIM, 1)).astype(BF16)
    w = p_ref[:, OFF_IW:OFF_IW + LANES] * (IDX_DIM ** -0.5 * IDX_HEADS ** -0.5)
    wT_ref[...] = w.T[0:SUBLANES, :]

    sq_ref[...] = (sqi_ref[...].astype(F32) * (SB_HEAD_DIM ** -0.5 * LOG2E)).astype(BF16)
    for h in range(SB_HEADS):
        k_t = ski_ref[:, h * 128:(h + 1) * 128].astype(F32).T.astype(BF16)
        sb_ck = skT_ref.shape[-1]
        for c in range(skT_ref.shape[1]):
            skT_ref[h, c] = k_t[:, c * sb_ck:(c + 1) * sb_ck]
    sv_ref[...] = svi_ref[...]


def _prep(proj_a, proj_b, ca, sa, ci, si, g_qa, g_ka, g_kidx, bsz, seq):
    t = proj_a.shape[0]
    ck = min(CHUNK, seq)
    sb_ck = min(SB_CK, seq)
    nck = seq // ck
    gi = jnp.concatenate([g_kidx, jnp.ones((LANES - IDX_DIM,), F32)]).reshape(1, LANES)
    tok = lambda b, i: (b * nck + i, 0)
    fixed = lambda b, i: (0, 0)
    tbl = pl.BlockSpec((ck, LANES), tok)
    gsp = pl.BlockSpec((1, LANES), fixed)
    sb_col = (OFF_SB - SPLIT_COL) // SB_WIDTH
    sb_in = lambda n: pl.BlockSpec((ck, SB_WIDTH), lambda b, i: (b * nck + i, sb_col + n))
    return pl.pallas_call(
        _prep_kernel,
        grid=(bsz, nck),
        in_specs=[pl.BlockSpec((ck, SPLIT_COL), tok), sb_in(0), sb_in(1), sb_in(2),
                  tbl, tbl, tbl, tbl, gsp, gsp, gsp],
        out_specs=[
            pl.BlockSpec((None, A_WIDTH, ck), lambda b, i: (b, 0, i)),
            pl.BlockSpec((None, IDX_HEADS * LANES, ck), lambda b, i: (b, 0, i)),
            pl.BlockSpec((None, SUBLANES, ck), lambda b, i: (b, 0, i)),
            pl.BlockSpec((None, ck, LANES), lambda b, i: (b, i, 0)),
            pl.BlockSpec((None, ck, A_WIDTH), lambda b, i: (b, i, 0)),
            pl.BlockSpec((None, None, A_WIDTH, ck), lambda b, i: (b, i, 0, 0)),
            pl.BlockSpec((ck, SB_WIDTH), tok),
            pl.BlockSpec((None, SB_HEADS, ck // sb_ck, SB_HEAD_DIM, sb_ck), lambda b, i: (b, 0, i, 0, 0)),
            pl.BlockSpec((None, ck, SB_WIDTH), lambda b, i: (b, i, 0)),
        ],
        out_shape=[
            jax.ShapeDtypeStruct((bsz, A_WIDTH, seq), BF16),
            jax.ShapeDtypeStruct((bsz, IDX_HEADS * LANES, seq), BF16),
            jax.ShapeDtypeStruct((bsz, SUBLANES, seq), F32),
            jax.ShapeDtypeStruct((bsz, seq, LANES), BF16),
            jax.ShapeDtypeStruct((bsz, seq, A_WIDTH), BF16),
            jax.ShapeDtypeStruct((bsz, nck, A_WIDTH, ck), BF16),
            jax.ShapeDtypeStruct((t, SB_WIDTH), BF16),
            jax.ShapeDtypeStruct((bsz, SB_HEADS, seq // sb_ck, SB_HEAD_DIM, sb_ck), BF16),
            jax.ShapeDtypeStruct((bsz, seq, SB_WIDTH), BF16),
        ],
        compiler_params=_cparams(("parallel", "parallel")),
        name="mixer_prep",
    )(proj_a, proj_b, proj_b, proj_b, ca, sa, ci, si, g_qa.reshape(1, LANES), g_ka.reshape(1, LANES), gi)


def _order_key(f):
    b = pltpu.bitcast(f, I32)
    return b ^ ((b >> 31) & 0x7FFFFFFF)


def _key_float(k):
    return pltpu.bitcast(k ^ ((k >> 31) & 0x7FFFFFFF), F32)


def _normal_key(k):
    k = jnp.where(jnp.logical_and(k > 0, k < KEY_MIN_NORMAL), KEY_MIN_NORMAL, k)
    return jnp.where(jnp.logical_and(k < 0, k >= -KEY_MIN_NORMAL), 0, k)


def _dsa_kernel(qT_ref, qiT_ref, wT_ref, ki_ref, ka_ref, vT_ref, tri_ref, o_ref, sc_ref, red_ref, m_ref, l_ref,
                acc_ref, *, tq, ck, grp, k_sel):
    i = pl.program_id(1)
    n_keys = (i + 1) * tq
    n_chunks = (n_keys + ck - 1) // ck
    n_groups = (n_keys + grp - 1) // grp
    q_pos = i * tq + lax.broadcasted_iota(I32, (1, tq), 1)
    neg_inf = jnp.full((ck, tq), -jnp.inf, F32)

    def chunk_start(c):
        return pl.multiple_of(c * ck, ck)

    def key_pos(c):
        return c * ck + lax.broadcasted_iota(I32, (ck, tq), 0)

    def score_chunk(c, diag):
        ks = chunk_start(c)
        kic = ki_ref[pl.ds(ks, ck), :]
        sc = None
        for h in range(IDX_HEADS):
            d = jnp.dot(kic, qiT_ref[h * LANES:(h + 1) * LANES, :], preferred_element_type=F32)
            term = jnp.maximum(d, 0.0) * wT_ref[h:h + 1, :]
            sc = term if sc is None else sc + term
        if diag:
            causal = key_pos(c) <= q_pos
            sc_hi = jnp.where(causal, sc, -jnp.inf)
            sc_lo = jnp.where(causal, sc, jnp.inf)
        else:
            sc_hi = sc_lo = sc
        sc_ref[pl.ds(ks, ck), :] = sc_hi
        fold = lambda a, op: op(a.reshape(ck // SUBLANES, SUBLANES, tq), axis=0)
        return fold(sc_hi, jnp.max), fold(sc_lo, jnp.min)

    def score_body(c, carry):
        mx, mn = score_chunk(c, False)
        return jnp.maximum(carry[0], mx), jnp.minimum(carry[1], mn)

    init = (jnp.full((SUBLANES, tq), -jnp.inf, F32), jnp.full((SUBLANES, tq), jnp.inf, F32))
    mx, mn = lax.fori_loop(0, n_chunks - 1, score_body, init)
    mx_d, mn_d = score_chunk(n_chunks - 1, True)
    def fold_rows(x, op):
        red_ref[...] = x
        r = red_ref[0:1, :]
        for j in range(1, SUBLANES):
            r = op(r, red_ref[j:j + 1, :])
        return r

    row_max = fold_rows(jnp.maximum(mx, mx_d), jnp.maximum)
    row_min = fold_rows(jnp.minimum(mn, mn_d), jnp.minimum)

    def pad_body(c, carry):
        sc_ref[pl.ds(chunk_start(c), ck), :] = neg_inf
        return carry
    lax.fori_loop(n_chunks, n_groups * (grp // ck), pad_body, 0)

    ones = jnp.ones((SUBLANES, grp), BF16)
    max_groups = sc_ref.shape[0] // grp

    def count_ge(t):
        for ng in range(1, max_groups + 1):
            @pl.when(n_groups == ng)
            def _(ng=ng):
                acc = None
                for g in range(ng):
                    ind = jnp.where(sc_ref[g * grp:(g + 1) * grp, :] >= t, 1.0, 0.0).astype(BF16)
                    part = jnp.dot(ones, ind, preferred_element_type=F32)
                    acc = part if acc is None else acc + part
                red_ref[...] = acc
        return red_ref[0:1, :].astype(I32)

    n_causal = q_pos + 1
    take_all = n_causal <= k_sel
    lo0 = _normal_key(_order_key(row_min))
    hi0 = _normal_key(_order_key(row_max)) + 1
    thr0 = jnp.full((1, tq), F32_LOWEST, F32)
    done0 = jnp.where(take_all, 1, 0)
    zero = jnp.zeros((1, tq), I32)

    def search_cond(st):
        it, _, _, _, _, done, _ = st
        return jnp.logical_and(it < SEARCH_PASS_CAP, jnp.min(done) == 0)

    def search_body(st):
        it, lo, hi, c_hi, thr, done, tied = st
        cross = jnp.logical_and(lo < 0, hi > 0)
        gap = hi - lo
        zero_probe = jnp.logical_and(it == 1, lo == 0)
        mid = _normal_key(jnp.where(cross, 0, jnp.where(zero_probe, KEY_MIN_NORMAL, lo + (gap >> 1))))
        v_mid = _normal_key(_order_key(0.5 * _key_float(lo) + 0.5 * _key_float(hi)))
        by_value = jnp.logical_and(jnp.logical_and(it < VALUE_PASSES, jnp.logical_not(jnp.logical_or(cross, zero_probe))),
                                   jnp.logical_and(v_mid > lo, v_mid < hi))
        mid = jnp.where(by_value, v_mid, mid)
        no_pivot = jnp.logical_or(gap == 1, jnp.logical_or(mid >= hi, mid <= lo))
        adjacent = jnp.logical_and(no_pivot, jnp.logical_not(cross))
        pivot = _key_float(mid)
        cnt = count_ge(pivot)
        live = done == 0
        hit = jnp.logical_and(live, jnp.logical_and(cnt == k_sel, jnp.logical_not(adjacent)))
        end = jnp.logical_and(live, adjacent)
        move = jnp.logical_and(live, jnp.logical_not(jnp.logical_or(hit, end)))
        up = jnp.logical_and(move, cnt > k_sel)
        dn = jnp.logical_and(move, cnt < k_sel)
        thr = jnp.where(hit, pivot, jnp.where(end, _key_float(lo), thr))
        tied = jnp.where(end, 1, tied)
        done = jnp.where(jnp.logical_or(hit, end), 1, done)
        lo = jnp.where(up, mid, lo)
        hi = jnp.where(dn, mid, hi)
        c_hi = jnp.where(dn, cnt, c_hi)
        return it + 1, lo, hi, c_hi, thr, done, tied

    st = lax.while_loop(search_cond, search_body, (jnp.int32(0), lo0, hi0, zero, thr0, done0, zero))
    _, _, _, c_hi, thr, _, tied = st

    room = jnp.where(tied > 0, k_sel - c_hi, jnp.int32(2 ** 30)).astype(F32)

    @pl.when(jnp.max(tied) > 0)
    def _():
        red_ref[0:1, :] = jnp.zeros((1, tq), F32)

        def drop_excess(c, carry):
            ks = chunk_start(c)
            s = sc_ref[pl.ds(ks, ck), :]
            is_tied = s == thr
            ind = jnp.where(is_tied, 1.0, 0.0).astype(BF16)
            pre = jnp.dot(tri_ref[...], ind, preferred_element_type=F32)
            before = red_ref[0:1, :]
            rank = pre[0:ck, :] + before
            sc_ref[pl.ds(ks, ck), :] = jnp.where(is_tied, jnp.where(rank >= room, -jnp.inf, s), s)
            red_ref[0:1, :] = before + pre[ck:ck + 1, :]
            return carry
        lax.fori_loop(0, n_chunks, drop_excess, 0)

    m_ref[...] = jnp.full(m_ref.shape, M_INIT, F32)
    l_ref[...] = jnp.zeros(l_ref.shape, F32)
    acc_ref[...] = jnp.zeros(acc_ref.shape, F32)
    heads = [slice(h * A_HEAD_DIM, (h + 1) * A_HEAD_DIM) for h in range(A_HEADS)]

    def att_body(c, carry):
        ks = chunk_start(c)
        sel = sc_ref[pl.ds(ks, ck), :] >= thr
        logits = [jnp.dot(ka_ref[pl.ds(ks, ck), hs], qT_ref[hs, :], preferred_element_type=F32) for hs in heads]
        for h, hs in enumerate(heads):
            lg = jnp.where(sel, logits[h], NEG_BIG)
            m_old = m_ref[h, 0:1, :]
            m_new = jnp.maximum(m_old, jnp.max(lg, axis=0, keepdims=True))
            alpha = jnp.exp2(m_old - m_new)
            p = jnp.exp2(lg - m_new)
            l_ref[h, 0:1, :] = alpha * l_ref[h, 0:1, :] + jnp.sum(p, axis=0, keepdims=True)
            m_ref[h, 0:1, :] = m_new
            pv = jnp.dot(vT_ref[c, hs, :], p.astype(BF16), preferred_element_type=F32)
            acc_ref[h] = acc_ref[h] * alpha + pv
        return carry

    lax.fori_loop(0, n_chunks, att_body, 0)
    for h, hs in enumerate(heads):
        o_ref[:, hs] = (acc_ref[h] / l_ref[h, 0:1, :]).T.astype(o_ref.dtype)


def _dsa_attention(qT, qiT, wT, ki, ka, vT, bsz, seq, tq):
    ck = min(CHUNK, seq)
    nck = seq // ck
    grp = min(2 * ck, seq)
    k_sel = min(TOPK_MAX, seq // 4)
    kern = functools.partial(_dsa_kernel, tq=tq, ck=ck, grp=grp, k_sel=k_sel)
    once = pl.Buffered(1)
    out = pl.pallas_call(
        kern,
        grid=(bsz, seq // tq),
        in_specs=[
            pl.BlockSpec((None, A_WIDTH, tq), lambda b, i: (b, 0, i)),
            pl.BlockSpec((None, IDX_HEADS * LANES, tq), lambda b, i: (b, 0, i)),
            pl.BlockSpec((None, SUBLANES, tq), lambda b, i: (b, 0, i)),
            pl.BlockSpec((None, seq, LANES), lambda b, i: (b, 0, 0), pipeline_mode=once),
            pl.BlockSpec((None, seq, A_WIDTH), lambda b, i: (b, 0, 0), pipeline_mode=once),
            pl.BlockSpec((None, nck, A_WIDTH, ck), lambda b, i: (b, 0, 0, 0), pipeline_mode=once),
            pl.BlockSpec((ck + TRI_PAD, ck), lambda b, i: (0, 0), pipeline_mode=once),
        ],
        out_specs=pl.BlockSpec((None, tq, A_WIDTH), lambda b, i: (b, i, 0)),
        out_shape=jax.ShapeDtypeStruct((bsz, seq, A_WIDTH), BF16),
        scratch_shapes=[pltpu.VMEM((seq, tq), F32), pltpu.VMEM((SUBLANES, tq), F32),
                        pltpu.VMEM((A_HEADS, SUBLANES, tq), F32), pltpu.VMEM((A_HEADS, SUBLANES, tq), F32),
                        pltpu.VMEM((A_HEADS, A_HEAD_DIM, tq), F32)],
        compiler_params=_cparams(("parallel", "arbitrary")),
        name="dsa_attention",
    )(qT, qiT, wT, ki, ka, vT, _prefix_matrix(ck))
    return out.reshape(bsz * seq, A_WIDTH)


def _prefix_matrix(n):
    r = jnp.arange(n + TRI_PAD)[:, None]
    c = jnp.arange(n)[None, :]
    return jnp.logical_or(c < r, r >= n).astype(BF16)


SB_CK = 256
SB_TQ = 1024
SB_STEP = 2


def _sb_kernel(q_ref, kT_ref, v_ref, u_ref, o_ref, acc_ref, carry_ref, *, tq, ck):
    i = pl.program_id(2)
    q = q_ref[...]
    acc_ref[...] = jnp.zeros(acc_ref.shape, F32)
    carry_ref[...] = jnp.zeros(carry_ref.shape, F32)
    per = tq // ck
    row = i * tq + lax.broadcasted_iota(I32, (tq, ck), 0)
    col = lax.broadcasted_iota(I32, (tq, ck), 1)

    def chunk(c, diag):
        z = jnp.dot(q, kT_ref[c], preferred_element_type=F32)
        neg_abs = pltpu.bitcast(pltpu.bitcast(z, I32) | jnp.int32(-2 ** 31), F32)
        neg_soft = jnp.log(1.0 + jnp.exp2(neg_abs)) * (-LOG2E)
        lk = neg_soft - jnp.maximum(z, 0.0)
        if diag:
            mask = (c * ck + col) < row
            lk = jnp.where(mask, lk, 0.0)
        lk_b = lk.astype(BF16)
        cum = jnp.dot(lk_b, u_ref[...], preferred_element_type=F32)
        a = jnp.exp2((z + lk) + cum)
        if diag:
            a = jnp.where(mask, a, 0.0)
        vs = pl.multiple_of(c * ck, ck)
        pv = jnp.dot(a.astype(BF16), v_ref[pl.ds(vs, ck), :], preferred_element_type=F32)
        total = cum[:, 0:1] + lk_b[:, 0:1].astype(F32)
        return pv, total

    def fold(parts):
        for pv, total in parts:
            carry = carry_ref[...]
            acc_ref[...] += jnp.exp2(carry) * pv
            carry_ref[...] = carry + jnp.broadcast_to(total, carry.shape)

    step = min(SB_STEP, per)
    for d0 in reversed(range(0, per, step)):
        fold([chunk(i * per + d0 + d, True) for d in reversed(range(step))])

    def body(s, carry):
        first = i * per - 1 - s * step
        fold([chunk(first - d, False) for d in range(step)])
        return carry

    lax.fori_loop(0, i * (per // step), body, 0)
    o_ref[...] = acc_ref[...].astype(o_ref.dtype)


def _cumsum_matrix(n):
    j = jnp.arange(n)[:, None]
    s = jnp.arange(n)[None, :]
    return (j > s).astype(BF16)


def _sb_attention(sq, skT, sv, bsz, seq):
    t = bsz * seq
    tq = min(SB_TQ, seq)
    ck = min(SB_CK, seq)
    nq = seq // tq
    return pl.pallas_call(
        functools.partial(_sb_kernel, tq=tq, ck=ck),
        grid=(bsz, SB_HEADS, nq),
        in_specs=[
            pl.BlockSpec((tq, SB_HEAD_DIM), lambda b, h, i: (b * nq + i, h)),
            pl.BlockSpec((None, None, seq // ck, SB_HEAD_DIM, ck), lambda b, h, i: (b, h, 0, 0, 0)),
            pl.BlockSpec((None, seq, SB_HEAD_DIM), lambda b, h, i: (b, 0, h)),
            pl.BlockSpec((ck, ck), lambda b, h, i: (0, 0)),
        ],
        out_specs=pl.BlockSpec((tq, SB_HEAD_DIM), lambda b, h, i: (b * nq + i, h)),
        out_shape=jax.ShapeDtypeStruct((t, SB_WIDTH), BF16),
        scratch_shapes=[pltpu.VMEM((tq, SB_HEAD_DIM), F32), pltpu.VMEM((tq, LANES), F32)],
        compiler_params=_cparams(("parallel", "parallel", "arbitrary")),
        name="stickbreak",
    )(sq, skT, sv, _cumsum_matrix(ck))


CB_HALO = 32
CC_HALO = 16
CONV_ROWS = 64


def _conv_kernel(ga_ref, gb_ref, hga_ref, hgb_ref, cb_ref, cc_ref, cx_ref, hcc_ref, hcx_ref,
                 wb_ref, bb_ref, lng_ref, lnb_ref, wc_ref, ob_ref, oc_ref, buf_ref, u_ref, buf2_ref, *, ts, seq):
    i = pl.program_id(0)
    keep = jnp.where((i * ts) % seq == 0, 0.0, 1.0)

    f32 = lambda ref: ref[...].astype(F32)
    buf_ref[0:CB_HALO, :] = f32(hga_ref) * jax.nn.sigmoid(f32(hgb_ref)) * keep
    buf_ref[CB_HALO:, :] = f32(ga_ref) * jax.nn.sigmoid(f32(gb_ref))
    off0 = CB_HALO - (CB_CONV - 1)
    for r in range(ts // CONV_ROWS):
        t0 = r * CONV_ROWS
        for g in range(CB_WIDTH // LANES):
            ls = slice(g * LANES, (g + 1) * LANES)
            acc = jnp.broadcast_to(bb_ref[:, ls], (CONV_ROWS, LANES))
            for s in range(SUBLANES):
                rows = CONV_ROWS + (SUBLANES if s else 0)
                part = None
                for k in range(CB_CONV):
                    if (off0 + k) % SUBLANES != s:
                        continue
                    base = t0 + (off0 + k) - s
                    term = wb_ref[k:k + 1, ls] * buf_ref[base:base + rows, ls]
                    part = term if part is None else part + term
                acc = acc + part[s:s + CONV_ROWS, :]
            u_ref[t0:t0 + CONV_ROWS, ls] = acc
    u = u_ref[...]
    mu = jnp.mean(u, axis=-1, keepdims=True)
    uc = u - mu
    y = uc * lax.rsqrt(jnp.mean(uc * uc, axis=-1, keepdims=True) + NORM_EPS) * lng_ref[...] + lnb_ref[...]
    ob_ref[...] = (y * jax.nn.sigmoid(y)).astype(ob_ref.dtype)

    buf2_ref[0:CC_HALO, :] = f32(hcc_ref) * f32(hcx_ref) * keep
    buf2_ref[CC_HALO:, :] = f32(cc_ref) * f32(cx_ref)
    conv = jnp.zeros((ts, CC_WIDTH), F32)
    for k in range(CC_CONV):
        base = CC_HALO - (CC_CONV - 1) + k
        conv = conv + wc_ref[k:k + 1, :] * buf2_ref[base:base + ts, :]
    oc_ref[...] = (f32(cb_ref) * conv).astype(oc_ref.dtype)


def _conv_mixers(proj, wb, bb, lng, lnb, wc, seq, ts):
    t = proj.shape[0]
    w = CB_WIDTH
    c0 = (OFF_GLU - SPLIT_COL) // w
    c1 = (OFF_CC - SPLIT_COL) // w
    cur = lambda col: pl.BlockSpec((ts, w), lambda i: (i, col))
    halo = lambda rows, col: pl.BlockSpec((rows, w), lambda i: (jnp.maximum(i * (ts // rows) - 1, 0), col))
    full = lambda r: pl.BlockSpec((r, w), lambda i: (0, 0))
    wb_p = jnp.pad(wb, ((0, CB_HALO - CB_CONV), (0, 0)))
    wc_p = jnp.pad(wc, ((0, SUBLANES - CC_CONV), (0, 0)))
    return pl.pallas_call(
        functools.partial(_conv_kernel, ts=ts, seq=seq),
        grid=(t // ts,),
        in_specs=[cur(c0), cur(c0 + 1), halo(CB_HALO, c0), halo(CB_HALO, c0 + 1),
                  cur(c1), cur(c1 + 1), cur(c1 + 2), halo(CC_HALO, c1 + 1), halo(CC_HALO, c1 + 2),
                  full(CB_HALO), full(1), full(1), full(1), full(SUBLANES)],
        out_specs=[pl.BlockSpec((ts, w), lambda i: (i, 0))] * 2,
        out_shape=[jax.ShapeDtypeStruct((t, w), BF16)] * 2,
        scratch_shapes=[pltpu.VMEM((CB_HALO + ts, w), F32), pltpu.VMEM((ts, w), F32),
                        pltpu.VMEM((CC_HALO + ts, w), F32)],
        compiler_params=_cparams(("parallel",)),
        name="conv_mixers",
    )(proj, proj, proj, proj, proj, proj, proj, proj, proj,
      wb_p, bb.reshape(1, w), lng.reshape(1, w), lnb.reshape(1, w), wc_p)


def _merge_kernel(x_ref, oa_ref, ob_ref, oc_ref, od_ref, g0_ref, g1_ref, g2_ref, g3_ref, bg_ref,
                  wa_ref, wb_ref, wc_ref, wd_ref, wm_ref, o_ref):
    merged = None
    branches = ((oa_ref, wa_ref, g0_ref), (ob_ref, wb_ref, g1_ref), (oc_ref, wc_ref, g2_ref), (od_ref, wd_ref, g3_ref))
    for n, (b_ref, w_ref, g_ref) in enumerate(branches):
        y = jnp.dot(b_ref[...], w_ref[...], preferred_element_type=F32)
        term = jax.nn.sigmoid(g_ref[...].astype(F32) + bg_ref[n:n + 1, :]) * y
        merged = term if merged is None else merged + term
    o_ref[...] = x_ref[...] + jnp.dot(merged.astype(BF16), wm_ref[...], preferred_element_type=F32)


def _merge(x2d, oa, ob, oc, od, proj, b_gate, w_oa, w_ob, w_oc, w_od, w_merge, tm):
    t, d = x2d.shape
    g0 = (OFF_GATE - SPLIT_COL) // d
    row = lambda i: (i, 0)
    fixed = lambda i: (0, 0)
    br = pl.BlockSpec((tm, A_WIDTH), row)
    gate = lambda n: pl.BlockSpec((tm, d), lambda i: (i, g0 + n))
    wsp = pl.BlockSpec((A_WIDTH, d), fixed)
    bf = lambda w: w.astype(BF16)
    return pl.pallas_call(
        _merge_kernel,
        grid=(t // tm,),
        in_specs=[pl.BlockSpec((tm, d), row), br, br, br, br, gate(0), gate(1), gate(2), gate(3),
                  pl.BlockSpec((N_BRANCH, d), fixed), wsp, wsp, wsp, wsp, pl.BlockSpec((d, d), fixed)],
        out_specs=pl.BlockSpec((tm, d), row),
        out_shape=jax.ShapeDtypeStruct((t, d), F32),
        compiler_params=_cparams(("parallel",)),
        name="merge",
    )(x2d, oa, ob, oc, od, proj, proj, proj, proj, b_gate.reshape(N_BRANCH, d),
      bf(w_oa), bf(w_ob), bf(w_oc), bf(w_od), bf(w_merge))


FFN_HALO = 8


def _ffn_kernel(x_ref, hx_ref, g_ref, wg_ref, wu_ref, cw_ref, wd_ref, o_ref, h_ref, gp_ref, acc_ref, *, tm, seq):
    i = pl.program_id(0)
    f = pl.program_id(1)

    def norm(x):
        ms = jnp.mean(x * x, axis=-1, keepdims=True)
        return (x * lax.rsqrt(ms + NORM_EPS) * g_ref[...]).astype(BF16)

    @pl.when(f == 0)
    def _():
        h_ref[0:FFN_HALO, :] = norm(hx_ref[...])
        h_ref[FFN_HALO:, :] = norm(x_ref[...])
        acc_ref[...] = x_ref[...]

    keep = jnp.where((i * tm) % seq == 0, 0.0, 1.0)
    gp_ref[0:FFN_HALO, :] = jnp.dot(h_ref[0:FFN_HALO, :], wg_ref[...], preferred_element_type=F32) * keep
    gp_ref[FFN_HALO:, :] = jnp.dot(h_ref[FFN_HALO:, :], wg_ref[...], preferred_element_type=F32)
    gt = jnp.zeros((tm, gp_ref.shape[1]), F32)
    for k in range(FFN_CONV):
        base = FFN_HALO - (FFN_CONV - 1) + k
        gt = gt + cw_ref[k:k + 1, :] * gp_ref[base:base + tm, :]
    up = jnp.dot(h_ref[FFN_HALO:, :], wu_ref[...], preferred_element_type=F32)
    act = (gt * jax.nn.sigmoid(gt) * up).astype(BF16)
    acc_ref[...] += jnp.dot(act, wd_ref[...], preferred_element_type=F32)

    @pl.when(f == pl.num_programs(1) - 1)
    def _():
        o_ref[...] = acc_ref[...]


def _ffn(x2d, g, wg, wu, cw, wd, seq, tm, tf):
    t, d = x2d.shape
    ff = wg.shape[1]
    cw_p = jnp.pad(cw, ((0, SUBLANES - FFN_CONV), (0, 0)))
    return pl.pallas_call(
        functools.partial(_ffn_kernel, tm=tm, seq=seq),
        grid=(t // tm, ff // tf),
        in_specs=[
            pl.BlockSpec((tm, d), lambda i, f: (i, 0)),
            pl.BlockSpec((FFN_HALO, d), lambda i, f: (jnp.maximum(i * (tm // FFN_HALO) - 1, 0), 0)),
            pl.BlockSpec((1, d), lambda i, f: (0, 0)),
            pl.BlockSpec((d, tf), lambda i, f: (0, f)),
            pl.BlockSpec((d, tf), lambda i, f: (0, f)),
            pl.BlockSpec((SUBLANES, tf), lambda i, f: (0, f)),
            pl.BlockSpec((tf, d), lambda i, f: (f, 0)),
        ],
        out_specs=pl.BlockSpec((tm, d), lambda i, f: (i, 0)),
        out_shape=jax.ShapeDtypeStruct((t, d), F32),
        scratch_shapes=[pltpu.VMEM((FFN_HALO + tm, d), BF16), pltpu.VMEM((FFN_HALO + tm, tf), F32),
                        pltpu.VMEM((tm, d), F32)],
        compiler_params=_cparams(("parallel", "arbitrary")),
        name="ffn",
    )(x2d, x2d, g.reshape(1, d), wg.astype(BF16), wu.astype(BF16), cw_p, wd.astype(BF16))


def _tiles(seq):
    big = min(seq, 512)
    return dict(
        inproj_tm=min(seq, 1024), inproj_tn=1024,
        conv_ts=big, merge_tm=big, ffn_tm=big, ffn_tf=D_FF // 2,
        dsa_tq=min(seq, 512),
    )


def kernel(x, positions, g_mix, w_in, b_gate, g_qa, g_ka, g_kidx, w_oa, cb_conv_w, cb_conv_b, cb_ln_g, cb_ln_b,
           w_ob, cc_conv_w, w_oc, w_od, w_merge, g_ffn, w_ffn_gate, w_ffn_up, ffn_conv_w, w_ffn_down):
    bsz, seq, dm = x.shape
    depth = w_in.shape[0]
    assert dm == D_MODEL and seq % min(CHUNK, seq) == 0 and seq % LANES == 0
    tl = _tiles(seq)
    t = bsz * seq

    cos_a, sin_a = _rope_tables(positions, A_HEAD_DIM)
    cos_i, sin_i = _rope_tables(positions, IDX_DIM)
    ca, sa = _rope_lane_tables(cos_a, sin_a, A_HEAD_DIM)
    ci, si = _rope_lane_tables(cos_i, sin_i, IDX_DIM)

    x2d = x.reshape(t, dm)
    for l in range(depth):
        w_packed = _pack_w_in(jnp.swapaxes(w_in[l], 0, 1))
        tm, tn = tl["inproj_tm"], tl["inproj_tn"]
        proj_a = _inproj(x2d, g_mix[l], w_packed, tm, tn, 0, SPLIT_COL, F32)
        proj = _inproj(x2d, g_mix[l], w_packed, tm, tn, SPLIT_COL, N_PACK - SPLIT_COL, BF16)
        qT, qiT, wT, ki, ka, vT, sq, skT, sv = _prep(proj_a, proj, ca, sa, ci, si, g_qa[l], g_ka[l], g_kidx[l],
                                                     bsz, seq)
        o_a = _dsa_attention(qT, qiT, wT, ki, ka, vT, bsz, seq, tl["dsa_tq"])
        o_d = _sb_attention(sq, skT, sv, bsz, seq)
        o_b, o_c = _conv_mixers(proj, cb_conv_w[l], cb_conv_b[l], cb_ln_g[l], cb_ln_b[l], cc_conv_w[l],
                                seq, tl["conv_ts"])
        x2d = _merge(x2d, o_a, o_b, o_c, o_d, proj, b_gate[l], w_oa[l], w_ob[l], w_oc[l], w_od[l], w_merge[l],
                     tl["merge_tm"])
        x2d = _ffn(x2d, g_ffn[l], w_ffn_gate[l], w_ffn_up[l], ffn_conv_w[l], w_ffn_down[l], seq,
                   tl["ffn_tm"], tl["ffn_tf"])
    return x2d.reshape(bsz, seq, dm)
```

```python
import functools
import math

import jax
import jax.numpy as jnp
from jax import lax
from jax.experimental import pallas as pl
from jax.experimental.pallas import tpu as pltpu

F32 = jnp.float32
BF16 = jnp.bfloat16
I32 = jnp.int32

D_MODEL = 1024
N_BRANCH = 4
A_HEADS = 4
A_HEAD_DIM = 128
A_WIDTH = A_HEADS * A_HEAD_DIM
IDX_HEADS = 4
IDX_DIM = 64
TOPK_MAX = 256
CB_WIDTH = 512
CB_CONV = 31
CC_WIDTH = 512
CC_CONV = 3
SB_HEADS = 4
SB_HEAD_DIM = 128
SB_WIDTH = SB_HEADS * SB_HEAD_DIM
D_FF = 2816
FFN_CONV = 3
ROPE_THETA = 500000.0
ROT_FRACTION_DIV = 4
NORM_EPS = 1e-6

LANES = 128
SUBLANES = 8
VMEM_LIMIT = 56 * 1024 * 1024

LOG2E = math.log2(math.e)
NEG_BIG = -1e30
M_INIT = -1e20
F32_LOWEST = float(jnp.finfo(jnp.float32).min)
KEY_MIN_NORMAL = 0x00800000
TRI_PAD = 16
VALUE_PASSES = 12
SEARCH_PASS_CAP = VALUE_PASSES + 36

OFF_AQ = 0
OFF_AK = 512
OFF_AV = 1024
OFF_IQ = 1536
OFF_IK = 1792
OFF_IW = 1920
OFF_GLU = 2048
OFF_CC = 3072
OFF_SB = 4608
OFF_GATE = 6144
N_PACK = 10240
SPLIT_COL = 2048
CHUNK = 512


def _cparams(sem):
    return pltpu.CompilerParams(dimension_semantics=sem, vmem_limit_bytes=VMEM_LIMIT)


def _trig_kernel(pos_ref, invf_ref, cos_ref, sin_ref):
    ang = pos_ref[...] * invf_ref[...]
    cos_ref[...] = jnp.cos(ang)
    sin_ref[...] = jnp.sin(ang)


def _rope_tables(positions, head_dim):
    rot = head_dim // ROT_FRACTION_DIV
    half = rot // 2
    inv_freq = ROPE_THETA ** (-(jnp.arange(half, dtype=F32) * 2.0) / rot)
    t = positions.size
    rows = t * half // LANES
    pos_rep = jnp.broadcast_to(positions.astype(F32).reshape(t, 1), (t, half)).reshape(rows, LANES)
    invf = jnp.tile(inv_freq, LANES // half).reshape(1, LANES)
    tr = min(rows, 512)
    cos, sin = pl.pallas_call(
        _trig_kernel,
        grid=(rows // tr,),
        in_specs=[pl.BlockSpec((tr, LANES), lambda i: (i, 0)), pl.BlockSpec((1, LANES), lambda i: (0, 0))],
        out_specs=[pl.BlockSpec((tr, LANES), lambda i: (i, 0))] * 2,
        out_shape=[jax.ShapeDtypeStruct((rows, LANES), F32)] * 2,
        compiler_params=_cparams(("parallel",)),
        name="rope_trig",
    )(pos_rep, invf)
    return cos.reshape(t, half), sin.reshape(t, half)


def _rope_lane_tables(cos, sin, head_dim):
    t, half = cos.shape
    rest = head_dim - 2 * half
    c = jnp.concatenate([cos, cos, jnp.ones((t, rest), F32)], axis=1)
    s = jnp.concatenate([sin, sin, jnp.zeros((t, rest), F32)], axis=1)
    reps = LANES // head_dim
    return jnp.tile(c, (1, reps)), jnp.tile(s, (1, reps))


def _inproj_kernel(x_ref, g_ref, w_ref, o_ref, h_ref):
    @pl.when(pl.program_id(1) == 0)
    def _():
        x = x_ref[...]
        ms = jnp.mean(x * x, axis=-1, keepdims=True)
        h_ref[...] = (x * lax.rsqrt(ms + NORM_EPS) * g_ref[...]).astype(BF16)

    o_ref[...] = jnp.dot(h_ref[...], w_ref[...], preferred_element_type=F32).astype(o_ref.dtype)


def _inproj(x2d, g, w_packed, tm, tn, col0, ncols, out_dtype):
    t, d = x2d.shape
    j0 = col0 // tn
    return pl.pallas_call(
        _inproj_kernel,
        grid=(t // tm, ncols // tn),
        in_specs=[
            pl.BlockSpec((tm, d), lambda i, j: (i, 0)),
            pl.BlockSpec((1, d), lambda i, j: (0, 0)),
            pl.BlockSpec((d, tn), lambda i, j: (0, j + j0)),
        ],
        out_specs=pl.BlockSpec((tm, tn), lambda i, j: (i, j)),
        out_shape=jax.ShapeDtypeStruct((t, ncols), out_dtype),
        scratch_shapes=[pltpu.VMEM((tm, d), BF16)],
        compiler_params=_cparams(("parallel", "arbitrary")),
        name="inproj",
    )(x2d, g.reshape(1, d), w_packed)


N_IN = 10052
OFF_IW_SRC = 1856
OFF_REST_SRC = 1860
PACK_FIRST = OFF_REST_SRC // LANES
PACK_ROT = (OFF_GLU - OFF_REST_SRC) % LANES


def _pack_kernel(a_ref, b_ref, o_ref):
    n = pl.program_id(0)
    row = lax.broadcasted_iota(I32, a_ref.shape, 0)
    a = a_ref[...]

    @pl.when(n < OFF_IK // LANES)
    def _():
        o_ref[...] = a.T.astype(BF16)

    @pl.when(n == OFF_IK // LANES)
    def _():
        o_ref[...] = jnp.where(row < IDX_DIM, a, 0.0).T.astype(BF16)

    @pl.when(n == OFF_IW // LANES)
    def _():
        iw = pltpu.roll(a, LANES - (OFF_IW_SRC - OFF_IK), 0)
        o_ref[...] = jnp.where(row < IDX_HEADS, iw, 0.0).T.astype(BF16)

    @pl.when(n >= OFF_GLU // LANES)
    def _():
        out = jnp.where(row < PACK_ROT, pltpu.roll(a, PACK_ROT, 0), pltpu.roll(b_ref[...], PACK_ROT, 0))
        o_ref[...] = out.T.astype(BF16)


def _pack_w_in(w_t):
    d = w_t.shape[1]
    first_rest = OFF_GLU // LANES

    def a_idx(n):
        return (jnp.where(n < first_rest, jnp.minimum(n, PACK_FIRST), n - first_rest + PACK_FIRST), 0)

    def b_idx(n):
        return (jnp.where(n < first_rest, PACK_FIRST, n - first_rest + PACK_FIRST + 1), 0)

    return pl.pallas_call(
        _pack_kernel,
        grid=(N_PACK // LANES,),
        in_specs=[pl.BlockSpec((LANES, d), a_idx), pl.BlockSpec((LANES, d), b_idx)],
        out_specs=pl.BlockSpec((d, LANES), lambda n: (0, n)),
        out_shape=jax.ShapeDtypeStruct((d, N_PACK), BF16),
        compiler_params=_cparams(("parallel",)),
        name="pack_w_in",
    )(w_t, w_t)


def _rotate_half(x, half, period):
    lane = lax.broadcasted_iota(I32, x.shape, 1)
    up = pltpu.roll(x, LANES - half, 1)
    dn = pltpu.roll(x, half, 1)
    return jnp.where((lane % period) < half, -up, dn)


def _prep_kernel(p_ref, sqi_ref, ski_ref, svi_ref, ca_ref, sa_ref, ci_ref, si_ref, gq_ref, gk_ref, gi_ref,
                 qT_ref, qiT_ref, wT_ref, ki_ref, ka_ref, vT_ref, sq_ref, skT_ref, sv_ref):
    ca, sa, ci, si = ca_ref[...], sa_ref[...], ci_ref[...], si_ref[...]
    a_half = A_HEAD_DIM // ROT_FRACTION_DIV // 2
    i_half = IDX_DIM // ROT_FRACTION_DIV // 2

    def head_norm(x, g, n_real):
        ms = jnp.sum(x * x, axis=-1, keepdims=True) * (1.0 / n_real)
        return x * lax.rsqrt(ms + NORM_EPS) * g

    for h in range(A_HEADS):
        sl = slice(h * A_HEAD_DIM, (h + 1) * A_HEAD_DIM)
        q = head_norm(p_ref[:, OFF_AQ + h * 128:OFF_AQ + (h + 1) * 128], gq_ref[...], A_HEAD_DIM)
        q = q * ca + _rotate_half(q, a_half, A_HEAD_DIM) * sa
        qT_ref[sl, :] = (q * (A_HEAD_DIM ** -0.5 * LOG2E)).T.astype(BF16)
        k = head_norm(p_ref[:, OFF_AK + h * 128:OFF_AK + (h + 1) * 128], gk_ref[...], A_HEAD_DIM)
        ka_ref[:, sl] = (k * ca + _rotate_half(k, a_half, A_HEAD_DIM) * sa).astype(BF16)
        vT_ref[sl, :] = p_ref[:, OFF_AV + h * 128:OFF_AV + (h + 1) * 128].T.astype(BF16)

    row = lax.broadcasted_iota(I32, (LANES, p_ref.shape[0]), 0)
    for j in range(IDX_HEADS * IDX_DIM // LANES):
        qi = p_ref[:, OFF_IQ + j * LANES:OFF_IQ + (j + 1) * LANES]
        qt = (qi * ci + _rotate_half(qi, i_half, IDX_DIM) * si).T
        qiT_ref[(2 * j) * LANES:(2 * j + 1) * LANES, :] = jnp.where(row < IDX_DIM, qt, 0.0).astype(BF16)
        qiT_ref[(2 * j + 1) * LANES:(2 * j + 2) * LANES, :] = jnp.where(row >= IDX_DIM, qt, 0.0).astype(BF16)
    ki = head_norm(p_ref[:, OFF_IK:OFF_IK + LANES], gi_ref[...], IDX_DIM)
    ki = ki * ci + _rotate_half(ki, i_half, IDX_DIM) * si
    ki_ref[...] = (ki + pltpu.roll(ki, IDX_DIM, 1)).astype(BF16)
    w = p_ref[:, OFF_IW:OFF_IW + LANES] * (IDX_DIM ** -0.5 * IDX_HEADS ** -0.5)
    wT_ref[...] = w.T[0:SUBLANES, :]

    sq_ref[...] = (sqi_ref[...].astype(F32) * (SB_HEAD_DIM ** -0.5 * LOG2E)).astype(BF16)
    for h in range(SB_HEADS):
        k_t = ski_ref[:, h * 128:(h + 1) * 128].astype(F32).T.astype(BF16)
        sb_ck = skT_ref.shape[-1]
        for c in range(skT_ref.shape[1]):
            skT_ref[h, c] = k_t[:, c * sb_ck:(c + 1) * sb_ck]
    sv_ref[...] = svi_ref[...]


def _prep(proj_a, proj_b, ca, sa, ci, si, g_qa, g_ka, g_kidx, bsz, seq):
    t = proj_a.shape[0]
    ck = min(CHUNK, seq)
    sb_ck = min(SB_CK, seq)
    nck = seq // ck
    gi = jnp.concatenate([g_kidx, jnp.ones((LANES - IDX_DIM,), F32)]).reshape(1, LANES)
    tok = lambda b, i: (b * nck + i, 0)
    fixed = lambda b, i: (0, 0)
    tbl = pl.BlockSpec((ck, LANES), tok)
    gsp = pl.BlockSpec((1, LANES), fixed)
    sb_col = (OFF_SB - SPLIT_COL) // SB_WIDTH
    sb_in = lambda n: pl.BlockSpec((ck, SB_WIDTH), lambda b, i: (b * nck + i, sb_col + n))
    return pl.pallas_call(
        _prep_kernel,
        grid=(bsz, nck),
        in_specs=[pl.BlockSpec((ck, SPLIT_COL), tok), sb_in(0), sb_in(1), sb_in(2),
                  tbl, tbl, tbl, tbl, gsp, gsp, gsp],
        out_specs=[
            pl.BlockSpec((None, A_WIDTH, ck), lambda b, i: (b, 0, i)),
            pl.BlockSpec((None, IDX_HEADS * LANES, ck), lambda b, i: (b, 0, i)),
            pl.BlockSpec((None, SUBLANES, ck), lambda b, i: (b, 0, i)),
            pl.BlockSpec((None, ck, LANES), lambda b, i: (b, i, 0)),
            pl.BlockSpec((None, ck, A_WIDTH), lambda b, i: (b, i, 0)),
            pl.BlockSpec((None, None, A_WIDTH, ck), lambda b, i: (b, i, 0, 0)),
            pl.BlockSpec((ck, SB_WIDTH), tok),
            pl.BlockSpec((None, SB_HEADS, ck // sb_ck, SB_HEAD_DIM, sb_ck), lambda b, i: (b, 0, i, 0, 0)),
            pl.BlockSpec((None, ck, SB_WIDTH), lambda b, i: (b, i, 0)),
        ],
        out_shape=[
            jax.ShapeDtypeStruct((bsz, A_WIDTH, seq), BF16),
            jax.ShapeDtypeStruct((bsz, IDX_HEADS * LANES, seq), BF16),
            jax.ShapeDtypeStruct((bsz, SUBLANES, seq), F32),
            jax.ShapeDtypeStruct((bsz, seq, LANES), BF16),
            jax.ShapeDtypeStruct((bsz, seq, A_WIDTH), BF16),
            jax.ShapeDtypeStruct((bsz, nck, A_WIDTH, ck), BF16),
            jax.ShapeDtypeStruct((t, SB_WIDTH), BF16),
            jax.ShapeDtypeStruct((bsz, SB_HEADS, seq // sb_ck, SB_HEAD_DIM, sb_ck), BF16),
            jax.ShapeDtypeStruct((bsz, seq, SB_WIDTH), BF16),
        ],
        compiler_params=_cparams(("parallel", "parallel")),
        name="mixer_prep",
    )(proj_a, proj_b, proj_b, proj_b, ca, sa, ci, si, g_qa.reshape(1, LANES), g_ka.reshape(1, LANES), gi)


def _order_key(f):
    b = pltpu.bitcast(f, I32)
    return b ^ ((b >> 31) & 0x7FFFFFFF)


def _key_float(k):
    return pltpu.bitcast(k ^ ((k >> 31) & 0x7FFFFFFF), F32)


def _normal_key(k):
    k = jnp.where(jnp.logical_and(k > 0, k < KEY_MIN_NORMAL), KEY_MIN_NORMAL, k)
    return jnp.where(jnp.logical_and(k < 0, k >= -KEY_MIN_NORMAL), 0, k)


def _dsa_kernel(qT_ref, qiT_ref, wT_ref, ki_ref, ka_ref, vT_ref, tri_ref, o_ref, sc_ref, red_ref, m_ref, acc_ref,
                *, tq, ck, grp, k_sel):
    i = pl.program_id(1)
    n_keys = (i + 1) * tq
    n_chunks = (n_keys + ck - 1) // ck
    n_groups = (n_keys + grp - 1) // grp
    q_pos = i * tq + lax.broadcasted_iota(I32, (1, tq), 1)
    neg_inf = jnp.full((ck, tq), -jnp.inf, F32)

    def chunk_start(c):
        return pl.multiple_of(c * ck, ck)

    def key_pos(c):
        return c * ck + lax.broadcasted_iota(I32, (ck, tq), 0)

    def score_chunk(c, diag):
        ks = chunk_start(c)
        kic = ki_ref[pl.ds(ks, ck), :]
        sc = None
        for h in range(IDX_HEADS):
            d = jnp.dot(kic, qiT_ref[h * LANES:(h + 1) * LANES, :], preferred_element_type=F32)
            term = jnp.maximum(d, 0.0) * wT_ref[h:h + 1, :]
            sc = term if sc is None else sc + term
        if diag:
            causal = key_pos(c) <= q_pos
            sc_hi = jnp.where(causal, sc, -jnp.inf)
            sc_lo = jnp.where(causal, sc, jnp.inf)
        else:
            sc_hi = sc_lo = sc
        sc_ref[pl.ds(ks, ck), :] = sc_hi
        fold = lambda a, op: op(a.reshape(ck // SUBLANES, SUBLANES, tq), axis=0)
        return fold(sc_hi, jnp.max), fold(sc_lo, jnp.min)

    def score_body(c, carry):
        mx, mn = score_chunk(c, False)
        return jnp.maximum(carry[0], mx), jnp.minimum(carry[1], mn)

    init = (jnp.full((SUBLANES, tq), -jnp.inf, F32), jnp.full((SUBLANES, tq), jnp.inf, F32))
    mx, mn = lax.fori_loop(0, n_chunks - 1, score_body, init)
    mx_d, mn_d = score_chunk(n_chunks - 1, True)
    def fold_rows(x, op):
        red_ref[...] = x
        r = red_ref[0:1, :]
        for j in range(1, SUBLANES):
            r = op(r, red_ref[j:j + 1, :])
        return r

    row_max = fold_rows(jnp.maximum(mx, mx_d), jnp.maximum)
    row_min = fold_rows(jnp.minimum(mn, mn_d), jnp.minimum)

    def pad_body(c, carry):
        sc_ref[pl.ds(chunk_start(c), ck), :] = neg_inf
        return carry
    lax.fori_loop(n_chunks, n_groups * (grp // ck), pad_body, 0)

    ones = jnp.ones((SUBLANES, grp), BF16)
    max_groups = sc_ref.shape[0] // grp

    def count_ge(t):
        for ng in range(1, max_groups + 1):
            @pl.when(n_groups == ng)
            def _(ng=ng):
                acc = None
                for g in range(ng):
                    ind = jnp.where(sc_ref[g * grp:(g + 1) * grp, :] >= t, 1.0, 0.0).astype(BF16)
                    part = jnp.dot(ones, ind, preferred_element_type=F32)
                    acc = part if acc is None else acc + part
                red_ref[...] = acc
        return red_ref[0:1, :].astype(I32)

    n_causal = q_pos + 1
    take_all = n_causal <= k_sel
    lo0 = _normal_key(_order_key(row_min))
    hi0 = _normal_key(_order_key(row_max)) + 1
    thr0 = jnp.full((1, tq), F32_LOWEST, F32)
    done0 = jnp.where(take_all, 1, 0)
    zero = jnp.zeros((1, tq), I32)

    def search_cond(st):
        it, _, _, _, _, done, _ = st
        return jnp.logical_and(it < SEARCH_PASS_CAP, jnp.min(done) == 0)

    def search_body(st):
        it, lo, hi, c_hi, thr, done, tied = st
        cross = jnp.logical_and(lo < 0, hi > 0)
        gap = hi - lo
        zero_probe = jnp.logical_and(it == 1, lo == 0)
        mid = _normal_key(jnp.where(cross, 0, jnp.where(zero_probe, KEY_MIN_NORMAL, lo + (gap >> 1))))
        v_mid = _normal_key(_order_key(0.5 * _key_float(lo) + 0.5 * _key_float(hi)))
        by_value = jnp.logical_and(jnp.logical_and(it < VALUE_PASSES, jnp.logical_not(jnp.logical_or(cross, zero_probe))),
                                   jnp.logical_and(v_mid > lo, v_mid < hi))
        mid = jnp.where(by_value, v_mid, mid)
        no_pivot = jnp.logical_or(gap == 1, jnp.logical_or(mid >= hi, mid <= lo))
        adjacent = jnp.logical_and(no_pivot, jnp.logical_not(cross))
        pivot = _key_float(mid)
        cnt = count_ge(pivot)
        live = done == 0
        hit = jnp.logical_and(live, jnp.logical_and(cnt == k_sel, jnp.logical_not(adjacent)))
        end = jnp.logical_and(live, adjacent)
        move = jnp.logical_and(live, jnp.logical_not(jnp.logical_or(hit, end)))
        up = jnp.logical_and(move, cnt > k_sel)
        dn = jnp.logical_and(move, cnt < k_sel)
        thr = jnp.where(hit, pivot, jnp.where(end, _key_float(lo), thr))
        tied = jnp.where(end, 1, tied)
        done = jnp.where(jnp.logical_or(hit, end), 1, done)
        lo = jnp.where(up, mid, lo)
        hi = jnp.where(dn, mid, hi)
        c_hi = jnp.where(dn, cnt, c_hi)
        return it + 1, lo, hi, c_hi, thr, done, tied

    st = lax.while_loop(search_cond, search_body, (jnp.int32(0), lo0, hi0, zero, thr0, done0, zero))
    _, _, _, c_hi, thr, _, tied = st

    room = jnp.where(tied > 0, k_sel - c_hi, jnp.int32(2 ** 30)).astype(F32)

    @pl.when(jnp.max(tied) > 0)
    def _():
        red_ref[0:1, :] = jnp.zeros((1, tq), F32)

        def drop_excess(c, carry):
            ks = chunk_start(c)
            s = sc_ref[pl.ds(ks, ck), :]
            is_tied = s == thr
            ind = jnp.where(is_tied, 1.0, 0.0).astype(BF16)
            pre = jnp.dot(tri_ref[...], ind, preferred_element_type=F32)
            before = red_ref[0:1, :]
            rank = pre[0:ck, :] + before
            sc_ref[pl.ds(ks, ck), :] = jnp.where(is_tied, jnp.where(rank >= room, -jnp.inf, s), s)
            red_ref[0:1, :] = before + pre[ck:ck + 1, :]
            return carry
        lax.fori_loop(0, n_chunks, drop_excess, 0)

    m_ref[...] = jnp.full(m_ref.shape, M_INIT, F32)
    acc_ref[...] = jnp.zeros(acc_ref.shape, F32)
    heads = [slice(h * A_HEAD_DIM, (h + 1) * A_HEAD_DIM) for h in range(A_HEADS)]
    ones_rows = jnp.ones((TRI_PAD, ck), BF16)

    def att_body(c, carry):
        ks = chunk_start(c)
        sel = sc_ref[pl.ds(ks, ck), :] >= thr
        logits = [jnp.dot(ka_ref[pl.ds(ks, ck), hs], qT_ref[hs, :], preferred_element_type=F32) for hs in heads]
        for h, hs in enumerate(heads):
            m_old = m_ref[h, 0:1, :]
            m_new = jnp.maximum(m_old, jnp.max(jnp.where(sel, logits[h], NEG_BIG), axis=0, keepdims=True))
            alpha = jnp.exp2(m_old - m_new)
            p = jnp.where(sel, jnp.exp2(logits[h] - m_new), 0.0).astype(BF16)
            m_ref[h, 0:1, :] = m_new
            lhs = jnp.concatenate([vT_ref[c, hs, :], ones_rows], axis=0)
            acc_ref[h] = acc_ref[h] * alpha + jnp.dot(lhs, p, preferred_element_type=F32)
        return carry

    lax.fori_loop(0, n_chunks, att_body, 0)
    for h, hs in enumerate(heads):
        denom = acc_ref[h, A_HEAD_DIM:A_HEAD_DIM + 1, :]
        o_ref[:, hs] = (acc_ref[h, 0:A_HEAD_DIM, :] / denom).T.astype(o_ref.dtype)


def _dsa_attention(qT, qiT, wT, ki, ka, vT, bsz, seq, tq):
    ck = min(CHUNK, seq)
    nck = seq // ck
    grp = min(2 * ck, seq)
    k_sel = min(TOPK_MAX, seq // 4)
    kern = functools.partial(_dsa_kernel, tq=tq, ck=ck, grp=grp, k_sel=k_sel)
    once = pl.Buffered(1)
    out = pl.pallas_call(
        kern,
        grid=(bsz, seq // tq),
        in_specs=[
            pl.BlockSpec((None, A_WIDTH, tq), lambda b, i: (b, 0, i)),
            pl.BlockSpec((None, IDX_HEADS * LANES, tq), lambda b, i: (b, 0, i)),
            pl.BlockSpec((None, SUBLANES, tq), lambda b, i: (b, 0, i)),
            pl.BlockSpec((None, seq, LANES), lambda b, i: (b, 0, 0), pipeline_mode=once),
            pl.BlockSpec((None, seq, A_WIDTH), lambda b, i: (b, 0, 0), pipeline_mode=once),
            pl.BlockSpec((None, nck, A_WIDTH, ck), lambda b, i: (b, 0, 0, 0), pipeline_mode=once),
            pl.BlockSpec((ck + TRI_PAD, ck), lambda b, i: (0, 0), pipeline_mode=once),
        ],
        out_specs=pl.BlockSpec((None, tq, A_WIDTH), lambda b, i: (b, i, 0)),
        out_shape=jax.ShapeDtypeStruct((bsz, seq, A_WIDTH), BF16),
        scratch_shapes=[pltpu.VMEM((seq, tq), F32), pltpu.VMEM((SUBLANES, tq), F32),
                        pltpu.VMEM((A_HEADS, SUBLANES, tq), F32),
                        pltpu.VMEM((A_HEADS, A_HEAD_DIM + TRI_PAD, tq), F32)],
        compiler_params=_cparams(("parallel", "arbitrary")),
        name="dsa_attention",
    )(qT, qiT, wT, ki, ka, vT, _prefix_matrix(ck))
    return out.reshape(bsz * seq, A_WIDTH)


def _prefix_matrix(n):
    r = jnp.arange(n + TRI_PAD)[:, None]
    c = jnp.arange(n)[None, :]
    return jnp.logical_or(c < r, r >= n).astype(BF16)


SB_CK = 256
SB_TQ = 1024
SB_STEP = 4


def _sb_kernel(q_ref, kT_ref, v_ref, u_ref, o_ref, acc_ref, carry_ref, *, tq, ck):
    i = pl.program_id(2)
    q = q_ref[...]
    acc_ref[...] = jnp.zeros(acc_ref.shape, F32)
    carry_ref[...] = jnp.zeros(carry_ref.shape, F32)
    per = tq // ck
    row = i * tq + lax.broadcasted_iota(I32, (tq, ck), 0)
    col = lax.broadcasted_iota(I32, (tq, ck), 1)

    def chunk(c, diag):
        z = jnp.dot(q, kT_ref[c], preferred_element_type=F32)
        neg_abs = pltpu.bitcast(pltpu.bitcast(z, I32) | jnp.int32(-2 ** 31), F32)
        neg_soft = jnp.log(1.0 + jnp.exp2(neg_abs)) * (-LOG2E)
        lk = neg_soft - jnp.maximum(z, 0.0)
        if diag:
            mask = (c * ck + col) < row
            lk = jnp.where(mask, lk, 0.0)
        cum = jnp.dot(lk.astype(BF16), u_ref[...], preferred_element_type=F32)
        a = jnp.exp2(z + cum)
        if diag:
            a = jnp.where(mask, a, 0.0)
        vs = pl.multiple_of(c * ck, ck)
        pv = jnp.dot(a.astype(BF16), v_ref[pl.ds(vs, ck), :], preferred_element_type=F32)
        return pv, cum[:, 0:1]

    def fold(parts):
        for pv, total in parts:
            carry = carry_ref[...]
            acc_ref[...] += jnp.exp2(carry) * pv
            carry_ref[...] = carry + jnp.broadcast_to(total, carry.shape)

    step = min(SB_STEP, per)
    for d0 in reversed(range(0, per, step)):
        fold([chunk(i * per + d0 + d, True) for d in reversed(range(step))])

    def body(s, carry):
        first = i * per - 1 - s * step
        fold([chunk(first - d, False) for d in range(step)])
        return carry

    lax.fori_loop(0, i * (per // step), body, 0)
    o_ref[...] = acc_ref[...].astype(o_ref.dtype)


def _cumsum_matrix(n):
    j = jnp.arange(n)[:, None]
    s = jnp.arange(n)[None, :]
    return (j >= s).astype(BF16)


def _sb_attention(sq, skT, sv, bsz, seq):
    t = bsz * seq
    tq = min(SB_TQ, seq)
    ck = min(SB_CK, seq)
    nq = seq // tq
    return pl.pallas_call(
        functools.partial(_sb_kernel, tq=tq, ck=ck),
        grid=(bsz, SB_HEADS, nq),
        in_specs=[
            pl.BlockSpec((tq, SB_HEAD_DIM), lambda b, h, i: (b * nq + i, h)),
            pl.BlockSpec((None, None, seq // ck, SB_HEAD_DIM, ck), lambda b, h, i: (b, h, 0, 0, 0)),
            pl.BlockSpec((None, seq, SB_HEAD_DIM), lambda b, h, i: (b, 0, h)),
            pl.BlockSpec((ck, ck), lambda b, h, i: (0, 0)),
        ],
        out_specs=pl.BlockSpec((tq, SB_HEAD_DIM), lambda b, h, i: (b * nq + i, h)),
        out_shape=jax.ShapeDtypeStruct((t, SB_WIDTH), BF16),
        scratch_shapes=[pltpu.VMEM((tq, SB_HEAD_DIM), F32), pltpu.VMEM((tq, LANES), F32)],
        compiler_params=_cparams(("parallel", "parallel", "arbitrary")),
        name="stickbreak",
    )(sq, skT, sv, _cumsum_matrix(ck))


CB_HALO = 32
CC_HALO = 16
CONV_ROWS = 64


def _conv_kernel(ga_ref, gb_ref, hga_ref, hgb_ref, cb_ref, cc_ref, cx_ref, hcc_ref, hcx_ref,
                 wb_ref, bb_ref, lng_ref, lnb_ref, wc_ref, ob_ref, oc_ref, buf_ref, u_ref, buf2_ref, *, ts, seq):
    i = pl.program_id(0)
    keep = jnp.where((i * ts) % seq == 0, 0.0, 1.0)

    f32 = lambda ref: ref[...].astype(F32)
    buf_ref[0:CB_HALO, :] = f32(hga_ref) * jax.nn.sigmoid(f32(hgb_ref)) * keep
    buf_ref[CB_HALO:, :] = f32(ga_ref) * jax.nn.sigmoid(f32(gb_ref))
    off0 = CB_HALO - (CB_CONV - 1)
    for r in range(ts // CONV_ROWS):
        t0 = r * CONV_ROWS
        for g in range(CB_WIDTH // LANES):
            ls = slice(g * LANES, (g + 1) * LANES)
            acc = jnp.broadcast_to(bb_ref[:, ls], (CONV_ROWS, LANES))
            for s in range(SUBLANES):
                rows = CONV_ROWS + (SUBLANES if s else 0)
                part = None
                for k in range(CB_CONV):
                    if (off0 + k) % SUBLANES != s:
                        continue
                    base = t0 + (off0 + k) - s
                    term = wb_ref[k:k + 1, ls] * buf_ref[base:base + rows, ls]
                    part = term if part is None else part + term
                acc = acc + part[s:s + CONV_ROWS, :]
            u_ref[t0:t0 + CONV_ROWS, ls] = acc
    u = u_ref[...]
    mu = jnp.mean(u, axis=-1, keepdims=True)
    uc = u - mu
    y = uc * lax.rsqrt(jnp.mean(uc * uc, axis=-1, keepdims=True) + NORM_EPS) * lng_ref[...] + lnb_ref[...]
    ob_ref[...] = (y * jax.nn.sigmoid(y)).astype(ob_ref.dtype)

    buf2_ref[0:CC_HALO, :] = f32(hcc_ref) * f32(hcx_ref) * keep
    buf2_ref[CC_HALO:, :] = f32(cc_ref) * f32(cx_ref)
    conv = jnp.zeros((ts, CC_WIDTH), F32)
    for k in range(CC_CONV):
        base = CC_HALO - (CC_CONV - 1) + k
        conv = conv + wc_ref[k:k + 1, :] * buf2_ref[base:base + ts, :]
    oc_ref[...] = (f32(cb_ref) * conv).astype(oc_ref.dtype)


def _conv_mixers(proj, wb, bb, lng, lnb, wc, seq, ts):
    t = proj.shape[0]
    w = CB_WIDTH
    c0 = (OFF_GLU - SPLIT_COL) // w
    c1 = (OFF_CC - SPLIT_COL) // w
    cur = lambda col: pl.BlockSpec((ts, w), lambda i: (i, col))
    halo = lambda rows, col: pl.BlockSpec((rows, w), lambda i: (jnp.maximum(i * (ts // rows) - 1, 0), col))
    full = lambda r: pl.BlockSpec((r, w), lambda i: (0, 0))
    wb_p = jnp.pad(wb, ((0, CB_HALO - CB_CONV), (0, 0)))
    wc_p = jnp.pad(wc, ((0, SUBLANES - CC_CONV), (0, 0)))
    return pl.pallas_call(
        functools.partial(_conv_kernel, ts=ts, seq=seq),
        grid=(t // ts,),
        in_specs=[cur(c0), cur(c0 + 1), halo(CB_HALO, c0), halo(CB_HALO, c0 + 1),
                  cur(c1), cur(c1 + 1), cur(c1 + 2), halo(CC_HALO, c1 + 1), halo(CC_HALO, c1 + 2),
                  full(CB_HALO), full(1), full(1), full(1), full(SUBLANES)],
        out_specs=[pl.BlockSpec((ts, w), lambda i: (i, 0))] * 2,
        out_shape=[jax.ShapeDtypeStruct((t, w), BF16)] * 2,
        scratch_shapes=[pltpu.VMEM((CB_HALO + ts, w), F32), pltpu.VMEM((ts, w), F32),
                        pltpu.VMEM((CC_HALO + ts, w), F32)],
        compiler_params=_cparams(("parallel",)),
        name="conv_mixers",
    )(proj, proj, proj, proj, proj, proj, proj, proj, proj,
      wb_p, bb.reshape(1, w), lng.reshape(1, w), lnb.reshape(1, w), wc_p)


def _merge_kernel(x_ref, oa_ref, ob_ref, oc_ref, od_ref, g0_ref, g1_ref, g2_ref, g3_ref, bg_ref,
                  wa_ref, wb_ref, wc_ref, wd_ref, wm_ref, o_ref):
    merged = None
    branches = ((oa_ref, wa_ref, g0_ref), (ob_ref, wb_ref, g1_ref), (oc_ref, wc_ref, g2_ref), (od_ref, wd_ref, g3_ref))
    for n, (b_ref, w_ref, g_ref) in enumerate(branches):
        y = jnp.dot(b_ref[...], w_ref[...], preferred_element_type=F32)
        term = jax.nn.sigmoid(g_ref[...].astype(F32) + bg_ref[n:n + 1, :]) * y
        merged = term if merged is None else merged + term
    o_ref[...] = x_ref[...] + jnp.dot(merged.astype(BF16), wm_ref[...], preferred_element_type=F32)


def _merge(x2d, oa, ob, oc, od, proj, b_gate, w_oa, w_ob, w_oc, w_od, w_merge, tm):
    t, d = x2d.shape
    g0 = (OFF_GATE - SPLIT_COL) // d
    row = lambda i: (i, 0)
    fixed = lambda i: (0, 0)
    br = pl.BlockSpec((tm, A_WIDTH), row)
    gate = lambda n: pl.BlockSpec((tm, d), lambda i: (i, g0 + n))
    wsp = pl.BlockSpec((A_WIDTH, d), fixed)
    bf = lambda w: w.astype(BF16)
    return pl.pallas_call(
        _merge_kernel,
        grid=(t // tm,),
        in_specs=[pl.BlockSpec((tm, d), row), br, br, br, br, gate(0), gate(1), gate(2), gate(3),
                  pl.BlockSpec((N_BRANCH, d), fixed), wsp, wsp, wsp, wsp, pl.BlockSpec((d, d), fixed)],
        out_specs=pl.BlockSpec((tm, d), row),
        out_shape=jax.ShapeDtypeStruct((t, d), F32),
        compiler_params=_cparams(("parallel",)),
        name="merge",
    )(x2d, oa, ob, oc, od, proj, proj, proj, proj, b_gate.reshape(N_BRANCH, d),
      bf(w_oa), bf(w_ob), bf(w_oc), bf(w_od), bf(w_merge))


FFN_HALO = 8


def _ffn_kernel(x_ref, hx_ref, g_ref, wg_ref, wu_ref, cw_ref, wd_ref, o_ref, h_ref, gp_ref, acc_ref, *, tm, seq):
    i = pl.program_id(0)
    f = pl.program_id(1)

    def norm(x):
        ms = jnp.mean(x * x, axis=-1, keepdims=True)
        return (x * lax.rsqrt(ms + NORM_EPS) * g_ref[...]).astype(BF16)

    @pl.when(f == 0)
    def _():
        h_ref[0:FFN_HALO, :] = norm(hx_ref[...])
        h_ref[FFN_HALO:, :] = norm(x_ref[...])
        acc_ref[...] = x_ref[...]

    keep = jnp.where((i * tm) % seq == 0, 0.0, 1.0)
    gp_ref[0:FFN_HALO, :] = jnp.dot(h_ref[0:FFN_HALO, :], wg_ref[...], preferred_element_type=F32) * keep
    gp_ref[FFN_HALO:, :] = jnp.dot(h_ref[FFN_HALO:, :], wg_ref[...], preferred_element_type=F32)
    gt = jnp.zeros((tm, gp_ref.shape[1]), F32)
    for k in range(FFN_CONV):
        base = FFN_HALO - (FFN_CONV - 1) + k
        gt = gt + cw_ref[k:k + 1, :] * gp_ref[base:base + tm, :]
    up = jnp.dot(h_ref[FFN_HALO:, :], wu_ref[...], preferred_element_type=F32)
    act = (gt * jax.nn.sigmoid(gt) * up).astype(BF16)
    acc_ref[...] += jnp.dot(act, wd_ref[...], preferred_element_type=F32)

    @pl.when(f == pl.num_programs(1) - 1)
    def _():
        o_ref[...] = acc_ref[...]


def _ffn(x2d, g, wg, wu, cw, wd, seq, tm, tf):
    t, d = x2d.shape
    ff = wg.shape[1]
    cw_p = jnp.pad(cw, ((0, SUBLANES - FFN_CONV), (0, 0)))
    return pl.pallas_call(
        functools.partial(_ffn_kernel, tm=tm, seq=seq),
        grid=(t // tm, ff // tf),
        in_specs=[
            pl.BlockSpec((tm, d), lambda i, f: (i, 0)),
            pl.BlockSpec((FFN_HALO, d), lambda i, f: (jnp.maximum(i * (tm // FFN_HALO) - 1, 0), 0)),
            pl.BlockSpec((1, d), lambda i, f: (0, 0)),
            pl.BlockSpec((d, tf), lambda i, f: (0, f)),
            pl.BlockSpec((d, tf), lambda i, f: (0, f)),
            pl.BlockSpec((SUBLANES, tf), lambda i, f: (0, f)),
            pl.BlockSpec((tf, d), lambda i, f: (f, 0)),
        ],
        out_specs=pl.BlockSpec((tm, d), lambda i, f: (i, 0)),
        out_shape=jax.ShapeDtypeStruct((t, d), F32),
        scratch_shapes=[pltpu.VMEM((FFN_HALO + tm, d), BF16), pltpu.VMEM((FFN_HALO + tm, tf), F32),
                        pltpu.VMEM((tm, d), F32)],
        compiler_params=_cparams(("parallel", "arbitrary")),
        name="ffn",
    )(x2d, x2d, g.reshape(1, d), wg.astype(BF16), wu.astype(BF16), cw_p, wd.astype(BF16))


def _tiles(seq):
    big = min(seq, 512)
    return dict(
        inproj_tm=min(seq, 1024), inproj_tn=1024,
        conv_ts=big, merge_tm=big, ffn_tm=big, ffn_tf=D_FF // 2,
        dsa_tq=min(seq, 512),
    )


def kernel(x, positions, g_mix, w_in, b_gate, g_qa, g_ka, g_kidx, w_oa, cb_conv_w, cb_conv_b, cb_ln_g, cb_ln_b,
           w_ob, cc_conv_w, w_oc, w_od, w_merge, g_ffn, w_ffn_gate, w_ffn_up, ffn_conv_w, w_ffn_down):
    bsz, seq, dm = x.shape
    depth = w_in.shape[0]
    assert dm == D_MODEL and seq % min(CHUNK, seq) == 0 and seq % LANES == 0
    tl = _tiles(seq)
    t = bsz * seq

    cos_a, sin_a = _rope_tables(positions, A_HEAD_DIM)
    cos_i, sin_i = _rope_tables(positions, IDX_DIM)
    ca, sa = _rope_lane_tables(cos_a, sin_a, A_HEAD_DIM)
    ci, si = _rope_lane_tables(cos_i, sin_i, IDX_DIM)

    x2d = x.reshape(t, dm)
    for l in range(depth):
        w_packed = _pack_w_in(jnp.swapaxes(w_in[l], 0, 1))
        tm, tn = tl["inproj_tm"], tl["inproj_tn"]
        proj_a = _inproj(x2d, g_mix[l], w_packed, tm, tn, 0, SPLIT_COL, F32)
        proj = _inproj(x2d, g_mix[l], w_packed, tm, tn, SPLIT_COL, N_PACK - SPLIT_COL, BF16)
        qT, qiT, wT, ki, ka, vT, sq, skT, sv = _prep(proj_a, proj, ca, sa, ci, si, g_qa[l], g_ka[l], g_kidx[l],
                                                     bsz, seq)
        o_a = _dsa_attention(qT, qiT, wT, ki, ka, vT, bsz, seq, tl["dsa_tq"])
        o_d = _sb_attention(sq, skT, sv, bsz, seq)
        o_b, o_c = _conv_mixers(proj, cb_conv_w[l], cb_conv_b[l], cb_ln_g[l], cb_ln_b[l], cc_conv_w[l],
                                seq, tl["conv_ts"])
        x2d = _merge(x2d, o_a, o_b, o_c, o_d, proj, b_gate[l], w_oa[l], w_ob[l], w_oc[l], w_od[l], w_merge[l],
                     tl["merge_tm"])
        x2d = _ffn(x2d, g_ffn[l], w_ffn_gate[l], w_ffn_up[l], ffn_conv_w[l], w_ffn_down[l], seq,
                   tl["ffn_tm"], tl["ffn_tf"])
    return x2d.reshape(bsz, seq, dm)
```

```python
import functools
import math

import jax
import jax.numpy as jnp
from jax import lax
from jax.experimental import pallas as pl
from jax.experimental.pallas import tpu as pltpu

F32 = jnp.float32
BF16 = jnp.bfloat16
I32 = jnp.int32

D_MODEL = 1024
N_BRANCH = 4
A_HEADS = 4
A_HEAD_DIM = 128
A_WIDTH = A_HEADS * A_HEAD_DIM
IDX_HEADS = 4
IDX_DIM = 64
TOPK_MAX = 256
CB_WIDTH = 512
CB_CONV = 31
CC_WIDTH = 512
CC_CONV = 3
SB_HEADS = 4
SB_HEAD_DIM = 128
SB_WIDTH = SB_HEADS * SB_HEAD_DIM
D_FF = 2816
FFN_CONV = 3
ROPE_THETA = 500000.0
ROT_FRACTION_DIV = 4
NORM_EPS = 1e-6

LANES = 128
SUBLANES = 8
VMEM_LIMIT = 56 * 1024 * 1024

LOG2E = math.log2(math.e)
NEG_BIG = -1e30
M_INIT = -1e20
F32_LOWEST = float(jnp.finfo(jnp.float32).min)
KEY_MIN_NORMAL = 0x00800000
TRI_PAD = 16
VALUE_PASSES = 12
SEARCH_PASS_CAP = VALUE_PASSES + 36

OFF_AQ = 0
OFF_AK = 512
OFF_AV = 1024
OFF_IQ = 1536
OFF_IK = 1792
OFF_IW = 1920
OFF_GLU = 2048
OFF_CC = 3072
OFF_SB = 4608
OFF_GATE = 6144
N_PACK = 10240
SPLIT_COL = 2048
CHUNK = 512


def _cparams(sem):
    return pltpu.CompilerParams(dimension_semantics=sem, vmem_limit_bytes=VMEM_LIMIT)


def _trig_kernel(pos_ref, invf_ref, cos_ref, sin_ref):
    ang = pos_ref[...] * invf_ref[...]
    cos_ref[...] = jnp.cos(ang)
    sin_ref[...] = jnp.sin(ang)


def _rope_tables(positions, head_dim):
    rot = head_dim // ROT_FRACTION_DIV
    half = rot // 2
    inv_freq = ROPE_THETA ** (-(jnp.arange(half, dtype=F32) * 2.0) / rot)
    t = positions.size
    rows = t * half // LANES
    pos_rep = jnp.broadcast_to(positions.astype(F32).reshape(t, 1), (t, half)).reshape(rows, LANES)
    invf = jnp.tile(inv_freq, LANES // half).reshape(1, LANES)
    tr = min(rows, 512)
    cos, sin = pl.pallas_call(
        _trig_kernel,
        grid=(rows // tr,),
        in_specs=[pl.BlockSpec((tr, LANES), lambda i: (i, 0)), pl.BlockSpec((1, LANES), lambda i: (0, 0))],
        out_specs=[pl.BlockSpec((tr, LANES), lambda i: (i, 0))] * 2,
        out_shape=[jax.ShapeDtypeStruct((rows, LANES), F32)] * 2,
        compiler_params=_cparams(("parallel",)),
        name="rope_trig",
    )(pos_rep, invf)
    return cos.reshape(t, half), sin.reshape(t, half)


def _rope_lane_tables(cos, sin, head_dim):
    t, half = cos.shape
    rest = head_dim - 2 * half
    c = jnp.concatenate([cos, cos, jnp.ones((t, rest), F32)], axis=1)
    s = jnp.concatenate([sin, sin, jnp.zeros((t, rest), F32)], axis=1)
    reps = LANES // head_dim
    return jnp.tile(c, (1, reps)), jnp.tile(s, (1, reps))


def _inproj_kernel(x_ref, g_ref, w_ref, o_ref, h_ref):
    @pl.when(pl.program_id(1) == 0)
    def _():
        x = x_ref[...]
        ms = jnp.mean(x * x, axis=-1, keepdims=True)
        h_ref[...] = (x * lax.rsqrt(ms + NORM_EPS) * g_ref[...]).astype(BF16)

    o_ref[...] = jnp.dot(h_ref[...], w_ref[...], preferred_element_type=F32).astype(o_ref.dtype)


def _inproj(x2d, g, w_packed, tm, tn, col0, ncols, out_dtype):
    t, d = x2d.shape
    j0 = col0 // tn
    return pl.pallas_call(
        _inproj_kernel,
        grid=(t // tm, ncols // tn),
        in_specs=[
            pl.BlockSpec((tm, d), lambda i, j: (i, 0)),
            pl.BlockSpec((1, d), lambda i, j: (0, 0)),
            pl.BlockSpec((d, tn), lambda i, j: (0, j + j0)),
        ],
        out_specs=pl.BlockSpec((tm, tn), lambda i, j: (i, j)),
        out_shape=jax.ShapeDtypeStruct((t, ncols), out_dtype),
        scratch_shapes=[pltpu.VMEM((tm, d), BF16)],
        compiler_params=_cparams(("parallel", "arbitrary")),
        name="inproj",
    )(x2d, g.reshape(1, d), w_packed)


N_IN = 10052
OFF_IW_SRC = 1856
OFF_REST_SRC = 1860
PACK_FIRST = OFF_REST_SRC // LANES
PACK_ROT = (OFF_GLU - OFF_REST_SRC) % LANES


def _pack_kernel(a_ref, b_ref, o_ref):
    n = pl.program_id(0)
    row = lax.broadcasted_iota(I32, a_ref.shape, 0)
    a = a_ref[...]

    @pl.when(n < OFF_IK // LANES)
    def _():
        o_ref[...] = a.T.astype(BF16)

    @pl.when(n == OFF_IK // LANES)
    def _():
        o_ref[...] = jnp.where(row < IDX_DIM, a, 0.0).T.astype(BF16)

    @pl.when(n == OFF_IW // LANES)
    def _():
        iw = pltpu.roll(a, LANES - (OFF_IW_SRC - OFF_IK), 0)
        o_ref[...] = jnp.where(row < IDX_HEADS, iw, 0.0).T.astype(BF16)

    @pl.when(n >= OFF_GLU // LANES)
    def _():
        out = jnp.where(row < PACK_ROT, pltpu.roll(a, PACK_ROT, 0), pltpu.roll(b_ref[...], PACK_ROT, 0))
        o_ref[...] = out.T.astype(BF16)


def _pack_w_in(w_t):
    d = w_t.shape[1]
    first_rest = OFF_GLU // LANES

    def a_idx(n):
        return (jnp.where(n < first_rest, jnp.minimum(n, PACK_FIRST), n - first_rest + PACK_FIRST), 0)

    def b_idx(n):
        return (jnp.where(n < first_rest, PACK_FIRST, n - first_rest + PACK_FIRST + 1), 0)

    return pl.pallas_call(
        _pack_kernel,
        grid=(N_PACK // LANES,),
        in_specs=[pl.BlockSpec((LANES, d), a_idx), pl.BlockSpec((LANES, d), b_idx)],
        out_specs=pl.BlockSpec((d, LANES), lambda n: (0, n)),
        out_shape=jax.ShapeDtypeStruct((d, N_PACK), BF16),
        compiler_params=_cparams(("parallel",)),
        name="pack_w_in",
    )(w_t, w_t)


def _rotate_half(x, half, period):
    lane = lax.broadcasted_iota(I32, x.shape, 1)
    up = pltpu.roll(x, LANES - half, 1)
    dn = pltpu.roll(x, half, 1)
    return jnp.where((lane % period) < half, -up, dn)


def _prep_kernel(p_ref, sqi_ref, ski_ref, svi_ref, ca_ref, sa_ref, ci_ref, si_ref, gq_ref, gk_ref, gi_ref,
                 qT_ref, qiT_ref, wT_ref, ki_ref, ka_ref, vT_ref, sq_ref, skT_ref, sv_ref):
    ca, sa, ci, si = ca_ref[...], sa_ref[...], ci_ref[...], si_ref[...]
    a_half = A_HEAD_DIM // ROT_FRACTION_DIV // 2
    i_half = IDX_DIM // ROT_FRACTION_DIV // 2

    def head_norm(x, g, n_real):
        ms = jnp.sum(x * x, axis=-1, keepdims=True) * (1.0 / n_real)
        return x * lax.rsqrt(ms + NORM_EPS) * g

    for h in range(A_HEADS):
        sl = slice(h * A_HEAD_DIM, (h + 1) * A_HEAD_DIM)
        q = head_norm(p_ref[:, OFF_AQ + h * 128:OFF_AQ + (h + 1) * 128], gq_ref[...], A_HEAD_DIM)
        q = q * ca + _rotate_half(q, a_half, A_HEAD_DIM) * sa
        qT_ref[sl, :] = (q * (A_HEAD_DIM ** -0.5 * LOG2E)).T.astype(BF16)
        k = head_norm(p_ref[:, OFF_AK + h * 128:OFF_AK + (h + 1) * 128], gk_ref[...], A_HEAD_DIM)
        ka_ref[:, sl] = (k * ca + _rotate_half(k, a_half, A_HEAD_DIM) * sa).astype(BF16)
        vT_ref[sl, :] = p_ref[:, OFF_AV + h * 128:OFF_AV + (h + 1) * 128].T.astype(BF16)

    row = lax.broadcasted_iota(I32, (LANES, p_ref.shape[0]), 0)
    for j in range(IDX_HEADS * IDX_DIM // LANES):
        qi = p_ref[:, OFF_IQ + j * LANES:OFF_IQ + (j + 1) * LANES]
        qt = (qi * ci + _rotate_half(qi, i_half, IDX_DIM) * si).T
        qiT_ref[(2 * j) * LANES:(2 * j + 1) * LANES, :] = jnp.where(row < IDX_DIM, qt, 0.0).astype(BF16)
        qiT_ref[(2 * j + 1) * LANES:(2 * j + 2) * LANES, :] = jnp.where(row >= IDX_DIM, qt, 0.0).astype(BF16)
    ki = head_norm(p_ref[:, OFF_IK:OFF_IK + LANES], gi_ref[...], IDX_DIM)
    ki = ki * ci + _rotate_half(ki, i_half, IDX_DIM) * si
    ki_ref[...] = (ki + pltpu.roll(ki, IDX_DIM, 1)).astype(BF16)
    w = p_ref[:, OFF_IW:OFF_IW + LANES] * (IDX_DIM ** -0.5 * IDX_HEADS ** -0.5)
    wT_ref[...] = w.T[0:SUBLANES, :]

    sq_ref[...] = (sqi_ref[...].astype(F32) * (SB_HEAD_DIM ** -0.5 * LOG2E)).astype(BF16)
    for h in range(SB_HEADS):
        k_t = ski_ref[:, h * 128:(h + 1) * 128].astype(F32).T.astype(BF16)
        sb_ck = skT_ref.shape[-1]
        for c in range(skT_ref.shape[1]):
            skT_ref[h, c] = k_t[:, c * sb_ck:(c + 1) * sb_ck]
    sv_ref[...] = svi_ref[...]


def _prep(proj_a, proj_b, ca, sa, ci, si, g_qa, g_ka, g_kidx, bsz, seq):
    t = proj_a.shape[0]
    ck = min(CHUNK, seq)
    sb_ck = min(SB_CK, seq)
    nck = seq // ck
    gi = jnp.concatenate([g_kidx, jnp.ones((LANES - IDX_DIM,), F32)]).reshape(1, LANES)
    tok = lambda b, i: (b * nck + i, 0)
    fixed = lambda b, i: (0, 0)
    tbl = pl.BlockSpec((ck, LANES), tok)
    gsp = pl.BlockSpec((1, LANES), fixed)
    sb_col = (OFF_SB - SPLIT_COL) // SB_WIDTH
    sb_in = lambda n: pl.BlockSpec((ck, SB_WIDTH), lambda b, i: (b * nck + i, sb_col + n))
    return pl.pallas_call(
        _prep_kernel,
        grid=(bsz, nck),
        in_specs=[pl.BlockSpec((ck, SPLIT_COL), tok), sb_in(0), sb_in(1), sb_in(2),
                  tbl, tbl, tbl, tbl, gsp, gsp, gsp],
        out_specs=[
            pl.BlockSpec((None, A_WIDTH, ck), lambda b, i: (b, 0, i)),
            pl.BlockSpec((None, IDX_HEADS * LANES, ck), lambda b, i: (b, 0, i)),
            pl.BlockSpec((None, SUBLANES, ck), lambda b, i: (b, 0, i)),
            pl.BlockSpec((None, ck, LANES), lambda b, i: (b, i, 0)),
            pl.BlockSpec((None, ck, A_WIDTH), lambda b, i: (b, i, 0)),
            pl.BlockSpec((None, None, A_WIDTH, ck), lambda b, i: (b, i, 0, 0)),
            pl.BlockSpec((ck, SB_WIDTH), tok),
            pl.BlockSpec((None, SB_HEADS, ck // sb_ck, SB_HEAD_DIM, sb_ck), lambda b, i: (b, 0, i, 0, 0)),
            pl.BlockSpec((None, ck, SB_WIDTH), lambda b, i: (b, i, 0)),
        ],
        out_shape=[
            jax.ShapeDtypeStruct((bsz, A_WIDTH, seq), BF16),
            jax.ShapeDtypeStruct((bsz, IDX_HEADS * LANES, seq), BF16),
            jax.ShapeDtypeStruct((bsz, SUBLANES, seq), F32),
            jax.ShapeDtypeStruct((bsz, seq, LANES), BF16),
            jax.ShapeDtypeStruct((bsz, seq, A_WIDTH), BF16),
            jax.ShapeDtypeStruct((bsz, nck, A_WIDTH, ck), BF16),
            jax.ShapeDtypeStruct((t, SB_WIDTH), BF16),
            jax.ShapeDtypeStruct((bsz, SB_HEADS, seq // sb_ck, SB_HEAD_DIM, sb_ck), BF16),
            jax.ShapeDtypeStruct((bsz, seq, SB_WIDTH), BF16),
        ],
        compiler_params=_cparams(("parallel", "parallel")),
        name="mixer_prep",
    )(proj_a, proj_b, proj_b, proj_b, ca, sa, ci, si, g_qa.reshape(1, LANES), g_ka.reshape(1, LANES), gi)


def _order_key(f):
    b = pltpu.bitcast(f, I32)
    return b ^ ((b >> 31) & 0x7FFFFFFF)


def _key_float(k):
    return pltpu.bitcast(k ^ ((k >> 31) & 0x7FFFFFFF), F32)


def _normal_key(k):
    k = jnp.where(jnp.logical_and(k > 0, k < KEY_MIN_NORMAL), KEY_MIN_NORMAL, k)
    return jnp.where(jnp.logical_and(k < 0, k >= -KEY_MIN_NORMAL), 0, k)


def _dsa_kernel(qT_ref, qiT_ref, wT_ref, ki_ref, ka_ref, vT_ref, tri_ref, o_ref, sc_ref, red_ref, m_ref, acc_ref,
                *, tq, ck, grp, k_sel):
    i = pl.program_id(1)
    n_keys = (i + 1) * tq
    n_chunks = (n_keys + ck - 1) // ck
    n_groups = (n_keys + grp - 1) // grp
    q_pos = i * tq + lax.broadcasted_iota(I32, (1, tq), 1)
    neg_inf = jnp.full((ck, tq), -jnp.inf, F32)

    def chunk_start(c):
        return pl.multiple_of(c * ck, ck)

    def key_pos(c):
        return c * ck + lax.broadcasted_iota(I32, (ck, tq), 0)

    def score_chunk(c, diag):
        ks = chunk_start(c)
        kic = ki_ref[pl.ds(ks, ck), :]
        sc = None
        for h in range(IDX_HEADS):
            d = jnp.dot(kic, qiT_ref[h * LANES:(h + 1) * LANES, :], preferred_element_type=F32)
            term = jnp.maximum(d, 0.0) * wT_ref[h:h + 1, :]
            sc = term if sc is None else sc + term
        if diag:
            causal = key_pos(c) <= q_pos
            sc_hi = jnp.where(causal, sc, -jnp.inf)
            sc_lo = jnp.where(causal, sc, jnp.inf)
        else:
            sc_hi = sc_lo = sc
        sc_ref[pl.ds(ks, ck), :] = sc_hi
        fold = lambda a, op: op(a.reshape(ck // SUBLANES, SUBLANES, tq), axis=0)
        return fold(sc_hi, jnp.max), fold(sc_lo, jnp.min)

    def score_body(c, carry):
        mx, mn = score_chunk(c, False)
        return jnp.maximum(carry[0], mx), jnp.minimum(carry[1], mn)

    init = (jnp.full((SUBLANES, tq), -jnp.inf, F32), jnp.full((SUBLANES, tq), jnp.inf, F32))
    mx, mn = lax.fori_loop(0, n_chunks - 1, score_body, init)
    mx_d, mn_d = score_chunk(n_chunks - 1, True)
    def fold_rows(x, op):
        red_ref[...] = x
        r = red_ref[0:1, :]
        for j in range(1, SUBLANES):
            r = op(r, red_ref[j:j + 1, :])
        return r

    row_max = fold_rows(jnp.maximum(mx, mx_d), jnp.maximum)
    row_min = fold_rows(jnp.minimum(mn, mn_d), jnp.minimum)

    def pad_body(c, carry):
        sc_ref[pl.ds(chunk_start(c), ck), :] = neg_inf
        return carry
    lax.fori_loop(n_chunks, n_groups * (grp // ck), pad_body, 0)

    ones = jnp.ones((SUBLANES, grp), BF16)
    max_groups = sc_ref.shape[0] // grp

    def count_ge(t):
        for ng in range(1, max_groups + 1):
            @pl.when(n_groups == ng)
            def _(ng=ng):
                acc = None
                for g in range(ng):
                    ind = jnp.where(sc_ref[g * grp:(g + 1) * grp, :] >= t, 1.0, 0.0).astype(BF16)
                    part = jnp.dot(ones, ind, preferred_element_type=F32)
                    acc = part if acc is None else acc + part
                red_ref[...] = acc
        return red_ref[0:1, :].astype(I32)

    n_causal = q_pos + 1
    take_all = n_causal <= k_sel
    lo0 = _normal_key(_order_key(row_min))
    hi0 = _normal_key(_order_key(row_max)) + 1
    thr0 = jnp.full((1, tq), F32_LOWEST, F32)
    done0 = jnp.where(take_all, 1, 0)
    zero = jnp.zeros((1, tq), I32)

    def search_cond(st):
        it, _, _, _, _, done, _ = st
        return jnp.logical_and(it < SEARCH_PASS_CAP, jnp.min(done) == 0)

    def search_body(st):
        it, lo, hi, c_hi, thr, done, tied = st
        cross = jnp.logical_and(lo < 0, hi > 0)
        gap = hi - lo
        zero_probe = jnp.logical_and(it == 1, lo == 0)
        mid = _normal_key(jnp.where(cross, 0, jnp.where(zero_probe, KEY_MIN_NORMAL, lo + (gap >> 1))))
        v_mid = _normal_key(_order_key(0.5 * _key_float(lo) + 0.5 * _key_float(hi)))
        by_value = jnp.logical_and(jnp.logical_and(it < VALUE_PASSES, jnp.logical_not(jnp.logical_or(cross, zero_probe))),
                                   jnp.logical_and(v_mid > lo, v_mid < hi))
        mid = jnp.where(by_value, v_mid, mid)
        no_pivot = jnp.logical_or(gap == 1, jnp.logical_or(mid >= hi, mid <= lo))
        adjacent = jnp.logical_and(no_pivot, jnp.logical_not(cross))
        pivot = _key_float(mid)
        cnt = count_ge(pivot)
        live = done == 0
        hit = jnp.logical_and(live, jnp.logical_and(cnt == k_sel, jnp.logical_not(adjacent)))
        end = jnp.logical_and(live, adjacent)
        move = jnp.logical_and(live, jnp.logical_not(jnp.logical_or(hit, end)))
        up = jnp.logical_and(move, cnt > k_sel)
        dn = jnp.logical_and(move, cnt < k_sel)
        thr = jnp.where(hit, pivot, jnp.where(end, _key_float(lo), thr))
        tied = jnp.where(end, 1, tied)
        done = jnp.where(jnp.logical_or(hit, end), 1, done)
        lo = jnp.where(up, mid, lo)
        hi = jnp.where(dn, mid, hi)
        c_hi = jnp.where(dn, cnt, c_hi)
        return it + 1, lo, hi, c_hi, thr, done, tied

    st = lax.while_loop(search_cond, search_body, (jnp.int32(0), lo0, hi0, zero, thr0, done0, zero))
    _, _, _, c_hi, thr, _, tied = st

    room = jnp.where(tied > 0, k_sel - c_hi, jnp.int32(2 ** 30)).astype(F32)

    @pl.when(jnp.max(tied) > 0)
    def _():
        red_ref[0:1, :] = jnp.zeros((1, tq), F32)

        def drop_excess(c, carry):
            ks = chunk_start(c)
            s = sc_ref[pl.ds(ks, ck), :]
            is_tied = s == thr
            ind = jnp.where(is_tied, 1.0, 0.0).astype(BF16)
            pre = jnp.dot(tri_ref[...], ind, preferred_element_type=F32)
            before = red_ref[0:1, :]
            rank = pre[0:ck, :] + before
            sc_ref[pl.ds(ks, ck), :] = jnp.where(is_tied, jnp.where(rank >= room, -jnp.inf, s), s)
            red_ref[0:1, :] = before + pre[ck:ck + 1, :]
            return carry
        lax.fori_loop(0, n_chunks, drop_excess, 0)

    m_ref[...] = jnp.full(m_ref.shape, M_INIT, F32)
    acc_ref[...] = jnp.zeros(acc_ref.shape, F32)
    heads = [slice(h * A_HEAD_DIM, (h + 1) * A_HEAD_DIM) for h in range(A_HEADS)]
    ones_rows = jnp.ones((TRI_PAD, ck), BF16)

    def att_body(c, carry):
        ks = chunk_start(c)
        sel = sc_ref[pl.ds(ks, ck), :] >= thr
        logits = [jnp.dot(ka_ref[pl.ds(ks, ck), hs], qT_ref[hs, :], preferred_element_type=F32) for hs in heads]
        for h, hs in enumerate(heads):
            m_old = m_ref[h, 0:1, :]
            m_new = jnp.maximum(m_old, jnp.max(jnp.where(sel, logits[h], NEG_BIG), axis=0, keepdims=True))
            alpha = jnp.exp2(m_old - m_new)
            p = jnp.where(sel, jnp.exp2(logits[h] - m_new), 0.0).astype(BF16)
            m_ref[h, 0:1, :] = m_new
            lhs = jnp.concatenate([vT_ref[c, hs, :], ones_rows], axis=0)
            acc_ref[h] = acc_ref[h] * alpha + jnp.dot(lhs, p, preferred_element_type=F32)
        return carry

    lax.fori_loop(0, n_chunks, att_body, 0)
    for h, hs in enumerate(heads):
        denom = acc_ref[h, A_HEAD_DIM:A_HEAD_DIM + 1, :]
        o_ref[:, hs] = (acc_ref[h, 0:A_HEAD_DIM, :] / denom).T.astype(o_ref.dtype)


def _dsa_attention(qT, qiT, wT, ki, ka, vT, bsz, seq, tq):
    ck = min(CHUNK, seq)
    nck = seq // ck
    grp = ck
    k_sel = min(TOPK_MAX, seq // 4)
    kern = functools.partial(_dsa_kernel, tq=tq, ck=ck, grp=grp, k_sel=k_sel)
    once = pl.Buffered(1)
    out = pl.pallas_call(
        kern,
        grid=(bsz, seq // tq),
        in_specs=[
            pl.BlockSpec((None, A_WIDTH, tq), lambda b, i: (b, 0, i)),
            pl.BlockSpec((None, IDX_HEADS * LANES, tq), lambda b, i: (b, 0, i)),
            pl.BlockSpec((None, SUBLANES, tq), lambda b, i: (b, 0, i)),
            pl.BlockSpec((None, seq, LANES), lambda b, i: (b, 0, 0), pipeline_mode=once),
            pl.BlockSpec((None, seq, A_WIDTH), lambda b, i: (b, 0, 0), pipeline_mode=once),
            pl.BlockSpec((None, nck, A_WIDTH, ck), lambda b, i: (b, 0, 0, 0), pipeline_mode=once),
            pl.BlockSpec((ck + TRI_PAD, ck), lambda b, i: (0, 0), pipeline_mode=once),
        ],
        out_specs=pl.BlockSpec((None, tq, A_WIDTH), lambda b, i: (b, i, 0)),
        out_shape=jax.ShapeDtypeStruct((bsz, seq, A_WIDTH), BF16),
        scratch_shapes=[pltpu.VMEM((seq, tq), F32), pltpu.VMEM((SUBLANES, tq), F32),
                        pltpu.VMEM((A_HEADS, SUBLANES, tq), F32),
                        pltpu.VMEM((A_HEADS, A_HEAD_DIM + TRI_PAD, tq), F32)],
        compiler_params=_cparams(("parallel", "arbitrary")),
        name="dsa_attention",
    )(qT, qiT, wT, ki, ka, vT, _prefix_matrix(ck))
    return out.reshape(bsz * seq, A_WIDTH)


def _prefix_matrix(n):
    r = jnp.arange(n + TRI_PAD)[:, None]
    c = jnp.arange(n)[None, :]
    return jnp.logical_or(c < r, r >= n).astype(BF16)


SB_CK = 256
SB_TQ = 1024
SB_STEP = 4


def _sb_kernel(q_ref, kT_ref, v_ref, u_ref, o_ref, acc_ref, carry_ref, *, tq, ck):
    i = pl.program_id(2)
    q = q_ref[...]
    acc_ref[...] = jnp.zeros(acc_ref.shape, F32)
    carry_ref[...] = jnp.zeros(carry_ref.shape, F32)
    per = tq // ck

    def chunk(c, r0=None):
        diag = r0 is not None
        r0 = r0 or 0
        rows = tq - r0
        z = jnp.dot(q[r0:, :], kT_ref[c], preferred_element_type=F32)
        neg_abs = pltpu.bitcast(pltpu.bitcast(z, I32) | jnp.int32(-2 ** 31), F32)
        neg_soft = jnp.log(1.0 + jnp.exp2(neg_abs)) * (-LOG2E)
        lk = neg_soft - jnp.maximum(z, 0.0)
        if diag:
            q_pos = i * tq + r0 + lax.broadcasted_iota(I32, (rows, ck), 0)
            mask = (c * ck + lax.broadcasted_iota(I32, (rows, ck), 1)) < q_pos
            lk = jnp.where(mask, lk, 0.0)
        cum = jnp.dot(lk.astype(BF16), u_ref[...], preferred_element_type=F32)
        a = jnp.exp2(z + cum)
        if diag:
            a = jnp.where(mask, a, 0.0)
        vs = pl.multiple_of(c * ck, ck)
        pv = jnp.dot(a.astype(BF16), v_ref[pl.ds(vs, ck), :], preferred_element_type=F32)
        return pv, cum[:, 0:1], r0

    def fold(parts):
        for pv, total, r0 in parts:
            carry = carry_ref[r0:, :]
            acc_ref[r0:, :] += jnp.exp2(carry) * pv
            carry_ref[r0:, :] = carry + jnp.broadcast_to(total, carry.shape)

    step = min(SB_STEP, per)
    for d0 in reversed(range(0, per, step)):
        fold([chunk(i * per + d0 + d, (d0 + d) * ck) for d in reversed(range(step))])

    def body(s, carry):
        first = i * per - 1 - s * step
        fold([chunk(first - d) for d in range(step)])
        return carry

    lax.fori_loop(0, i * (per // step), body, 0)
    o_ref[...] = acc_ref[...].astype(o_ref.dtype)


def _cumsum_matrix(n):
    j = jnp.arange(n)[:, None]
    s = jnp.arange(n)[None, :]
    return (j >= s).astype(BF16)


def _sb_attention(sq, skT, sv, bsz, seq):
    t = bsz * seq
    tq = min(SB_TQ, seq)
    ck = min(SB_CK, seq)
    nq = seq // tq
    return pl.pallas_call(
        functools.partial(_sb_kernel, tq=tq, ck=ck),
        grid=(bsz, SB_HEADS, nq),
        in_specs=[
            pl.BlockSpec((tq, SB_HEAD_DIM), lambda b, h, i: (b * nq + i, h)),
            pl.BlockSpec((None, None, seq // ck, SB_HEAD_DIM, ck), lambda b, h, i: (b, h, 0, 0, 0)),
            pl.BlockSpec((None, seq, SB_HEAD_DIM), lambda b, h, i: (b, 0, h)),
            pl.BlockSpec((ck, ck), lambda b, h, i: (0, 0)),
        ],
        out_specs=pl.BlockSpec((tq, SB_HEAD_DIM), lambda b, h, i: (b * nq + i, h)),
        out_shape=jax.ShapeDtypeStruct((t, SB_WIDTH), BF16),
        scratch_shapes=[pltpu.VMEM((tq, SB_HEAD_DIM), F32), pltpu.VMEM((tq, LANES), F32)],
        compiler_params=_cparams(("parallel", "parallel", "arbitrary")),
        name="stickbreak",
    )(sq, skT, sv, _cumsum_matrix(ck))


CB_HALO = 32
CC_HALO = 16
CONV_ROWS = 64


def _conv_kernel(ga_ref, gb_ref, hga_ref, hgb_ref, cb_ref, cc_ref, cx_ref, hcc_ref, hcx_ref,
                 wb_ref, bb_ref, lng_ref, lnb_ref, wc_ref, ob_ref, oc_ref, buf_ref, u_ref, buf2_ref, *, ts, seq):
    i = pl.program_id(0)
    keep = jnp.where((i * ts) % seq == 0, 0.0, 1.0)

    f32 = lambda ref: ref[...].astype(F32)
    buf_ref[0:CB_HALO, :] = f32(hga_ref) * jax.nn.sigmoid(f32(hgb_ref)) * keep
    buf_ref[CB_HALO:, :] = f32(ga_ref) * jax.nn.sigmoid(f32(gb_ref))
    off0 = CB_HALO - (CB_CONV - 1)
    for r in range(ts // CONV_ROWS):
        t0 = r * CONV_ROWS
        for g in range(CB_WIDTH // LANES):
            ls = slice(g * LANES, (g + 1) * LANES)
            acc = jnp.broadcast_to(bb_ref[:, ls], (CONV_ROWS, LANES))
            for s in range(SUBLANES):
                rows = CONV_ROWS + (SUBLANES if s else 0)
                part = None
                for k in range(CB_CONV):
                    if (off0 + k) % SUBLANES != s:
                        continue
                    base = t0 + (off0 + k) - s
                    term = wb_ref[k:k + 1, ls] * buf_ref[base:base + rows, ls]
                    part = term if part is None else part + term
                acc = acc + part[s:s + CONV_ROWS, :]
            u_ref[t0:t0 + CONV_ROWS, ls] = acc
    u = u_ref[...]
    mu = jnp.mean(u, axis=-1, keepdims=True)
    uc = u - mu
    y = uc * lax.rsqrt(jnp.mean(uc * uc, axis=-1, keepdims=True) + NORM_EPS) * lng_ref[...] + lnb_ref[...]
    ob_ref[...] = (y * jax.nn.sigmoid(y)).astype(ob_ref.dtype)

    buf2_ref[0:CC_HALO, :] = f32(hcc_ref) * f32(hcx_ref) * keep
    buf2_ref[CC_HALO:, :] = f32(cc_ref) * f32(cx_ref)
    conv = jnp.zeros((ts, CC_WIDTH), F32)
    for k in range(CC_CONV):
        base = CC_HALO - (CC_CONV - 1) + k
        conv = conv + wc_ref[k:k + 1, :] * buf2_ref[base:base + ts, :]
    oc_ref[...] = (f32(cb_ref) * conv).astype(oc_ref.dtype)


def _conv_mixers(proj, wb, bb, lng, lnb, wc, seq, ts):
    t = proj.shape[0]
    w = CB_WIDTH
    c0 = (OFF_GLU - SPLIT_COL) // w
    c1 = (OFF_CC - SPLIT_COL) // w
    cur = lambda col: pl.BlockSpec((ts, w), lambda i: (i, col))
    halo = lambda rows, col: pl.BlockSpec((rows, w), lambda i: (jnp.maximum(i * (ts // rows) - 1, 0), col))
    full = lambda r: pl.BlockSpec((r, w), lambda i: (0, 0))
    wb_p = jnp.pad(wb, ((0, CB_HALO - CB_CONV), (0, 0)))
    wc_p = jnp.pad(wc, ((0, SUBLANES - CC_CONV), (0, 0)))
    return pl.pallas_call(
        functools.partial(_conv_kernel, ts=ts, seq=seq),
        grid=(t // ts,),
        in_specs=[cur(c0), cur(c0 + 1), halo(CB_HALO, c0), halo(CB_HALO, c0 + 1),
                  cur(c1), cur(c1 + 1), cur(c1 + 2), halo(CC_HALO, c1 + 1), halo(CC_HALO, c1 + 2),
                  full(CB_HALO), full(1), full(1), full(1), full(SUBLANES)],
        out_specs=[pl.BlockSpec((ts, w), lambda i: (i, 0))] * 2,
        out_shape=[jax.ShapeDtypeStruct((t, w), BF16)] * 2,
        scratch_shapes=[pltpu.VMEM((CB_HALO + ts, w), F32), pltpu.VMEM((ts, w), F32),
                        pltpu.VMEM((CC_HALO + ts, w), F32)],
        compiler_params=_cparams(("parallel",)),
        name="conv_mixers",
    )(proj, proj, proj, proj, proj, proj, proj, proj, proj,
      wb_p, bb.reshape(1, w), lng.reshape(1, w), lnb.reshape(1, w), wc_p)


def _merge_kernel(x_ref, oa_ref, ob_ref, oc_ref, od_ref, g0_ref, g1_ref, g2_ref, g3_ref, bg_ref,
                  wa_ref, wb_ref, wc_ref, wd_ref, wm_ref, o_ref):
    merged = None
    branches = ((oa_ref, wa_ref, g0_ref), (ob_ref, wb_ref, g1_ref), (oc_ref, wc_ref, g2_ref), (od_ref, wd_ref, g3_ref))
    for n, (b_ref, w_ref, g_ref) in enumerate(branches):
        y = jnp.dot(b_ref[...], w_ref[...], preferred_element_type=F32)
        term = jax.nn.sigmoid(g_ref[...].astype(F32) + bg_ref[n:n + 1, :]) * y
        merged = term if merged is None else merged + term
    o_ref[...] = x_ref[...] + jnp.dot(merged.astype(BF16), wm_ref[...], preferred_element_type=F32)


def _merge(x2d, oa, ob, oc, od, proj, b_gate, w_oa, w_ob, w_oc, w_od, w_merge, tm):
    t, d = x2d.shape
    g0 = (OFF_GATE - SPLIT_COL) // d
    row = lambda i: (i, 0)
    fixed = lambda i: (0, 0)
    br = pl.BlockSpec((tm, A_WIDTH), row)
    gate = lambda n: pl.BlockSpec((tm, d), lambda i: (i, g0 + n))
    wsp = pl.BlockSpec((A_WIDTH, d), fixed)
    bf = lambda w: w.astype(BF16)
    return pl.pallas_call(
        _merge_kernel,
        grid=(t // tm,),
        in_specs=[pl.BlockSpec((tm, d), row), br, br, br, br, gate(0), gate(1), gate(2), gate(3),
                  pl.BlockSpec((N_BRANCH, d), fixed), wsp, wsp, wsp, wsp, pl.BlockSpec((d, d), fixed)],
        out_specs=pl.BlockSpec((tm, d), row),
        out_shape=jax.ShapeDtypeStruct((t, d), F32),
        compiler_params=_cparams(("parallel",)),
        name="merge",
    )(x2d, oa, ob, oc, od, proj, proj, proj, proj, b_gate.reshape(N_BRANCH, d),
      bf(w_oa), bf(w_ob), bf(w_oc), bf(w_od), bf(w_merge))


FFN_HALO = 8


def _ffn_kernel(x_ref, hx_ref, g_ref, wg_ref, wu_ref, cw_ref, wd_ref, o_ref, h_ref, gp_ref, acc_ref, *, tm, seq):
    i = pl.program_id(0)
    f = pl.program_id(1)

    def norm(x):
        ms = jnp.mean(x * x, axis=-1, keepdims=True)
        return (x * lax.rsqrt(ms + NORM_EPS) * g_ref[...]).astype(BF16)

    @pl.when(f == 0)
    def _():
        h_ref[0:FFN_HALO, :] = norm(hx_ref[...])
        h_ref[FFN_HALO:, :] = norm(x_ref[...])
        acc_ref[...] = x_ref[...]

    keep = jnp.where((i * tm) % seq == 0, 0.0, 1.0)
    gp_ref[0:FFN_HALO, :] = jnp.dot(h_ref[0:FFN_HALO, :], wg_ref[...], preferred_element_type=F32) * keep
    gp_ref[FFN_HALO:, :] = jnp.dot(h_ref[FFN_HALO:, :], wg_ref[...], preferred_element_type=F32)
    gt = jnp.zeros((tm, gp_ref.shape[1]), F32)
    for k in range(FFN_CONV):
        base = FFN_HALO - (FFN_CONV - 1) + k
        gt = gt + cw_ref[k:k + 1, :] * gp_ref[base:base + tm, :]
    up = jnp.dot(h_ref[FFN_HALO:, :], wu_ref[...], preferred_element_type=F32)
    act = (gt * jax.nn.sigmoid(gt) * up).astype(BF16)
    acc_ref[...] += jnp.dot(act, wd_ref[...], preferred_element_type=F32)

    @pl.when(f == pl.num_programs(1) - 1)
    def _():
        o_ref[...] = acc_ref[...]


def _ffn(x2d, g, wg, wu, cw, wd, seq, tm, tf):
    t, d = x2d.shape
    ff = wg.shape[1]
    cw_p = jnp.pad(cw, ((0, SUBLANES - FFN_CONV), (0, 0)))
    return pl.pallas_call(
        functools.partial(_ffn_kernel, tm=tm, seq=seq),
        grid=(t // tm, ff // tf),
        in_specs=[
            pl.BlockSpec((tm, d), lambda i, f: (i, 0)),
            pl.BlockSpec((FFN_HALO, d), lambda i, f: (jnp.maximum(i * (tm // FFN_HALO) - 1, 0), 0)),
            pl.BlockSpec((1, d), lambda i, f: (0, 0)),
            pl.BlockSpec((d, tf), lambda i, f: (0, f)),
            pl.BlockSpec((d, tf), lambda i, f: (0, f)),
            pl.BlockSpec((SUBLANES, tf), lambda i, f: (0, f)),
            pl.BlockSpec((tf, d), lambda i, f: (f, 0)),
        ],
        out_specs=pl.BlockSpec((tm, d), lambda i, f: (i, 0)),
        out_shape=jax.ShapeDtypeStruct((t, d), F32),
        scratch_shapes=[pltpu.VMEM((FFN_HALO + tm, d), BF16), pltpu.VMEM((FFN_HALO + tm, tf), F32),
                        pltpu.VMEM((tm, d), F32)],
        compiler_params=_cparams(("parallel", "arbitrary")),
        name="ffn",
    )(x2d, x2d, g.reshape(1, d), wg.astype(BF16), wu.astype(BF16), cw_p, wd.astype(BF16))


def _tiles(seq):
    big = min(seq, 512)
    return dict(
        inproj_tm=min(seq, 1024), inproj_tn=1024,
        conv_ts=big, merge_tm=big, ffn_tm=big, ffn_tf=D_FF // 2,
        dsa_tq=min(seq, 512),
    )


def kernel(x, positions, g_mix, w_in, b_gate, g_qa, g_ka, g_kidx, w_oa, cb_conv_w, cb_conv_b, cb_ln_g, cb_ln_b,
           w_ob, cc_conv_w, w_oc, w_od, w_merge, g_ffn, w_ffn_gate, w_ffn_up, ffn_conv_w, w_ffn_down):
    bsz, seq, dm = x.shape
    depth = w_in.shape[0]
    assert dm == D_MODEL and seq % min(CHUNK, seq) == 0 and seq % LANES == 0
    tl = _tiles(seq)
    t = bsz * seq

    cos_a, sin_a = _rope_tables(positions, A_HEAD_DIM)
    cos_i, sin_i = _rope_tables(positions, IDX_DIM)
    ca, sa = _rope_lane_tables(cos_a, sin_a, A_HEAD_DIM)
    ci, si = _rope_lane_tables(cos_i, sin_i, IDX_DIM)

    x2d = x.reshape(t, dm)
    for l in range(depth):
        w_packed = _pack_w_in(jnp.swapaxes(w_in[l], 0, 1))
        tm, tn = tl["inproj_tm"], tl["inproj_tn"]
        proj_a = _inproj(x2d, g_mix[l], w_packed, tm, tn, 0, SPLIT_COL, F32)
        proj = _inproj(x2d, g_mix[l], w_packed, tm, tn, SPLIT_COL, N_PACK - SPLIT_COL, BF16)
        qT, qiT, wT, ki, ka, vT, sq, skT, sv = _prep(proj_a, proj, ca, sa, ci, si, g_qa[l], g_ka[l], g_kidx[l],
                                                     bsz, seq)
        o_a = _dsa_attention(qT, qiT, wT, ki, ka, vT, bsz, seq, tl["dsa_tq"])
        o_d = _sb_attention(sq, skT, sv, bsz, seq)
        o_b, o_c = _conv_mixers(proj, cb_conv_w[l], cb_conv_b[l], cb_ln_g[l], cb_ln_b[l], cc_conv_w[l],
                                seq, tl["conv_ts"])
        x2d = _merge(x2d, o_a, o_b, o_c, o_d, proj, b_gate[l], w_oa[l], w_ob[l], w_oc[l], w_od[l], w_merge[l],
                     tl["merge_tm"])
        x2d = _ffn(x2d, g_ffn[l], w_ffn_gate[l], w_ffn_up[l], ffn_conv_w[l], w_ffn_down[l], seq,
                   tl["ffn_tm"], tl["ffn_tf"])
    return x2d.reshape(bsz, seq, dm)
```

```python
import functools
import math

import jax
import jax.numpy as jnp
from jax import lax
from jax.experimental import pallas as pl
from jax.experimental.pallas import tpu as pltpu

F32 = jnp.float32
BF16 = jnp.bfloat16
I32 = jnp.int32

D_MODEL = 1024
N_BRANCH = 4
A_HEADS = 4
A_HEAD_DIM = 128
A_WIDTH = A_HEADS * A_HEAD_DIM
IDX_HEADS = 4
IDX_DIM = 64
TOPK_MAX = 256
CB_WIDTH = 512
CB_CONV = 31
CC_WIDTH = 512
CC_CONV = 3
SB_HEADS = 4
SB_HEAD_DIM = 128
SB_WIDTH = SB_HEADS * SB_HEAD_DIM
D_FF = 2816
FFN_CONV = 3
ROPE_THETA = 500000.0
ROT_FRACTION_DIV = 4
NORM_EPS = 1e-6

LANES = 128
SUBLANES = 8
VMEM_LIMIT = 56 * 1024 * 1024

LOG2E = math.log2(math.e)
NEG_BIG = -1e30
M_INIT = -1e20
F32_LOWEST = float(jnp.finfo(jnp.float32).min)
KEY_MIN_NORMAL = 0x00800000
TRI_PAD = 16
COUNT_VPU_SHARE = 4
VALUE_PASSES = 12
SEARCH_PASS_CAP = VALUE_PASSES + 36

OFF_AQ = 0
OFF_AK = 512
OFF_AV = 1024
OFF_IQ = 1536
OFF_IK = 1792
OFF_IW = 1920
OFF_GLU = 2048
OFF_CC = 3072
OFF_SB = 4608
OFF_GATE = 6144
N_PACK = 10240
SPLIT_COL = 2048
CHUNK = 512


def _cparams(sem):
    return pltpu.CompilerParams(dimension_semantics=sem, vmem_limit_bytes=VMEM_LIMIT)


def _trig_kernel(pos_ref, invf_ref, cos_ref, sin_ref):
    ang = pos_ref[...] * invf_ref[...]
    cos_ref[...] = jnp.cos(ang)
    sin_ref[...] = jnp.sin(ang)


def _rope_tables(positions, head_dim):
    rot = head_dim // ROT_FRACTION_DIV
    half = rot // 2
    inv_freq = ROPE_THETA ** (-(jnp.arange(half, dtype=F32) * 2.0) / rot)
    t = positions.size
    rows = t * half // LANES
    pos_rep = jnp.broadcast_to(positions.astype(F32).reshape(t, 1), (t, half)).reshape(rows, LANES)
    invf = jnp.tile(inv_freq, LANES // half).reshape(1, LANES)
    tr = min(rows, 512)
    cos, sin = pl.pallas_call(
        _trig_kernel,
        grid=(rows // tr,),
        in_specs=[pl.BlockSpec((tr, LANES), lambda i: (i, 0)), pl.BlockSpec((1, LANES), lambda i: (0, 0))],
        out_specs=[pl.BlockSpec((tr, LANES), lambda i: (i, 0))] * 2,
        out_shape=[jax.ShapeDtypeStruct((rows, LANES), F32)] * 2,
        compiler_params=_cparams(("parallel",)),
        name="rope_trig",
    )(pos_rep, invf)
    return cos.reshape(t, half), sin.reshape(t, half)


def _rope_lane_tables(cos, sin, head_dim):
    t, half = cos.shape
    rest = head_dim - 2 * half
    c = jnp.concatenate([cos, cos, jnp.ones((t, rest), F32)], axis=1)
    s = jnp.concatenate([sin, sin, jnp.zeros((t, rest), F32)], axis=1)
    reps = LANES // head_dim
    return jnp.tile(c, (1, reps)), jnp.tile(s, (1, reps))


def _inproj_kernel(x_ref, g_ref, w_ref, o_ref, h_ref):
    @pl.when(pl.program_id(1) == 0)
    def _():
        x = x_ref[...]
        ms = jnp.mean(x * x, axis=-1, keepdims=True)
        h_ref[...] = (x * lax.rsqrt(ms + NORM_EPS) * g_ref[...]).astype(BF16)

    o_ref[...] = jnp.dot(h_ref[...], w_ref[...], preferred_element_type=F32).astype(o_ref.dtype)


def _inproj(x2d, g, w_packed, tm, tn, col0, ncols, out_dtype):
    t, d = x2d.shape
    j0 = col0 // tn
    return pl.pallas_call(
        _inproj_kernel,
        grid=(t // tm, ncols // tn),
        in_specs=[
            pl.BlockSpec((tm, d), lambda i, j: (i, 0)),
            pl.BlockSpec((1, d), lambda i, j: (0, 0)),
            pl.BlockSpec((d, tn), lambda i, j: (0, j + j0)),
        ],
        out_specs=pl.BlockSpec((tm, tn), lambda i, j: (i, j)),
        out_shape=jax.ShapeDtypeStruct((t, ncols), out_dtype),
        scratch_shapes=[pltpu.VMEM((tm, d), BF16)],
        compiler_params=_cparams(("parallel", "arbitrary")),
        name="inproj",
    )(x2d, g.reshape(1, d), w_packed)


N_IN = 10052
OFF_IW_SRC = 1856
OFF_REST_SRC = 1860
PACK_FIRST = OFF_REST_SRC // LANES
PACK_ROT = (OFF_GLU - OFF_REST_SRC) % LANES


def _pack_kernel(a_ref, b_ref, o_ref):
    n = pl.program_id(0)
    row = lax.broadcasted_iota(I32, a_ref.shape, 0)
    a = a_ref[...]

    @pl.when(n < OFF_IK // LANES)
    def _():
        o_ref[...] = a.T.astype(BF16)

    @pl.when(n == OFF_IK // LANES)
    def _():
        o_ref[...] = jnp.where(row < IDX_DIM, a, 0.0).T.astype(BF16)

    @pl.when(n == OFF_IW // LANES)
    def _():
        iw = pltpu.roll(a, LANES - (OFF_IW_SRC - OFF_IK), 0)
        o_ref[...] = jnp.where(row < IDX_HEADS, iw, 0.0).T.astype(BF16)

    @pl.when(n >= OFF_GLU // LANES)
    def _():
        out = jnp.where(row < PACK_ROT, pltpu.roll(a, PACK_ROT, 0), pltpu.roll(b_ref[...], PACK_ROT, 0))
        o_ref[...] = out.T.astype(BF16)


def _pack_w_in(w_t):
    d = w_t.shape[1]
    first_rest = OFF_GLU // LANES

    def a_idx(n):
        return (jnp.where(n < first_rest, jnp.minimum(n, PACK_FIRST), n - first_rest + PACK_FIRST), 0)

    def b_idx(n):
        return (jnp.where(n < first_rest, PACK_FIRST, n - first_rest + PACK_FIRST + 1), 0)

    return pl.pallas_call(
        _pack_kernel,
        grid=(N_PACK // LANES,),
        in_specs=[pl.BlockSpec((LANES, d), a_idx), pl.BlockSpec((LANES, d), b_idx)],
        out_specs=pl.BlockSpec((d, LANES), lambda n: (0, n)),
        out_shape=jax.ShapeDtypeStruct((d, N_PACK), BF16),
        compiler_params=_cparams(("parallel",)),
        name="pack_w_in",
    )(w_t, w_t)


def _rotate_half(x, half, period):
    lane = lax.broadcasted_iota(I32, x.shape, 1)
    up = pltpu.roll(x, LANES - half, 1)
    dn = pltpu.roll(x, half, 1)
    return jnp.where((lane % period) < half, -up, dn)


def _prep_kernel(p_ref, sqi_ref, ski_ref, svi_ref, ca_ref, sa_ref, ci_ref, si_ref, gq_ref, gk_ref, gi_ref,
                 qT_ref, qiT_ref, wT_ref, ki_ref, ka_ref, vT_ref, sq_ref, skT_ref, sv_ref):
    ca, sa, ci, si = ca_ref[...], sa_ref[...], ci_ref[...], si_ref[...]
    a_half = A_HEAD_DIM // ROT_FRACTION_DIV // 2
    i_half = IDX_DIM // ROT_FRACTION_DIV // 2

    def head_norm(x, g, n_real):
        ms = jnp.sum(x * x, axis=-1, keepdims=True) * (1.0 / n_real)
        return x * lax.rsqrt(ms + NORM_EPS) * g

    for h in range(A_HEADS):
        sl = slice(h * A_HEAD_DIM, (h + 1) * A_HEAD_DIM)
        q = head_norm(p_ref[:, OFF_AQ + h * 128:OFF_AQ + (h + 1) * 128], gq_ref[...], A_HEAD_DIM)
        q = q * ca + _rotate_half(q, a_half, A_HEAD_DIM) * sa
        qT_ref[sl, :] = (q * (A_HEAD_DIM ** -0.5 * LOG2E)).T.astype(BF16)
        k = head_norm(p_ref[:, OFF_AK + h * 128:OFF_AK + (h + 1) * 128], gk_ref[...], A_HEAD_DIM)
        ka_ref[:, sl] = (k * ca + _rotate_half(k, a_half, A_HEAD_DIM) * sa).astype(BF16)
        vT_ref[sl, :] = p_ref[:, OFF_AV + h * 128:OFF_AV + (h + 1) * 128].T.astype(BF16)

    row = lax.broadcasted_iota(I32, (LANES, p_ref.shape[0]), 0)
    for j in range(IDX_HEADS * IDX_DIM // LANES):
        qi = p_ref[:, OFF_IQ + j * LANES:OFF_IQ + (j + 1) * LANES]
        qt = (qi * ci + _rotate_half(qi, i_half, IDX_DIM) * si).T
        qiT_ref[(2 * j) * LANES:(2 * j + 1) * LANES, :] = jnp.where(row < IDX_DIM, qt, 0.0).astype(BF16)
        qiT_ref[(2 * j + 1) * LANES:(2 * j + 2) * LANES, :] = jnp.where(row >= IDX_DIM, qt, 0.0).astype(BF16)
    ki = head_norm(p_ref[:, OFF_IK:OFF_IK + LANES], gi_ref[...], IDX_DIM)
    ki = ki * ci + _rotate_half(ki, i_half, IDX_DIM) * si
    ki_ref[...] = (ki + pltpu.roll(ki, IDX_DIM, 1)).astype(BF16)
    w = p_ref[:, OFF_IW:OFF_IW + LANES] * (IDX_DIM ** -0.5 * IDX_HEADS ** -0.5)
    wT_ref[...] = w.T[0:SUBLANES, :]

    sq_ref[...] = (sqi_ref[...].astype(F32) * (SB_HEAD_DIM ** -0.5 * LOG2E)).astype(BF16)
    for h in range(SB_HEADS):
        k_t = ski_ref[:, h * 128:(h + 1) * 128].astype(F32).T.astype(BF16)
        sb_ck = skT_ref.shape[-1]
        for c in range(skT_ref.shape[1]):
            skT_ref[h, c] = k_t[:, c * sb_ck:(c + 1) * sb_ck]
    sv_ref[...] = svi_ref[...]


def _prep(proj_a, proj_b, ca, sa, ci, si, g_qa, g_ka, g_kidx, bsz, seq):
    t = proj_a.shape[0]
    ck = min(CHUNK, seq)
    sb_ck = min(SB_CK, seq)
    nck = seq // ck
    gi = jnp.concatenate([g_kidx, jnp.ones((LANES - IDX_DIM,), F32)]).reshape(1, LANES)
    tok = lambda b, i: (b * nck + i, 0)
    fixed = lambda b, i: (0, 0)
    tbl = pl.BlockSpec((ck, LANES), tok)
    gsp = pl.BlockSpec((1, LANES), fixed)
    sb_col = (OFF_SB - SPLIT_COL) // SB_WIDTH
    sb_in = lambda n: pl.BlockSpec((ck, SB_WIDTH), lambda b, i: (b * nck + i, sb_col + n))
    return pl.pallas_call(
        _prep_kernel,
        grid=(bsz, nck),
        in_specs=[pl.BlockSpec((ck, SPLIT_COL), tok), sb_in(0), sb_in(1), sb_in(2),
                  tbl, tbl, tbl, tbl, gsp, gsp, gsp],
        out_specs=[
            pl.BlockSpec((None, A_WIDTH, ck), lambda b, i: (b, 0, i)),
            pl.BlockSpec((None, IDX_HEADS * LANES, ck), lambda b, i: (b, 0, i)),
            pl.BlockSpec((None, SUBLANES, ck), lambda b, i: (b, 0, i)),
            pl.BlockSpec((None, ck, LANES), lambda b, i: (b, i, 0)),
            pl.BlockSpec((None, ck, A_WIDTH), lambda b, i: (b, i, 0)),
            pl.BlockSpec((None, None, A_WIDTH, ck), lambda b, i: (b, i, 0, 0)),
            pl.BlockSpec((ck, SB_WIDTH), tok),
            pl.BlockSpec((None, SB_HEADS, ck // sb_ck, SB_HEAD_DIM, sb_ck), lambda b, i: (b, 0, i, 0, 0)),
            pl.BlockSpec((None, ck, SB_WIDTH), lambda b, i: (b, i, 0)),
        ],
        out_shape=[
            jax.ShapeDtypeStruct((bsz, A_WIDTH, seq), BF16),
            jax.ShapeDtypeStruct((bsz, IDX_HEADS * LANES, seq), BF16),
            jax.ShapeDtypeStruct((bsz, SUBLANES, seq), F32),
            jax.ShapeDtypeStruct((bsz, seq, LANES), BF16),
            jax.ShapeDtypeStruct((bsz, seq, A_WIDTH), BF16),
            jax.ShapeDtypeStruct((bsz, nck, A_WIDTH, ck), BF16),
            jax.ShapeDtypeStruct((t, SB_WIDTH), BF16),
            jax.ShapeDtypeStruct((bsz, SB_HEADS, seq // sb_ck, SB_HEAD_DIM, sb_ck), BF16),
            jax.ShapeDtypeStruct((bsz, seq, SB_WIDTH), BF16),
        ],
        compiler_params=_cparams(("parallel", "parallel")),
        name="mixer_prep",
    )(proj_a, proj_b, proj_b, proj_b, ca, sa, ci, si, g_qa.reshape(1, LANES), g_ka.reshape(1, LANES), gi)


def _order_key(f):
    b = pltpu.bitcast(f, I32)
    return b ^ ((b >> 31) & 0x7FFFFFFF)


def _key_float(k):
    return pltpu.bitcast(k ^ ((k >> 31) & 0x7FFFFFFF), F32)


def _normal_key(k):
    k = jnp.where(jnp.logical_and(k > 0, k < KEY_MIN_NORMAL), KEY_MIN_NORMAL, k)
    return jnp.where(jnp.logical_and(k < 0, k >= -KEY_MIN_NORMAL), 0, k)


def _dsa_kernel(qT_ref, qiT_ref, wT_ref, ki_ref, ka_ref, vT_ref, tri_ref, o_ref, sc_ref, red_ref, cnt_ref, m_ref,
                acc_ref, *, tq, ck, k_sel):
    i = pl.program_id(1)
    n_keys = (i + 1) * tq
    n_chunks = (n_keys + ck - 1) // ck
    q_pos = i * tq + lax.broadcasted_iota(I32, (1, tq), 1)

    def chunk_start(c):
        return pl.multiple_of(c * ck, ck)

    def key_pos(c):
        return c * ck + lax.broadcasted_iota(I32, (ck, tq), 0)

    def score_chunk(c, diag):
        ks = chunk_start(c)
        kic = ki_ref[pl.ds(ks, ck), :]
        sc = None
        for h in range(IDX_HEADS):
            d = jnp.dot(kic, qiT_ref[h * LANES:(h + 1) * LANES, :], preferred_element_type=F32)
            term = jnp.maximum(d, 0.0) * wT_ref[h:h + 1, :]
            sc = term if sc is None else sc + term
        if diag:
            causal = key_pos(c) <= q_pos
            sc_hi = jnp.where(causal, sc, -jnp.inf)
            sc_lo = jnp.where(causal, sc, jnp.inf)
        else:
            sc_hi = sc_lo = sc
        sc_ref[pl.ds(ks, ck), :] = sc_hi
        fold = lambda a, op: op(a.reshape(ck // SUBLANES, SUBLANES, tq), axis=0)
        return fold(sc_hi, jnp.max), fold(sc_lo, jnp.min)

    def score_body(c, carry):
        mx, mn = score_chunk(c, False)
        return jnp.maximum(carry[0], mx), jnp.minimum(carry[1], mn)

    init = (jnp.full((SUBLANES, tq), -jnp.inf, F32), jnp.full((SUBLANES, tq), jnp.inf, F32))
    mx, mn = lax.fori_loop(0, n_chunks - 1, score_body, init)
    mx_d, mn_d = score_chunk(n_chunks - 1, True)
    def fold_rows(x, op):
        red_ref[...] = x
        r = red_ref[0:1, :]
        for j in range(1, SUBLANES):
            r = op(r, red_ref[j:j + 1, :])
        return r

    row_max = fold_rows(jnp.maximum(mx, mx_d), jnp.maximum)
    row_min = fold_rows(jnp.minimum(mn, mn_d), jnp.minimum)

    mxu_rows = ck - ck // COUNT_VPU_SHARE
    ones = jnp.ones((SUBLANES, mxu_rows), BF16)

    def count_ge(t):
        for nc in range(1, sc_ref.shape[0] // ck + 1):
            @pl.when(n_chunks == nc)
            def _(nc=nc):
                on_mxu = None
                on_vpu = jnp.zeros((SUBLANES, tq), F32)
                for c in range(nc):
                    ind = jnp.where(sc_ref[c * ck:c * ck + mxu_rows, :] >= t, 1.0, 0.0).astype(BF16)
                    part = jnp.dot(ones, ind, preferred_element_type=F32)
                    on_mxu = part if on_mxu is None else on_mxu + part
                    tail = jnp.where(sc_ref[c * ck + mxu_rows:(c + 1) * ck, :] >= t, 1.0, 0.0)
                    on_vpu = on_vpu + tail.reshape(-1, SUBLANES, tq).sum(axis=0)
                cnt_ref[0:SUBLANES, :] = on_vpu
                cnt_ref[SUBLANES:, :] = on_mxu
        total = cnt_ref[SUBLANES:SUBLANES + 1, :]
        for j in range(SUBLANES):
            total = total + cnt_ref[j:j + 1, :]
        return total.astype(I32)

    n_causal = q_pos + 1
    take_all = n_causal <= k_sel
    lo0 = _normal_key(_order_key(row_min))
    hi0 = _normal_key(_order_key(row_max)) + 1
    thr0 = jnp.full((1, tq), F32_LOWEST, F32)
    done0 = jnp.where(take_all, 1, 0)
    zero = jnp.zeros((1, tq), I32)

    def search_cond(st):
        it, _, _, _, _, done, _ = st
        return jnp.logical_and(it < SEARCH_PASS_CAP, jnp.min(done) == 0)

    def search_body(st):
        it, lo, hi, c_hi, thr, done, tied = st
        cross = jnp.logical_and(lo < 0, hi > 0)
        gap = hi - lo
        zero_probe = jnp.logical_and(it == 1, lo == 0)
        mid = _normal_key(jnp.where(cross, 0, jnp.where(zero_probe, KEY_MIN_NORMAL, lo + (gap >> 1))))
        v_mid = _normal_key(_order_key(0.5 * _key_float(lo) + 0.5 * _key_float(hi)))
        by_value = jnp.logical_and(jnp.logical_and(it < VALUE_PASSES, jnp.logical_not(jnp.logical_or(cross, zero_probe))),
                                   jnp.logical_and(v_mid > lo, v_mid < hi))
        mid = jnp.where(by_value, v_mid, mid)
        no_pivot = jnp.logical_or(gap == 1, jnp.logical_or(mid >= hi, mid <= lo))
        adjacent = jnp.logical_and(no_pivot, jnp.logical_not(cross))
        pivot = _key_float(mid)
        cnt = count_ge(pivot)
        live = done == 0
        hit = jnp.logical_and(live, jnp.logical_and(cnt == k_sel, jnp.logical_not(adjacent)))
        end = jnp.logical_and(live, adjacent)
        move = jnp.logical_and(live, jnp.logical_not(jnp.logical_or(hit, end)))
        up = jnp.logical_and(move, cnt > k_sel)
        dn = jnp.logical_and(move, cnt < k_sel)
        thr = jnp.where(hit, pivot, jnp.where(end, _key_float(lo), thr))
        tied = jnp.where(end, 1, tied)
        done = jnp.where(jnp.logical_or(hit, end), 1, done)
        lo = jnp.where(up, mid, lo)
        hi = jnp.where(dn, mid, hi)
        c_hi = jnp.where(dn, cnt, c_hi)
        return it + 1, lo, hi, c_hi, thr, done, tied

    st = lax.while_loop(search_cond, search_body, (jnp.int32(0), lo0, hi0, zero, thr0, done0, zero))
    _, _, _, c_hi, thr, _, tied = st

    room = jnp.where(tied > 0, k_sel - c_hi, jnp.int32(2 ** 30)).astype(F32)

    @pl.when(jnp.max(tied) > 0)
    def _():
        red_ref[0:1, :] = jnp.zeros((1, tq), F32)

        def drop_excess(c, carry):
            ks = chunk_start(c)
            s = sc_ref[pl.ds(ks, ck), :]
            is_tied = s == thr
            ind = jnp.where(is_tied, 1.0, 0.0).astype(BF16)
            pre = jnp.dot(tri_ref[...], ind, preferred_element_type=F32)
            before = red_ref[0:1, :]
            rank = pre[0:ck, :] + before
            sc_ref[pl.ds(ks, ck), :] = jnp.where(is_tied, jnp.where(rank >= room, -jnp.inf, s), s)
            red_ref[0:1, :] = before + pre[ck:ck + 1, :]
            return carry
        lax.fori_loop(0, n_chunks, drop_excess, 0)

    m_ref[...] = jnp.full(m_ref.shape, M_INIT, F32)
    acc_ref[...] = jnp.zeros(acc_ref.shape, F32)
    heads = [slice(h * A_HEAD_DIM, (h + 1) * A_HEAD_DIM) for h in range(A_HEADS)]
    ones_rows = jnp.ones((TRI_PAD, ck), BF16)

    def att_body(c, carry):
        ks = chunk_start(c)
        sel = sc_ref[pl.ds(ks, ck), :] >= thr
        logits = [jnp.dot(ka_ref[pl.ds(ks, ck), hs], qT_ref[hs, :], preferred_element_type=F32) for hs in heads]
        for h, hs in enumerate(heads):
            m_old = m_ref[h, 0:1, :]
            m_new = jnp.maximum(m_old, jnp.max(jnp.where(sel, logits[h], NEG_BIG), axis=0, keepdims=True))
            alpha = jnp.exp2(m_old - m_new)
            p = jnp.where(sel, jnp.exp2(logits[h] - m_new), 0.0).astype(BF16)
            m_ref[h, 0:1, :] = m_new
            lhs = jnp.concatenate([vT_ref[c, hs, :], ones_rows], axis=0)
            acc_ref[h] = acc_ref[h] * alpha + jnp.dot(lhs, p, preferred_element_type=F32)
        return carry

    lax.fori_loop(0, n_chunks, att_body, 0)
    for h, hs in enumerate(heads):
        denom = acc_ref[h, A_HEAD_DIM:A_HEAD_DIM + 1, :]
        o_ref[:, hs] = (acc_ref[h, 0:A_HEAD_DIM, :] / denom).T.astype(o_ref.dtype)


def _dsa_attention(qT, qiT, wT, ki, ka, vT, bsz, seq, tq):
    ck = min(CHUNK, seq)
    nck = seq // ck
    k_sel = min(TOPK_MAX, seq // 4)
    kern = functools.partial(_dsa_kernel, tq=tq, ck=ck, k_sel=k_sel)
    once = pl.Buffered(1)
    out = pl.pallas_call(
        kern,
        grid=(bsz, seq // tq),
        in_specs=[
            pl.BlockSpec((None, A_WIDTH, tq), lambda b, i: (b, 0, i)),
            pl.BlockSpec((None, IDX_HEADS * LANES, tq), lambda b, i: (b, 0, i)),
            pl.BlockSpec((None, SUBLANES, tq), lambda b, i: (b, 0, i)),
            pl.BlockSpec((None, seq, LANES), lambda b, i: (b, 0, 0), pipeline_mode=once),
            pl.BlockSpec((None, seq, A_WIDTH), lambda b, i: (b, 0, 0), pipeline_mode=once),
            pl.BlockSpec((None, nck, A_WIDTH, ck), lambda b, i: (b, 0, 0, 0), pipeline_mode=once),
            pl.BlockSpec((ck + TRI_PAD, ck), lambda b, i: (0, 0), pipeline_mode=once),
        ],
        out_specs=pl.BlockSpec((None, tq, A_WIDTH), lambda b, i: (b, i, 0)),
        out_shape=jax.ShapeDtypeStruct((bsz, seq, A_WIDTH), BF16),
        scratch_shapes=[pltpu.VMEM((seq, tq), F32), pltpu.VMEM((SUBLANES, tq), F32),
                        pltpu.VMEM((2 * SUBLANES, tq), F32),
                        pltpu.VMEM((A_HEADS, SUBLANES, tq), F32),
                        pltpu.VMEM((A_HEADS, A_HEAD_DIM + TRI_PAD, tq), F32)],
        compiler_params=_cparams(("parallel", "arbitrary")),
        name="dsa_attention",
    )(qT, qiT, wT, ki, ka, vT, _prefix_matrix(ck))
    return out.reshape(bsz * seq, A_WIDTH)


def _prefix_matrix(n):
    r = jnp.arange(n + TRI_PAD)[:, None]
    c = jnp.arange(n)[None, :]
    return jnp.logical_or(c < r, r >= n).astype(BF16)


SB_CK = 256
SB_TQ = 1024
SB_STEP = 4


def _sb_kernel(q_ref, kT_ref, v_ref, u_ref, o_ref, acc_ref, carry_ref, *, tq, ck):
    i = pl.program_id(2)
    q = q_ref[...]
    acc_ref[...] = jnp.zeros(acc_ref.shape, F32)
    carry_ref[...] = jnp.zeros(carry_ref.shape, F32)
    per = tq // ck

    def chunk(c, r0=None):
        diag = r0 is not None
        r0 = r0 or 0
        rows = tq - r0
        z = jnp.dot(q[r0:, :], kT_ref[c], preferred_element_type=F32)
        neg_abs = pltpu.bitcast(pltpu.bitcast(z, I32) | jnp.int32(-2 ** 31), F32)
        neg_soft = jnp.log(1.0 + jnp.exp2(neg_abs)) * (-LOG2E)
        lk = neg_soft - jnp.maximum(z, 0.0)
        if diag:
            q_pos = i * tq + r0 + lax.broadcasted_iota(I32, (rows, ck), 0)
            mask = (c * ck + lax.broadcasted_iota(I32, (rows, ck), 1)) < q_pos
            lk = jnp.where(mask, lk, 0.0)
        cum = jnp.dot(lk.astype(BF16), u_ref[...], preferred_element_type=F32)
        a = jnp.exp2(z + cum)
        if diag:
            a = jnp.where(mask, a, 0.0)
        vs = pl.multiple_of(c * ck, ck)
        pv = jnp.dot(a.astype(BF16), v_ref[pl.ds(vs, ck), :], preferred_element_type=F32)
        return pv, cum[:, 0:1], r0

    def fold(parts):
        for pv, total, r0 in parts:
            carry = carry_ref[r0:, :]
            acc_ref[r0:, :] += jnp.exp2(carry) * pv
            carry_ref[r0:, :] = carry + jnp.broadcast_to(total, carry.shape)

    step = min(SB_STEP, per)
    for d0 in reversed(range(0, per, step)):
        fold([chunk(i * per + d0 + d, (d0 + d) * ck) for d in reversed(range(step))])

    def body(s, carry):
        first = i * per - 1 - s * step
        fold([chunk(first - d) for d in range(step)])
        return carry

    lax.fori_loop(0, i * (per // step), body, 0)
    o_ref[...] = acc_ref[...].astype(o_ref.dtype)


def _cumsum_matrix(n):
    j = jnp.arange(n)[:, None]
    s = jnp.arange(n)[None, :]
    return (j >= s).astype(BF16)


def _sb_attention(sq, skT, sv, bsz, seq):
    t = bsz * seq
    tq = min(SB_TQ, seq)
    ck = min(SB_CK, seq)
    nq = seq // tq
    return pl.pallas_call(
        functools.partial(_sb_kernel, tq=tq, ck=ck),
        grid=(bsz, SB_HEADS, nq),
        in_specs=[
            pl.BlockSpec((tq, SB_HEAD_DIM), lambda b, h, i: (b * nq + i, h)),
            pl.BlockSpec((None, None, seq // ck, SB_HEAD_DIM, ck), lambda b, h, i: (b, h, 0, 0, 0)),
            pl.BlockSpec((None, seq, SB_HEAD_DIM), lambda b, h, i: (b, 0, h)),
            pl.BlockSpec((ck, ck), lambda b, h, i: (0, 0)),
        ],
        out_specs=pl.BlockSpec((tq, SB_HEAD_DIM), lambda b, h, i: (b * nq + i, h)),
        out_shape=jax.ShapeDtypeStruct((t, SB_WIDTH), BF16),
        scratch_shapes=[pltpu.VMEM((tq, SB_HEAD_DIM), F32), pltpu.VMEM((tq, LANES), F32)],
        compiler_params=_cparams(("parallel", "parallel", "arbitrary")),
        name="stickbreak",
    )(sq, skT, sv, _cumsum_matrix(ck))


CB_HALO = 32
CC_HALO = 16
CONV_ROWS = 64


def _conv_kernel(ga_ref, gb_ref, hga_ref, hgb_ref, cb_ref, cc_ref, cx_ref, hcc_ref, hcx_ref,
                 wb_ref, bb_ref, lng_ref, lnb_ref, wc_ref, ob_ref, oc_ref, buf_ref, u_ref, buf2_ref, *, ts, seq):
    i = pl.program_id(0)
    keep = jnp.where((i * ts) % seq == 0, 0.0, 1.0)

    f32 = lambda ref: ref[...].astype(F32)
    buf_ref[0:CB_HALO, :] = f32(hga_ref) * jax.nn.sigmoid(f32(hgb_ref)) * keep
    buf_ref[CB_HALO:, :] = f32(ga_ref) * jax.nn.sigmoid(f32(gb_ref))
    off0 = CB_HALO - (CB_CONV - 1)
    for r in range(ts // CONV_ROWS):
        t0 = r * CONV_ROWS
        for g in range(CB_WIDTH // LANES):
            ls = slice(g * LANES, (g + 1) * LANES)
            acc = jnp.broadcast_to(bb_ref[:, ls], (CONV_ROWS, LANES))
            for s in range(SUBLANES):
                rows = CONV_ROWS + (SUBLANES if s else 0)
                part = None
                for k in range(CB_CONV):
                    if (off0 + k) % SUBLANES != s:
                        continue
                    base = t0 + (off0 + k) - s
                    term = wb_ref[k:k + 1, ls] * buf_ref[base:base + rows, ls]
                    part = term if part is None else part + term
                acc = acc + part[s:s + CONV_ROWS, :]
            u_ref[t0:t0 + CONV_ROWS, ls] = acc
    u = u_ref[...]
    mu = jnp.mean(u, axis=-1, keepdims=True)
    uc = u - mu
    y = uc * lax.rsqrt(jnp.mean(uc * uc, axis=-1, keepdims=True) + NORM_EPS) * lng_ref[...] + lnb_ref[...]
    ob_ref[...] = (y * jax.nn.sigmoid(y)).astype(ob_ref.dtype)

    buf2_ref[0:CC_HALO, :] = f32(hcc_ref) * f32(hcx_ref) * keep
    buf2_ref[CC_HALO:, :] = f32(cc_ref) * f32(cx_ref)
    conv = jnp.zeros((ts, CC_WIDTH), F32)
    for k in range(CC_CONV):
        base = CC_HALO - (CC_CONV - 1) + k
        conv = conv + wc_ref[k:k + 1, :] * buf2_ref[base:base + ts, :]
    oc_ref[...] = (f32(cb_ref) * conv).astype(oc_ref.dtype)


def _conv_mixers(proj, wb, bb, lng, lnb, wc, seq, ts):
    t = proj.shape[0]
    w = CB_WIDTH
    c0 = (OFF_GLU - SPLIT_COL) // w
    c1 = (OFF_CC - SPLIT_COL) // w
    cur = lambda col: pl.BlockSpec((ts, w), lambda i: (i, col))
    halo = lambda rows, col: pl.BlockSpec((rows, w), lambda i: (jnp.maximum(i * (ts // rows) - 1, 0), col))
    full = lambda r: pl.BlockSpec((r, w), lambda i: (0, 0))
    wb_p = jnp.pad(wb, ((0, CB_HALO - CB_CONV), (0, 0)))
    wc_p = jnp.pad(wc, ((0, SUBLANES - CC_CONV), (0, 0)))
    return pl.pallas_call(
        functools.partial(_conv_kernel, ts=ts, seq=seq),
        grid=(t // ts,),
        in_specs=[cur(c0), cur(c0 + 1), halo(CB_HALO, c0), halo(CB_HALO, c0 + 1),
                  cur(c1), cur(c1 + 1), cur(c1 + 2), halo(CC_HALO, c1 + 1), halo(CC_HALO, c1 + 2),
                  full(CB_HALO), full(1), full(1), full(1), full(SUBLANES)],
        out_specs=[pl.BlockSpec((ts, w), lambda i: (i, 0))] * 2,
        out_shape=[jax.ShapeDtypeStruct((t, w), BF16)] * 2,
        scratch_shapes=[pltpu.VMEM((CB_HALO + ts, w), F32), pltpu.VMEM((ts, w), F32),
                        pltpu.VMEM((CC_HALO + ts, w), F32)],
        compiler_params=_cparams(("parallel",)),
        name="conv_mixers",
    )(proj, proj, proj, proj, proj, proj, proj, proj, proj,
      wb_p, bb.reshape(1, w), lng.reshape(1, w), lnb.reshape(1, w), wc_p)


def _merge_kernel(x_ref, oa_ref, ob_ref, oc_ref, od_ref, g0_ref, g1_ref, g2_ref, g3_ref, bg_ref,
                  wa_ref, wb_ref, wc_ref, wd_ref, wm_ref, o_ref):
    merged = None
    branches = ((oa_ref, wa_ref, g0_ref), (ob_ref, wb_ref, g1_ref), (oc_ref, wc_ref, g2_ref), (od_ref, wd_ref, g3_ref))
    for n, (b_ref, w_ref, g_ref) in enumerate(branches):
        y = jnp.dot(b_ref[...], w_ref[...], preferred_element_type=F32)
        term = jax.nn.sigmoid(g_ref[...].astype(F32) + bg_ref[n:n + 1, :]) * y
        merged = term if merged is None else merged + term
    o_ref[...] = x_ref[...] + jnp.dot(merged.astype(BF16), wm_ref[...], preferred_element_type=F32)


def _merge(x2d, oa, ob, oc, od, proj, b_gate, w_oa, w_ob, w_oc, w_od, w_merge, tm):
    t, d = x2d.shape
    g0 = (OFF_GATE - SPLIT_COL) // d
    row = lambda i: (i, 0)
    fixed = lambda i: (0, 0)
    br = pl.BlockSpec((tm, A_WIDTH), row)
    gate = lambda n: pl.BlockSpec((tm, d), lambda i: (i, g0 + n))
    wsp = pl.BlockSpec((A_WIDTH, d), fixed)
    bf = lambda w: w.astype(BF16)
    return pl.pallas_call(
        _merge_kernel,
        grid=(t // tm,),
        in_specs=[pl.BlockSpec((tm, d), row), br, br, br, br, gate(0), gate(1), gate(2), gate(3),
                  pl.BlockSpec((N_BRANCH, d), fixed), wsp, wsp, wsp, wsp, pl.BlockSpec((d, d), fixed)],
        out_specs=pl.BlockSpec((tm, d), row),
        out_shape=jax.ShapeDtypeStruct((t, d), F32),
        compiler_params=_cparams(("parallel",)),
        name="merge",
    )(x2d, oa, ob, oc, od, proj, proj, proj, proj, b_gate.reshape(N_BRANCH, d),
      bf(w_oa), bf(w_ob), bf(w_oc), bf(w_od), bf(w_merge))


FFN_HALO = 8


def _ffn_kernel(x_ref, hx_ref, g_ref, wg_ref, wu_ref, cw_ref, wd_ref, o_ref, h_ref, gp_ref, acc_ref, *, tm, seq):
    i = pl.program_id(0)
    f = pl.program_id(1)

    def norm(x):
        ms = jnp.mean(x * x, axis=-1, keepdims=True)
        return (x * lax.rsqrt(ms + NORM_EPS) * g_ref[...]).astype(BF16)

    @pl.when(f == 0)
    def _():
        h_ref[0:FFN_HALO, :] = norm(hx_ref[...])
        h_ref[FFN_HALO:, :] = norm(x_ref[...])
        acc_ref[...] = x_ref[...]

    keep = jnp.where((i * tm) % seq == 0, 0.0, 1.0)
    gp_ref[0:FFN_HALO, :] = jnp.dot(h_ref[0:FFN_HALO, :], wg_ref[...], preferred_element_type=F32) * keep
    gp_ref[FFN_HALO:, :] = jnp.dot(h_ref[FFN_HALO:, :], wg_ref[...], preferred_element_type=F32)
    gt = jnp.zeros((tm, gp_ref.shape[1]), F32)
    for k in range(FFN_CONV):
        base = FFN_HALO - (FFN_CONV - 1) + k
        gt = gt + cw_ref[k:k + 1, :] * gp_ref[base:base + tm, :]
    up = jnp.dot(h_ref[FFN_HALO:, :], wu_ref[...], preferred_element_type=F32)
    act = (gt * jax.nn.sigmoid(gt) * up).astype(BF16)
    acc_ref[...] += jnp.dot(act, wd_ref[...], preferred_element_type=F32)

    @pl.when(f == pl.num_programs(1) - 1)
    def _():
        o_ref[...] = acc_ref[...]


def _ffn(x2d, g, wg, wu, cw, wd, seq, tm, tf):
    t, d = x2d.shape
    ff = wg.shape[1]
    cw_p = jnp.pad(cw, ((0, SUBLANES - FFN_CONV), (0, 0)))
    return pl.pallas_call(
        functools.partial(_ffn_kernel, tm=tm, seq=seq),
        grid=(t // tm, ff // tf),
        in_specs=[
            pl.BlockSpec((tm, d), lambda i, f: (i, 0)),
            pl.BlockSpec((FFN_HALO, d), lambda i, f: (jnp.maximum(i * (tm // FFN_HALO) - 1, 0), 0)),
            pl.BlockSpec((1, d), lambda i, f: (0, 0)),
            pl.BlockSpec((d, tf), lambda i, f: (0, f)),
            pl.BlockSpec((d, tf), lambda i, f: (0, f)),
            pl.BlockSpec((SUBLANES, tf), lambda i, f: (0, f)),
            pl.BlockSpec((tf, d), lambda i, f: (f, 0)),
        ],
        out_specs=pl.BlockSpec((tm, d), lambda i, f: (i, 0)),
        out_shape=jax.ShapeDtypeStruct((t, d), F32),
        scratch_shapes=[pltpu.VMEM((FFN_HALO + tm, d), BF16), pltpu.VMEM((FFN_HALO + tm, tf), F32),
                        pltpu.VMEM((tm, d), F32)],
        compiler_params=_cparams(("parallel", "arbitrary")),
        name="ffn",
    )(x2d, x2d, g.reshape(1, d), wg.astype(BF16), wu.astype(BF16), cw_p, wd.astype(BF16))


def _tiles(seq):
    big = min(seq, 512)
    return dict(
        inproj_tm=min(seq, 1024), inproj_tn=1024,
        conv_ts=big, merge_tm=big, ffn_tm=big, ffn_tf=D_FF // 2,
        dsa_tq=min(seq, 512),
    )


def kernel(x, positions, g_mix, w_in, b_gate, g_qa, g_ka, g_kidx, w_oa, cb_conv_w, cb_conv_b, cb_ln_g, cb_ln_b,
           w_ob, cc_conv_w, w_oc, w_od, w_merge, g_ffn, w_ffn_gate, w_ffn_up, ffn_conv_w, w_ffn_down):
    bsz, seq, dm = x.shape
    depth = w_in.shape[0]
    assert dm == D_MODEL and seq % min(CHUNK, seq) == 0 and seq % LANES == 0
    tl = _tiles(seq)
    t = bsz * seq

    cos_a, sin_a = _rope_tables(positions, A_HEAD_DIM)
    cos_i, sin_i = _rope_tables(positions, IDX_DIM)
    ca, sa = _rope_lane_tables(cos_a, sin_a, A_HEAD_DIM)
    ci, si = _rope_lane_tables(cos_i, sin_i, IDX_DIM)

    x2d = x.reshape(t, dm)
    for l in range(depth):
        w_packed = _pack_w_in(jnp.swapaxes(w_in[l], 0, 1))
        tm, tn = tl["inproj_tm"], tl["inproj_tn"]
        proj_a = _inproj(x2d, g_mix[l], w_packed, tm, tn, 0, SPLIT_COL, F32)
        proj = _inproj(x2d, g_mix[l], w_packed, tm, tn, SPLIT_COL, N_PACK - SPLIT_COL, BF16)
        qT, qiT, wT, ki, ka, vT, sq, skT, sv = _prep(proj_a, proj, ca, sa, ci, si, g_qa[l], g_ka[l], g_kidx[l],
                                                     bsz, seq)
        o_a = _dsa_attention(qT, qiT, wT, ki, ka, vT, bsz, seq, tl["dsa_tq"])
        o_d = _sb_attention(sq, skT, sv, bsz, seq)
        o_b, o_c = _conv_mixers(proj, cb_conv_w[l], cb_conv_b[l], cb_ln_g[l], cb_ln_b[l], cc_conv_w[l],
                                seq, tl["conv_ts"])
        x2d = _merge(x2d, o_a, o_b, o_c, o_d, proj, b_gate[l], w_oa[l], w_ob[l], w_oc[l], w_od[l], w_merge[l],
                     tl["merge_tm"])
        x2d = _ffn(x2d, g_ffn[l], w_ffn_gate[l], w_ffn_up[l], ffn_conv_w[l], w_ffn_down[l], seq,
                   tl["ffn_tm"], tl["ffn_tf"])
    return x2d.reshape(bsz, seq, dm)
```

```python
import functools
import math

import jax
import jax.numpy as jnp
from jax import lax
from jax.experimental import pallas as pl
from jax.experimental.pallas import tpu as pltpu

F32 = jnp.float32
BF16 = jnp.bfloat16
I32 = jnp.int32

D_MODEL = 1024
N_BRANCH = 4
A_HEADS = 4
A_HEAD_DIM = 128
A_WIDTH = A_HEADS * A_HEAD_DIM
IDX_HEADS = 4
IDX_DIM = 64
TOPK_MAX = 256
CB_WIDTH = 512
CB_CONV = 31
CC_WIDTH = 512
CC_CONV = 3
SB_HEADS = 4
SB_HEAD_DIM = 128
SB_WIDTH = SB_HEADS * SB_HEAD_DIM
D_FF = 2816
FFN_CONV = 3
ROPE_THETA = 500000.0
ROT_FRACTION_DIV = 4
NORM_EPS = 1e-6

LANES = 128
SUBLANES = 8
VMEM_LIMIT = 56 * 1024 * 1024

LOG2E = math.log2(math.e)
NEG_BIG = -1e30
M_INIT = -1e20
F32_LOWEST = float(jnp.finfo(jnp.float32).min)
KEY_MIN_NORMAL = 0x00800000
TRI_PAD = 16
COUNT_VPU_SHARE = 4
VALUE_PASSES = 12
SEARCH_PASS_CAP = VALUE_PASSES + 36

OFF_AQ = 0
OFF_AK = 512
OFF_AV = 1024
OFF_IQ = 1536
OFF_IK = 1792
OFF_IW = 1920
OFF_GLU = 2048
OFF_CC = 3072
OFF_SB = 4608
OFF_GATE = 6144
N_PACK = 10240
SPLIT_COL = 2048
CHUNK = 512


def _cparams(sem):
    return pltpu.CompilerParams(dimension_semantics=sem, vmem_limit_bytes=VMEM_LIMIT)


def _trig_kernel(pos_ref, invf_ref, cos_ref, sin_ref):
    ang = pos_ref[...] * invf_ref[...]
    cos_ref[...] = jnp.cos(ang)
    sin_ref[...] = jnp.sin(ang)


def _rope_tables(positions, head_dim):
    rot = head_dim // ROT_FRACTION_DIV
    half = rot // 2
    inv_freq = ROPE_THETA ** (-(jnp.arange(half, dtype=F32) * 2.0) / rot)
    t = positions.size
    rows = t * half // LANES
    pos_rep = jnp.broadcast_to(positions.astype(F32).reshape(t, 1), (t, half)).reshape(rows, LANES)
    invf = jnp.tile(inv_freq, LANES // half).reshape(1, LANES)
    tr = min(rows, 512)
    cos, sin = pl.pallas_call(
        _trig_kernel,
        grid=(rows // tr,),
        in_specs=[pl.BlockSpec((tr, LANES), lambda i: (i, 0)), pl.BlockSpec((1, LANES), lambda i: (0, 0))],
        out_specs=[pl.BlockSpec((tr, LANES), lambda i: (i, 0))] * 2,
        out_shape=[jax.ShapeDtypeStruct((rows, LANES), F32)] * 2,
        compiler_params=_cparams(("parallel",)),
        name="rope_trig",
    )(pos_rep, invf)
    return cos.reshape(t, half), sin.reshape(t, half)


def _rope_lane_tables(cos, sin, head_dim):
    t, half = cos.shape
    rest = head_dim - 2 * half
    c = jnp.concatenate([cos, cos, jnp.ones((t, rest), F32)], axis=1)
    s = jnp.concatenate([sin, sin, jnp.zeros((t, rest), F32)], axis=1)
    reps = LANES // head_dim
    return jnp.tile(c, (1, reps)), jnp.tile(s, (1, reps))


def _inproj_kernel(x_ref, g_ref, w_ref, oa_ref, ob_ref, h_ref, *, n_a):
    j = pl.program_id(1)

    @pl.when(j == 0)
    def _():
        x = x_ref[...]
        ms = jnp.mean(x * x, axis=-1, keepdims=True)
        h_ref[...] = (x * lax.rsqrt(ms + NORM_EPS) * g_ref[...]).astype(BF16)

    y = jnp.dot(h_ref[...], w_ref[...], preferred_element_type=F32)

    @pl.when(j < n_a)
    def _():
        oa_ref[...] = y

    @pl.when(j >= n_a)
    def _():
        ob_ref[...] = y.astype(ob_ref.dtype)


def _inproj(x2d, g, w_packed, tm, tn):
    t, d = x2d.shape
    n = w_packed.shape[1]
    n_a = SPLIT_COL // tn
    return pl.pallas_call(
        functools.partial(_inproj_kernel, n_a=n_a),
        grid=(t // tm, n // tn),
        in_specs=[
            pl.BlockSpec((tm, d), lambda i, j: (i, 0)),
            pl.BlockSpec((1, d), lambda i, j: (0, 0)),
            pl.BlockSpec((d, tn), lambda i, j: (0, j)),
        ],
        out_specs=[pl.BlockSpec((tm, tn), lambda i, j: (i, jnp.minimum(j, n_a - 1))),
                   pl.BlockSpec((tm, tn), lambda i, j: (i, jnp.maximum(j - n_a, 0)))],
        out_shape=[jax.ShapeDtypeStruct((t, SPLIT_COL), F32), jax.ShapeDtypeStruct((t, n - SPLIT_COL), BF16)],
        scratch_shapes=[pltpu.VMEM((tm, d), BF16)],
        compiler_params=_cparams(("parallel", "arbitrary")),
        name="inproj",
    )(x2d, g.reshape(1, d), w_packed)


N_IN = 10052
OFF_IW_SRC = 1856
OFF_REST_SRC = 1860
PACK_FIRST = OFF_REST_SRC // LANES
PACK_ROT = (OFF_GLU - OFF_REST_SRC) % LANES


def _pack_kernel(a_ref, b_ref, o_ref):
    n = pl.program_id(0)
    row = lax.broadcasted_iota(I32, a_ref.shape, 0)
    a = a_ref[...]

    @pl.when(n < OFF_IK // LANES)
    def _():
        o_ref[...] = a.T.astype(BF16)

    @pl.when(n == OFF_IK // LANES)
    def _():
        o_ref[...] = jnp.where(row < IDX_DIM, a, 0.0).T.astype(BF16)

    @pl.when(n == OFF_IW // LANES)
    def _():
        iw = pltpu.roll(a, LANES - (OFF_IW_SRC - OFF_IK), 0)
        o_ref[...] = jnp.where(row < IDX_HEADS, iw, 0.0).T.astype(BF16)

    @pl.when(n >= OFF_GLU // LANES)
    def _():
        out = jnp.where(row < PACK_ROT, pltpu.roll(a, PACK_ROT, 0), pltpu.roll(b_ref[...], PACK_ROT, 0))
        o_ref[...] = out.T.astype(BF16)


def _pack_w_in(w_t):
    d = w_t.shape[1]
    first_rest = OFF_GLU // LANES

    def a_idx(n):
        return (jnp.where(n < first_rest, jnp.minimum(n, PACK_FIRST), n - first_rest + PACK_FIRST), 0)

    def b_idx(n):
        return (jnp.where(n < first_rest, PACK_FIRST, n - first_rest + PACK_FIRST + 1), 0)

    return pl.pallas_call(
        _pack_kernel,
        grid=(N_PACK // LANES,),
        in_specs=[pl.BlockSpec((LANES, d), a_idx), pl.BlockSpec((LANES, d), b_idx)],
        out_specs=pl.BlockSpec((d, LANES), lambda n: (0, n)),
        out_shape=jax.ShapeDtypeStruct((d, N_PACK), BF16),
        compiler_params=_cparams(("parallel",)),
        name="pack_w_in",
    )(w_t, w_t)


def _rotate_half(x, half, period):
    lane = lax.broadcasted_iota(I32, x.shape, 1)
    up = pltpu.roll(x, LANES - half, 1)
    dn = pltpu.roll(x, half, 1)
    return jnp.where((lane % period) < half, -up, dn)


def _prep_kernel(p_ref, sqi_ref, ski_ref, svi_ref, ca_ref, sa_ref, ci_ref, si_ref, gq_ref, gk_ref, gi_ref,
                 qT_ref, qiT_ref, wT_ref, ki_ref, ka_ref, vT_ref, sq_ref, skT_ref, sv_ref):
    ca, sa, ci, si = ca_ref[...], sa_ref[...], ci_ref[...], si_ref[...]
    a_half = A_HEAD_DIM // ROT_FRACTION_DIV // 2
    i_half = IDX_DIM // ROT_FRACTION_DIV // 2

    def head_norm(x, g, n_real):
        ms = jnp.sum(x * x, axis=-1, keepdims=True) * (1.0 / n_real)
        return x * lax.rsqrt(ms + NORM_EPS) * g

    for h in range(A_HEADS):
        sl = slice(h * A_HEAD_DIM, (h + 1) * A_HEAD_DIM)
        col = lambda off: slice(off + h * A_HEAD_DIM, off + (h + 1) * A_HEAD_DIM)
        q = head_norm(p_ref[:, col(OFF_AQ)], gq_ref[...], A_HEAD_DIM)
        q = q * ca + _rotate_half(q, a_half, A_HEAD_DIM) * sa
        qT_ref[sl, :] = (q * (A_HEAD_DIM ** -0.5 * LOG2E)).T.astype(BF16)
        k = head_norm(p_ref[:, col(OFF_AK)], gk_ref[...], A_HEAD_DIM)
        ka_ref[:, sl] = (k * ca + _rotate_half(k, a_half, A_HEAD_DIM) * sa).astype(BF16)
        vT_ref[sl, :] = p_ref[:, col(OFF_AV)].T.astype(BF16)

    row = lax.broadcasted_iota(I32, (LANES, p_ref.shape[0]), 0)
    for j in range(IDX_HEADS * IDX_DIM // LANES):
        qi = p_ref[:, OFF_IQ + j * LANES:OFF_IQ + (j + 1) * LANES]
        qt = (qi * ci + _rotate_half(qi, i_half, IDX_DIM) * si).T
        qiT_ref[(2 * j) * LANES:(2 * j + 1) * LANES, :] = jnp.where(row < IDX_DIM, qt, 0.0).astype(BF16)
        qiT_ref[(2 * j + 1) * LANES:(2 * j + 2) * LANES, :] = jnp.where(row >= IDX_DIM, qt, 0.0).astype(BF16)
    ki = head_norm(p_ref[:, OFF_IK:OFF_IK + LANES], gi_ref[...], IDX_DIM)
    ki = ki * ci + _rotate_half(ki, i_half, IDX_DIM) * si
    ki_ref[...] = (ki + pltpu.roll(ki, IDX_DIM, 1)).astype(BF16)
    w = p_ref[:, OFF_IW:OFF_IW + LANES] * (IDX_DIM ** -0.5 * IDX_HEADS ** -0.5)
    wT_ref[...] = w.T[0:SUBLANES, :]

    sq_ref[...] = (sqi_ref[...].astype(F32) * (SB_HEAD_DIM ** -0.5 * LOG2E)).astype(BF16)
    for h in range(SB_HEADS):
        k_t = ski_ref[:, h * SB_HEAD_DIM:(h + 1) * SB_HEAD_DIM].astype(F32).T.astype(BF16)
        sb_ck = skT_ref.shape[-1]
        for c in range(skT_ref.shape[1]):
            skT_ref[h, c] = k_t[:, c * sb_ck:(c + 1) * sb_ck]
    sv_ref[...] = svi_ref[...]


def _prep(proj_a, proj_b, ca, sa, ci, si, g_qa, g_ka, g_kidx, bsz, seq):
    t = proj_a.shape[0]
    ck = min(CHUNK, seq)
    sb_ck = min(SB_CK, seq)
    nck = seq // ck
    gi = jnp.concatenate([g_kidx, jnp.ones((LANES - IDX_DIM,), F32)]).reshape(1, LANES)
    tok = lambda b, i: (b * nck + i, 0)
    fixed = lambda b, i: (0, 0)
    tbl = pl.BlockSpec((ck, LANES), tok)
    gsp = pl.BlockSpec((1, LANES), fixed)
    sb_col = (OFF_SB - SPLIT_COL) // SB_WIDTH
    sb_in = lambda n: pl.BlockSpec((ck, SB_WIDTH), lambda b, i: (b * nck + i, sb_col + n))
    return pl.pallas_call(
        _prep_kernel,
        grid=(bsz, nck),
        in_specs=[pl.BlockSpec((ck, SPLIT_COL), tok), sb_in(0), sb_in(1), sb_in(2),
                  tbl, tbl, tbl, tbl, gsp, gsp, gsp],
        out_specs=[
            pl.BlockSpec((None, A_WIDTH, ck), lambda b, i: (b, 0, i)),
            pl.BlockSpec((None, IDX_HEADS * LANES, ck), lambda b, i: (b, 0, i)),
            pl.BlockSpec((None, SUBLANES, ck), lambda b, i: (b, 0, i)),
            pl.BlockSpec((None, ck, LANES), lambda b, i: (b, i, 0)),
            pl.BlockSpec((None, ck, A_WIDTH), lambda b, i: (b, i, 0)),
            pl.BlockSpec((None, None, A_WIDTH, ck), lambda b, i: (b, i, 0, 0)),
            pl.BlockSpec((ck, SB_WIDTH), tok),
            pl.BlockSpec((None, SB_HEADS, ck // sb_ck, SB_HEAD_DIM, sb_ck), lambda b, i: (b, 0, i, 0, 0)),
            pl.BlockSpec((None, ck, SB_WIDTH), lambda b, i: (b, i, 0)),
        ],
        out_shape=[
            jax.ShapeDtypeStruct((bsz, A_WIDTH, seq), BF16),
            jax.ShapeDtypeStruct((bsz, IDX_HEADS * LANES, seq), BF16),
            jax.ShapeDtypeStruct((bsz, SUBLANES, seq), F32),
            jax.ShapeDtypeStruct((bsz, seq, LANES), BF16),
            jax.ShapeDtypeStruct((bsz, seq, A_WIDTH), BF16),
            jax.ShapeDtypeStruct((bsz, nck, A_WIDTH, ck), BF16),
            jax.ShapeDtypeStruct((t, SB_WIDTH), BF16),
            jax.ShapeDtypeStruct((bsz, SB_HEADS, seq // sb_ck, SB_HEAD_DIM, sb_ck), BF16),
            jax.ShapeDtypeStruct((bsz, seq, SB_WIDTH), BF16),
        ],
        compiler_params=_cparams(("parallel", "parallel")),
        name="mixer_prep",
    )(proj_a, proj_b, proj_b, proj_b, ca, sa, ci, si, g_qa.reshape(1, LANES), g_ka.reshape(1, LANES), gi)


def _order_key(f):
    b = pltpu.bitcast(f, I32)
    return b ^ ((b >> 31) & 0x7FFFFFFF)


def _key_float(k):
    return pltpu.bitcast(k ^ ((k >> 31) & 0x7FFFFFFF), F32)


def _normal_key(k):
    k = jnp.where(jnp.logical_and(k > 0, k < KEY_MIN_NORMAL), KEY_MIN_NORMAL, k)
    return jnp.where(jnp.logical_and(k < 0, k >= -KEY_MIN_NORMAL), 0, k)


def _dsa_kernel(qT_ref, qiT_ref, wT_ref, ki_ref, ka_ref, vT_ref, tri_ref, o_ref, sc_ref, red_ref, cnt_ref, m_ref,
                acc_ref, *, tq, ck, k_sel):
    i = pl.program_id(1)
    n_keys = (i + 1) * tq
    n_chunks = (n_keys + ck - 1) // ck
    q_pos = i * tq + lax.broadcasted_iota(I32, (1, tq), 1)

    def chunk_start(c):
        return pl.multiple_of(c * ck, ck)

    def key_pos(c):
        return c * ck + lax.broadcasted_iota(I32, (ck, tq), 0)

    def score_chunk(c, diag):
        ks = chunk_start(c)
        kic = ki_ref[pl.ds(ks, ck), :]
        sc = None
        for h in range(IDX_HEADS):
            d = jnp.dot(kic, qiT_ref[h * LANES:(h + 1) * LANES, :], preferred_element_type=F32)
            term = jnp.maximum(d, 0.0) * wT_ref[h:h + 1, :]
            sc = term if sc is None else sc + term
        if diag:
            causal = key_pos(c) <= q_pos
            sc_hi = jnp.where(causal, sc, -jnp.inf)
            sc_lo = jnp.where(causal, sc, jnp.inf)
        else:
            sc_hi = sc_lo = sc
        sc_ref[pl.ds(ks, ck), :] = sc_hi
        fold = lambda a, op: op(a.reshape(ck // SUBLANES, SUBLANES, tq), axis=0)
        return fold(sc_hi, jnp.max), fold(sc_lo, jnp.min)

    def score_body(c, carry):
        mx, mn = score_chunk(c, False)
        return jnp.maximum(carry[0], mx), jnp.minimum(carry[1], mn)

    init = (jnp.full((SUBLANES, tq), -jnp.inf, F32), jnp.full((SUBLANES, tq), jnp.inf, F32))
    mx, mn = lax.fori_loop(0, n_chunks - 1, score_body, init)
    mx_d, mn_d = score_chunk(n_chunks - 1, True)
    def fold_rows(x, op):
        red_ref[...] = x
        r = red_ref[0:1, :]
        for j in range(1, SUBLANES):
            r = op(r, red_ref[j:j + 1, :])
        return r

    row_max = fold_rows(jnp.maximum(mx, mx_d), jnp.maximum)
    row_min = fold_rows(jnp.minimum(mn, mn_d), jnp.minimum)

    mxu_rows = ck - ck // COUNT_VPU_SHARE
    ones = jnp.ones((SUBLANES, mxu_rows), BF16)

    def count_ge(t):
        for nc in range(1, sc_ref.shape[0] // ck + 1):
            @pl.when(n_chunks == nc)
            def _(nc=nc):
                on_mxu = None
                on_vpu = jnp.zeros((SUBLANES, tq), F32)
                for c in range(nc):
                    ind = jnp.where(sc_ref[c * ck:c * ck + mxu_rows, :] >= t, 1.0, 0.0).astype(BF16)
                    part = jnp.dot(ones, ind, preferred_element_type=F32)
                    on_mxu = part if on_mxu is None else on_mxu + part
                    tail = jnp.where(sc_ref[c * ck + mxu_rows:(c + 1) * ck, :] >= t, 1.0, 0.0)
                    on_vpu = on_vpu + tail.reshape(-1, SUBLANES, tq).sum(axis=0)
                cnt_ref[0:SUBLANES, :] = on_vpu
                cnt_ref[SUBLANES:, :] = on_mxu
        total = cnt_ref[SUBLANES:SUBLANES + 1, :]
        for j in range(SUBLANES):
            total = total + cnt_ref[j:j + 1, :]
        return total.astype(I32)

    n_causal = q_pos + 1
    take_all = n_causal <= k_sel
    lo0 = _normal_key(_order_key(row_min))
    hi0 = _normal_key(_order_key(row_max)) + 1
    thr0 = jnp.full((1, tq), F32_LOWEST, F32)
    done0 = jnp.where(take_all, 1, 0)
    zero = jnp.zeros((1, tq), I32)

    def search_cond(st):
        it, _, _, _, _, done, _ = st
        return jnp.logical_and(it < SEARCH_PASS_CAP, jnp.min(done) == 0)

    def search_body(st):
        it, lo, hi, c_hi, thr, done, tied = st
        cross = jnp.logical_and(lo < 0, hi > 0)
        gap = hi - lo
        zero_probe = jnp.logical_and(it == 1, lo == 0)
        mid = _normal_key(jnp.where(cross, 0, jnp.where(zero_probe, KEY_MIN_NORMAL, lo + (gap >> 1))))
        v_mid = _normal_key(_order_key(0.5 * _key_float(lo) + 0.5 * _key_float(hi)))
        by_value = jnp.logical_and(jnp.logical_and(it < VALUE_PASSES, jnp.logical_not(jnp.logical_or(cross, zero_probe))),
                                   jnp.logical_and(v_mid > lo, v_mid < hi))
        mid = jnp.where(by_value, v_mid, mid)
        no_pivot = jnp.logical_or(gap == 1, jnp.logical_or(mid >= hi, mid <= lo))
        adjacent = jnp.logical_and(no_pivot, jnp.logical_not(cross))
        pivot = _key_float(mid)
        cnt = count_ge(pivot)
        live = done == 0
        hit = jnp.logical_and(live, jnp.logical_and(cnt == k_sel, jnp.logical_not(adjacent)))
        end = jnp.logical_and(live, adjacent)
        move = jnp.logical_and(live, jnp.logical_not(jnp.logical_or(hit, end)))
        up = jnp.logical_and(move, cnt > k_sel)
        dn = jnp.logical_and(move, cnt < k_sel)
        thr = jnp.where(hit, pivot, jnp.where(end, _key_float(lo), thr))
        tied = jnp.where(end, 1, tied)
        done = jnp.where(jnp.logical_or(hit, end), 1, done)
        lo = jnp.where(up, mid, lo)
        hi = jnp.where(dn, mid, hi)
        c_hi = jnp.where(dn, cnt, c_hi)
        return it + 1, lo, hi, c_hi, thr, done, tied

    st = lax.while_loop(search_cond, search_body, (jnp.int32(0), lo0, hi0, zero, thr0, done0, zero))
    _, _, _, c_hi, thr, _, tied = st

    room = jnp.where(tied > 0, k_sel - c_hi, jnp.int32(2 ** 30)).astype(F32)

    @pl.when(jnp.max(tied) > 0)
    def _():
        red_ref[0:1, :] = jnp.zeros((1, tq), F32)

        def drop_excess(c, carry):
            ks = chunk_start(c)
            s = sc_ref[pl.ds(ks, ck), :]
            is_tied = s == thr
            ind = jnp.where(is_tied, 1.0, 0.0).astype(BF16)
            pre = jnp.dot(tri_ref[...], ind, preferred_element_type=F32)
            before = red_ref[0:1, :]
            rank = pre[0:ck, :] + before
            sc_ref[pl.ds(ks, ck), :] = jnp.where(is_tied, jnp.where(rank >= room, -jnp.inf, s), s)
            red_ref[0:1, :] = before + pre[ck:ck + 1, :]
            return carry
        lax.fori_loop(0, n_chunks, drop_excess, 0)

    m_ref[...] = jnp.full(m_ref.shape, M_INIT, F32)
    acc_ref[...] = jnp.zeros(acc_ref.shape, F32)
    heads = [slice(h * A_HEAD_DIM, (h + 1) * A_HEAD_DIM) for h in range(A_HEADS)]
    ones_rows = jnp.ones((TRI_PAD, ck), BF16)

    def att_body(c, carry):
        ks = chunk_start(c)
        sel = sc_ref[pl.ds(ks, ck), :] >= thr
        logits = [jnp.dot(ka_ref[pl.ds(ks, ck), hs], qT_ref[hs, :], preferred_element_type=F32) for hs in heads]
        for h, hs in enumerate(heads):
            m_old = m_ref[h, 0:1, :]
            m_new = jnp.maximum(m_old, jnp.max(jnp.where(sel, logits[h], NEG_BIG), axis=0, keepdims=True))
            alpha = jnp.exp2(m_old - m_new)
            p = jnp.where(sel, jnp.exp2(logits[h] - m_new), 0.0).astype(BF16)
            m_ref[h, 0:1, :] = m_new
            lhs = jnp.concatenate([vT_ref[c, hs, :], ones_rows], axis=0)
            acc_ref[h] = acc_ref[h] * alpha + jnp.dot(lhs, p, preferred_element_type=F32)
        return carry

    lax.fori_loop(0, n_chunks, att_body, 0)
    for h, hs in enumerate(heads):
        denom = acc_ref[h, A_HEAD_DIM:A_HEAD_DIM + 1, :]
        o_ref[:, hs] = (acc_ref[h, 0:A_HEAD_DIM, :] / denom).T.astype(o_ref.dtype)


def _dsa_attention(qT, qiT, wT, ki, ka, vT, bsz, seq, tq):
    ck = min(CHUNK, seq)
    nck = seq // ck
    k_sel = min(TOPK_MAX, seq // 4)
    kern = functools.partial(_dsa_kernel, tq=tq, ck=ck, k_sel=k_sel)
    once = pl.Buffered(1)
    out = pl.pallas_call(
        kern,
        grid=(bsz, seq // tq),
        in_specs=[
            pl.BlockSpec((None, A_WIDTH, tq), lambda b, i: (b, 0, i)),
            pl.BlockSpec((None, IDX_HEADS * LANES, tq), lambda b, i: (b, 0, i)),
            pl.BlockSpec((None, SUBLANES, tq), lambda b, i: (b, 0, i)),
            pl.BlockSpec((None, seq, LANES), lambda b, i: (b, 0, 0), pipeline_mode=once),
            pl.BlockSpec((None, seq, A_WIDTH), lambda b, i: (b, 0, 0), pipeline_mode=once),
            pl.BlockSpec((None, nck, A_WIDTH, ck), lambda b, i: (b, 0, 0, 0), pipeline_mode=once),
            pl.BlockSpec((ck + TRI_PAD, ck), lambda b, i: (0, 0), pipeline_mode=once),
        ],
        out_specs=pl.BlockSpec((None, tq, A_WIDTH), lambda b, i: (b, i, 0)),
        out_shape=jax.ShapeDtypeStruct((bsz, seq, A_WIDTH), BF16),
        scratch_shapes=[pltpu.VMEM((seq, tq), F32), pltpu.VMEM((SUBLANES, tq), F32),
                        pltpu.VMEM((2 * SUBLANES, tq), F32),
                        pltpu.VMEM((A_HEADS, SUBLANES, tq), F32),
                        pltpu.VMEM((A_HEADS, A_HEAD_DIM + TRI_PAD, tq), F32)],
        compiler_params=_cparams(("parallel", "arbitrary")),
        name="dsa_attention",
    )(qT, qiT, wT, ki, ka, vT, _prefix_matrix(ck))
    return out.reshape(bsz * seq, A_WIDTH)


def _prefix_matrix(n):
    r = jnp.arange(n + TRI_PAD)[:, None]
    c = jnp.arange(n)[None, :]
    return jnp.logical_or(c < r, r >= n).astype(BF16)


SB_CK = 256
SB_TQ = 1024
SB_STEP = 4


def _sb_kernel(q_ref, kT_ref, v_ref, u_ref, o_ref, acc_ref, carry_ref, *, tq, ck):
    i = pl.program_id(2)
    q = q_ref[...]
    acc_ref[...] = jnp.zeros(acc_ref.shape, F32)
    carry_ref[...] = jnp.zeros(carry_ref.shape, F32)
    per = tq // ck

    def chunk(c, r0=None):
        diag = r0 is not None
        r0 = r0 or 0
        rows = tq - r0
        z = jnp.dot(q[r0:, :], kT_ref[c], preferred_element_type=F32)
        neg_abs = pltpu.bitcast(pltpu.bitcast(z, I32) | jnp.int32(-2 ** 31), F32)
        neg_soft = jnp.log(1.0 + jnp.exp2(neg_abs)) * (-LOG2E)
        lk = neg_soft - jnp.maximum(z, 0.0)
        if diag:
            q_pos = i * tq + r0 + lax.broadcasted_iota(I32, (rows, ck), 0)
            mask = (c * ck + lax.broadcasted_iota(I32, (rows, ck), 1)) < q_pos
            lk = jnp.where(mask, lk, 0.0)
        cum = jnp.dot(lk.astype(BF16), u_ref[...], preferred_element_type=F32)
        a = jnp.exp2(z + cum)
        if diag:
            a = jnp.where(mask, a, 0.0)
        vs = pl.multiple_of(c * ck, ck)
        pv = jnp.dot(a.astype(BF16), v_ref[pl.ds(vs, ck), :], preferred_element_type=F32)
        return pv, cum[:, 0:1], r0

    def fold(parts):
        for pv, total, r0 in parts:
            carry = carry_ref[r0:, :]
            acc_ref[r0:, :] += jnp.exp2(carry) * pv
            carry_ref[r0:, :] = carry + jnp.broadcast_to(total, carry.shape)

    step = min(SB_STEP, per)
    for d0 in reversed(range(0, per, step)):
        fold([chunk(i * per + d0 + d, (d0 + d) * ck) for d in reversed(range(step))])

    def body(s, carry):
        first = i * per - 1 - s * step
        fold([chunk(first - d) for d in range(step)])
        return carry

    lax.fori_loop(0, i * (per // step), body, 0)
    o_ref[...] = acc_ref[...].astype(o_ref.dtype)


def _cumsum_matrix(n):
    j = jnp.arange(n)[:, None]
    s = jnp.arange(n)[None, :]
    return (j >= s).astype(BF16)


def _sb_attention(sq, skT, sv, bsz, seq):
    t = bsz * seq
    tq = min(SB_TQ, seq)
    ck = min(SB_CK, seq)
    nq = seq // tq
    return pl.pallas_call(
        functools.partial(_sb_kernel, tq=tq, ck=ck),
        grid=(bsz, SB_HEADS, nq),
        in_specs=[
            pl.BlockSpec((tq, SB_HEAD_DIM), lambda b, h, i: (b * nq + i, h)),
            pl.BlockSpec((None, None, seq // ck, SB_HEAD_DIM, ck), lambda b, h, i: (b, h, 0, 0, 0)),
            pl.BlockSpec((None, seq, SB_HEAD_DIM), lambda b, h, i: (b, 0, h)),
            pl.BlockSpec((ck, ck), lambda b, h, i: (0, 0)),
        ],
        out_specs=pl.BlockSpec((tq, SB_HEAD_DIM), lambda b, h, i: (b * nq + i, h)),
        out_shape=jax.ShapeDtypeStruct((t, SB_WIDTH), BF16),
        scratch_shapes=[pltpu.VMEM((tq, SB_HEAD_DIM), F32), pltpu.VMEM((tq, LANES), F32)],
        compiler_params=_cparams(("parallel", "parallel", "arbitrary")),
        name="stickbreak",
    )(sq, skT, sv, _cumsum_matrix(ck))


CB_HALO = 32
CC_HALO = 16
CONV_ROWS = 64


def _conv_kernel(ga_ref, gb_ref, hga_ref, hgb_ref, cb_ref, cc_ref, cx_ref, hcc_ref, hcx_ref,
                 wb_ref, bb_ref, lng_ref, lnb_ref, wc_ref, ob_ref, oc_ref, buf_ref, u_ref, buf2_ref, *, ts, seq):
    i = pl.program_id(0)
    keep = jnp.where((i * ts) % seq == 0, 0.0, 1.0)

    f32 = lambda ref: ref[...].astype(F32)
    buf_ref[0:CB_HALO, :] = f32(hga_ref) * jax.nn.sigmoid(f32(hgb_ref)) * keep
    buf_ref[CB_HALO:, :] = f32(ga_ref) * jax.nn.sigmoid(f32(gb_ref))
    off0 = CB_HALO - (CB_CONV - 1)
    for r in range(ts // CONV_ROWS):
        t0 = r * CONV_ROWS
        for g in range(CB_WIDTH // LANES):
            ls = slice(g * LANES, (g + 1) * LANES)
            acc = jnp.broadcast_to(bb_ref[:, ls], (CONV_ROWS, LANES))
            for s in range(SUBLANES):
                rows = CONV_ROWS + (SUBLANES if s else 0)
                part = None
                for k in range(CB_CONV):
                    if (off0 + k) % SUBLANES != s:
                        continue
                    base = t0 + (off0 + k) - s
                    term = wb_ref[k:k + 1, ls] * buf_ref[base:base + rows, ls]
                    part = term if part is None else part + term
                acc = acc + part[s:s + CONV_ROWS, :]
            u_ref[t0:t0 + CONV_ROWS, ls] = acc
    u = u_ref[...]
    mu = jnp.mean(u, axis=-1, keepdims=True)
    uc = u - mu
    y = uc * lax.rsqrt(jnp.mean(uc * uc, axis=-1, keepdims=True) + NORM_EPS) * lng_ref[...] + lnb_ref[...]
    ob_ref[...] = (y * jax.nn.sigmoid(y)).astype(ob_ref.dtype)

    buf2_ref[0:CC_HALO, :] = f32(hcc_ref) * f32(hcx_ref) * keep
    buf2_ref[CC_HALO:, :] = f32(cc_ref) * f32(cx_ref)
    conv = jnp.zeros((ts, CC_WIDTH), F32)
    for k in range(CC_CONV):
        base = CC_HALO - (CC_CONV - 1) + k
        conv = conv + wc_ref[k:k + 1, :] * buf2_ref[base:base + ts, :]
    oc_ref[...] = (f32(cb_ref) * conv).astype(oc_ref.dtype)


def _conv_mixers(proj, wb, bb, lng, lnb, wc, seq, ts):
    t = proj.shape[0]
    w = CB_WIDTH
    c0 = (OFF_GLU - SPLIT_COL) // w
    c1 = (OFF_CC - SPLIT_COL) // w
    cur = lambda col: pl.BlockSpec((ts, w), lambda i: (i, col))
    halo = lambda rows, col: pl.BlockSpec((rows, w), lambda i: (jnp.maximum(i * (ts // rows) - 1, 0), col))
    full = lambda r: pl.BlockSpec((r, w), lambda i: (0, 0))
    wb_p = jnp.pad(wb, ((0, CB_HALO - CB_CONV), (0, 0)))
    wc_p = jnp.pad(wc, ((0, SUBLANES - CC_CONV), (0, 0)))
    return pl.pallas_call(
        functools.partial(_conv_kernel, ts=ts, seq=seq),
        grid=(t // ts,),
        in_specs=[cur(c0), cur(c0 + 1), halo(CB_HALO, c0), halo(CB_HALO, c0 + 1),
                  cur(c1), cur(c1 + 1), cur(c1 + 2), halo(CC_HALO, c1 + 1), halo(CC_HALO, c1 + 2),
                  full(CB_HALO), full(1), full(1), full(1), full(SUBLANES)],
        out_specs=[pl.BlockSpec((ts, w), lambda i: (i, 0))] * 2,
        out_shape=[jax.ShapeDtypeStruct((t, w), BF16)] * 2,
        scratch_shapes=[pltpu.VMEM((CB_HALO + ts, w), F32), pltpu.VMEM((ts, w), F32),
                        pltpu.VMEM((CC_HALO + ts, w), F32)],
        compiler_params=_cparams(("parallel",)),
        name="conv_mixers",
    )(proj, proj, proj, proj, proj, proj, proj, proj, proj,
      wb_p, bb.reshape(1, w), lng.reshape(1, w), lnb.reshape(1, w), wc_p)


def _merge_kernel(x_ref, oa_ref, ob_ref, oc_ref, od_ref, g0_ref, g1_ref, g2_ref, g3_ref, bg_ref,
                  wa_ref, wb_ref, wc_ref, wd_ref, wm_ref, o_ref):
    merged = None
    branches = ((oa_ref, wa_ref, g0_ref), (ob_ref, wb_ref, g1_ref), (oc_ref, wc_ref, g2_ref), (od_ref, wd_ref, g3_ref))
    for n, (b_ref, w_ref, g_ref) in enumerate(branches):
        y = jnp.dot(b_ref[...], w_ref[...], preferred_element_type=F32)
        term = jax.nn.sigmoid(g_ref[...].astype(F32) + bg_ref[n:n + 1, :]) * y
        merged = term if merged is None else merged + term
    o_ref[...] = x_ref[...] + jnp.dot(merged.astype(BF16), wm_ref[...], preferred_element_type=F32)


def _merge(x2d, oa, ob, oc, od, proj, b_gate, w_oa, w_ob, w_oc, w_od, w_merge, tm):
    t, d = x2d.shape
    g0 = (OFF_GATE - SPLIT_COL) // d
    row = lambda i: (i, 0)
    fixed = lambda i: (0, 0)
    br = pl.BlockSpec((tm, A_WIDTH), row)
    gate = lambda n: pl.BlockSpec((tm, d), lambda i: (i, g0 + n))
    wsp = pl.BlockSpec((A_WIDTH, d), fixed)
    bf = lambda w: w.astype(BF16)
    return pl.pallas_call(
        _merge_kernel,
        grid=(t // tm,),
        in_specs=[pl.BlockSpec((tm, d), row), br, br, br, br, gate(0), gate(1), gate(2), gate(3),
                  pl.BlockSpec((N_BRANCH, d), fixed), wsp, wsp, wsp, wsp, pl.BlockSpec((d, d), fixed)],
        out_specs=pl.BlockSpec((tm, d), row),
        out_shape=jax.ShapeDtypeStruct((t, d), F32),
        compiler_params=_cparams(("parallel",)),
        name="merge",
    )(x2d, oa, ob, oc, od, proj, proj, proj, proj, b_gate.reshape(N_BRANCH, d),
      bf(w_oa), bf(w_ob), bf(w_oc), bf(w_od), bf(w_merge))


FFN_HALO = 8


def _ffn_kernel(x_ref, hx_ref, g_ref, wg_ref, wu_ref, cw_ref, wd_ref, o_ref, h_ref, gp_ref, acc_ref, *, tm, seq):
    i = pl.program_id(0)
    f = pl.program_id(1)

    def norm(x):
        ms = jnp.mean(x * x, axis=-1, keepdims=True)
        return (x * lax.rsqrt(ms + NORM_EPS) * g_ref[...]).astype(BF16)

    @pl.when(f == 0)
    def _():
        h_ref[0:FFN_HALO, :] = norm(hx_ref[...])
        h_ref[FFN_HALO:, :] = norm(x_ref[...])
        acc_ref[...] = x_ref[...]

    keep = jnp.where((i * tm) % seq == 0, 0.0, 1.0)
    gp_ref[0:FFN_HALO, :] = jnp.dot(h_ref[0:FFN_HALO, :], wg_ref[...], preferred_element_type=F32) * keep
    gp_ref[FFN_HALO:, :] = jnp.dot(h_ref[FFN_HALO:, :], wg_ref[...], preferred_element_type=F32)
    gt = jnp.zeros((tm, gp_ref.shape[1]), F32)
    for k in range(FFN_CONV):
        base = FFN_HALO - (FFN_CONV - 1) + k
        gt = gt + cw_ref[k:k + 1, :] * gp_ref[base:base + tm, :]
    up = jnp.dot(h_ref[FFN_HALO:, :], wu_ref[...], preferred_element_type=F32)
    act = (gt * jax.nn.sigmoid(gt) * up).astype(BF16)
    acc_ref[...] += jnp.dot(act, wd_ref[...], preferred_element_type=F32)

    @pl.when(f == pl.num_programs(1) - 1)
    def _():
        o_ref[...] = acc_ref[...]


def _ffn(x2d, g, wg, wu, cw, wd, seq, tm, tf):
    t, d = x2d.shape
    ff = wg.shape[1]
    cw_p = jnp.pad(cw, ((0, SUBLANES - FFN_CONV), (0, 0)))
    return pl.pallas_call(
        functools.partial(_ffn_kernel, tm=tm, seq=seq),
        grid=(t // tm, ff // tf),
        in_specs=[
            pl.BlockSpec((tm, d), lambda i, f: (i, 0)),
            pl.BlockSpec((FFN_HALO, d), lambda i, f: (jnp.maximum(i * (tm // FFN_HALO) - 1, 0), 0)),
            pl.BlockSpec((1, d), lambda i, f: (0, 0)),
            pl.BlockSpec((d, tf), lambda i, f: (0, f)),
            pl.BlockSpec((d, tf), lambda i, f: (0, f)),
            pl.BlockSpec((SUBLANES, tf), lambda i, f: (0, f)),
            pl.BlockSpec((tf, d), lambda i, f: (f, 0)),
        ],
        out_specs=pl.BlockSpec((tm, d), lambda i, f: (i, 0)),
        out_shape=jax.ShapeDtypeStruct((t, d), F32),
        scratch_shapes=[pltpu.VMEM((FFN_HALO + tm, d), BF16), pltpu.VMEM((FFN_HALO + tm, tf), F32),
                        pltpu.VMEM((tm, d), F32)],
        compiler_params=_cparams(("parallel", "arbitrary")),
        name="ffn",
    )(x2d, x2d, g.reshape(1, d), wg.astype(BF16), wu.astype(BF16), cw_p, wd.astype(BF16))


def _tiles(seq):
    big = min(seq, 512)
    return dict(
        inproj_tm=min(seq, 1024), inproj_tn=1024,
        conv_ts=big, merge_tm=big, ffn_tm=big, ffn_tf=D_FF // 2,
        dsa_tq=min(seq, 512),
    )


def kernel(x, positions, g_mix, w_in, b_gate, g_qa, g_ka, g_kidx, w_oa, cb_conv_w, cb_conv_b, cb_ln_g, cb_ln_b,
           w_ob, cc_conv_w, w_oc, w_od, w_merge, g_ffn, w_ffn_gate, w_ffn_up, ffn_conv_w, w_ffn_down):
    bsz, seq, dm = x.shape
    depth = w_in.shape[0]
    assert dm == D_MODEL and seq % min(CHUNK, seq) == 0 and seq % LANES == 0
    tl = _tiles(seq)
    t = bsz * seq

    cos_a, sin_a = _rope_tables(positions, A_HEAD_DIM)
    cos_i, sin_i = _rope_tables(positions, IDX_DIM)
    ca, sa = _rope_lane_tables(cos_a, sin_a, A_HEAD_DIM)
    ci, si = _rope_lane_tables(cos_i, sin_i, IDX_DIM)

    x2d = x.reshape(t, dm)
    for l in range(depth):
        w_packed = _pack_w_in(jnp.swapaxes(w_in[l], 0, 1))
        tm, tn = tl["inproj_tm"], tl["inproj_tn"]
        proj_a, proj = _inproj(x2d, g_mix[l], w_packed, tm, tn)
        qT, qiT, wT, ki, ka, vT, sq, skT, sv = _prep(proj_a, proj, ca, sa, ci, si, g_qa[l], g_ka[l], g_kidx[l],
                                                     bsz, seq)
        o_a = _dsa_attention(qT, qiT, wT, ki, ka, vT, bsz, seq, tl["dsa_tq"])
        o_d = _sb_attention(sq, skT, sv, bsz, seq)
        o_b, o_c = _conv_mixers(proj, cb_conv_w[l], cb_conv_b[l], cb_ln_g[l], cb_ln_b[l], cc_conv_w[l],
                                seq, tl["conv_ts"])
        x2d = _merge(x2d, o_a, o_b, o_c, o_d, proj, b_gate[l], w_oa[l], w_ob[l], w_oc[l], w_od[l], w_merge[l],
                     tl["merge_tm"])
        x2d = _ffn(x2d, g_ffn[l], w_ffn_gate[l], w_ffn_up[l], ffn_conv_w[l], w_ffn_down[l], seq,
                   tl["ffn_tm"], tl["ffn_tf"])
    return x2d.reshape(bsz, seq, dm)
```

```python
import functools
import math

import jax
import jax.numpy as jnp
from jax import lax
from jax.experimental import pallas as pl
from jax.experimental.pallas import tpu as pltpu

F32 = jnp.float32
BF16 = jnp.bfloat16
I32 = jnp.int32

D_MODEL = 1024
N_BRANCH = 4
A_HEADS = 4
A_HEAD_DIM = 128
A_WIDTH = A_HEADS * A_HEAD_DIM
IDX_HEADS = 4
IDX_DIM = 64
TOPK_MAX = 256
CB_WIDTH = 512
CB_CONV = 31
CC_WIDTH = 512
CC_CONV = 3
SB_HEADS = 4
SB_HEAD_DIM = 128
SB_WIDTH = SB_HEADS * SB_HEAD_DIM
D_FF = 2816
FFN_CONV = 3
ROPE_THETA = 500000.0
ROT_FRACTION_DIV = 4
NORM_EPS = 1e-6

LANES = 128
SUBLANES = 8
VMEM_LIMIT = 56 * 1024 * 1024

LOG2E = math.log2(math.e)
NEG_BIG = -1e30
M_INIT = -1e20
F32_LOWEST = float(jnp.finfo(jnp.float32).min)
KEY_MIN_NORMAL = 0x00800000
TRI_PAD = 16
COUNT_VPU_SHARE = 4
VALUE_PASSES = 12
SEARCH_PASS_CAP = VALUE_PASSES + 36

OFF_AQ = 0
OFF_AK = 512
OFF_AV = 1024
OFF_IQ = 1536
OFF_IK = 1792
OFF_IW = 1920
OFF_GLU = 2048
OFF_CC = 3072
OFF_SB = 4608
OFF_GATE = 6144
N_PACK = 10240
SPLIT_COL = 2048
CHUNK = 512


def _cparams(sem):
    return pltpu.CompilerParams(dimension_semantics=sem, vmem_limit_bytes=VMEM_LIMIT)


def _trig_kernel(pos_ref, invf_ref, cos_ref, sin_ref):
    ang = pos_ref[...] * invf_ref[...]
    cos_ref[...] = jnp.cos(ang)
    sin_ref[...] = jnp.sin(ang)


def _rope_tables(positions, head_dim):
    rot = head_dim // ROT_FRACTION_DIV
    half = rot // 2
    inv_freq = ROPE_THETA ** (-(jnp.arange(half, dtype=F32) * 2.0) / rot)
    t = positions.size
    rows = t * half // LANES
    pos_rep = jnp.broadcast_to(positions.astype(F32).reshape(t, 1), (t, half)).reshape(rows, LANES)
    invf = jnp.tile(inv_freq, LANES // half).reshape(1, LANES)
    tr = min(rows, 512)
    cos, sin = pl.pallas_call(
        _trig_kernel,
        grid=(rows // tr,),
        in_specs=[pl.BlockSpec((tr, LANES), lambda i: (i, 0)), pl.BlockSpec((1, LANES), lambda i: (0, 0))],
        out_specs=[pl.BlockSpec((tr, LANES), lambda i: (i, 0))] * 2,
        out_shape=[jax.ShapeDtypeStruct((rows, LANES), F32)] * 2,
        compiler_params=_cparams(("parallel",)),
        name="rope_trig",
    )(pos_rep, invf)
    return cos.reshape(t, half), sin.reshape(t, half)


def _rope_lane_tables(cos, sin, head_dim):
    t, half = cos.shape
    rest = head_dim - 2 * half
    c = jnp.concatenate([cos, cos, jnp.ones((t, rest), F32)], axis=1)
    s = jnp.concatenate([sin, sin, jnp.zeros((t, rest), F32)], axis=1)
    reps = LANES // head_dim
    return jnp.tile(c, (1, reps)), jnp.tile(s, (1, reps))


def _inproj_kernel(x_ref, g_ref, w_ref, oa_ref, ob_ref, h_ref, *, n_a):
    j = pl.program_id(1)

    @pl.when(j == 0)
    def _():
        x = x_ref[...]
        ms = jnp.mean(x * x, axis=-1, keepdims=True)
        h_ref[...] = (x * lax.rsqrt(ms + NORM_EPS) * g_ref[...]).astype(BF16)

    y = jnp.dot(h_ref[...], w_ref[...], preferred_element_type=F32)

    @pl.when(j < n_a)
    def _():
        oa_ref[...] = y

    @pl.when(j >= n_a)
    def _():
        ob_ref[...] = y.astype(ob_ref.dtype)


def _inproj(x2d, g, w_packed, tm, tn):
    t, d = x2d.shape
    n = w_packed.shape[1]
    n_a = SPLIT_COL // tn
    return pl.pallas_call(
        functools.partial(_inproj_kernel, n_a=n_a),
        grid=(t // tm, n // tn),
        in_specs=[
            pl.BlockSpec((tm, d), lambda i, j: (i, 0)),
            pl.BlockSpec((1, d), lambda i, j: (0, 0)),
            pl.BlockSpec((d, tn), lambda i, j: (0, j)),
        ],
        out_specs=[pl.BlockSpec((tm, tn), lambda i, j: (i, jnp.minimum(j, n_a - 1))),
                   pl.BlockSpec((tm, tn), lambda i, j: (i, jnp.maximum(j - n_a, 0)))],
        out_shape=[jax.ShapeDtypeStruct((t, SPLIT_COL), F32), jax.ShapeDtypeStruct((t, n - SPLIT_COL), BF16)],
        scratch_shapes=[pltpu.VMEM((tm, d), BF16)],
        compiler_params=_cparams(("parallel", "arbitrary")),
        name="inproj",
    )(x2d, g.reshape(1, d), w_packed)


N_IN = 10052
OFF_IW_SRC = 1856
OFF_REST_SRC = 1860
PACK_FIRST = OFF_REST_SRC // LANES
PACK_ROT = (OFF_GLU - OFF_REST_SRC) % LANES


def _pack_kernel(a_ref, b_ref, o_ref):
    n = pl.program_id(0)
    row = lax.broadcasted_iota(I32, a_ref.shape, 0)
    a = a_ref[...]

    @pl.when(n < OFF_IK // LANES)
    def _():
        o_ref[...] = a.T.astype(BF16)

    @pl.when(n == OFF_IK // LANES)
    def _():
        o_ref[...] = jnp.where(row < IDX_DIM, a, 0.0).T.astype(BF16)

    @pl.when(n == OFF_IW // LANES)
    def _():
        iw = pltpu.roll(a, LANES - (OFF_IW_SRC - OFF_IK), 0)
        o_ref[...] = jnp.where(row < IDX_HEADS, iw, 0.0).T.astype(BF16)

    @pl.when(n >= OFF_GLU // LANES)
    def _():
        out = jnp.where(row < PACK_ROT, pltpu.roll(a, PACK_ROT, 0), pltpu.roll(b_ref[...], PACK_ROT, 0))
        o_ref[...] = out.T.astype(BF16)


def _pack_w_in(w_t):
    d = w_t.shape[1]
    first_rest = OFF_GLU // LANES

    def a_idx(n):
        return (jnp.where(n < first_rest, jnp.minimum(n, PACK_FIRST), n - first_rest + PACK_FIRST), 0)

    def b_idx(n):
        return (jnp.where(n < first_rest, PACK_FIRST, n - first_rest + PACK_FIRST + 1), 0)

    return pl.pallas_call(
        _pack_kernel,
        grid=(N_PACK // LANES,),
        in_specs=[pl.BlockSpec((LANES, d), a_idx), pl.BlockSpec((LANES, d), b_idx)],
        out_specs=pl.BlockSpec((d, LANES), lambda n: (0, n)),
        out_shape=jax.ShapeDtypeStruct((d, N_PACK), BF16),
        compiler_params=_cparams(("parallel",)),
        name="pack_w_in",
    )(w_t, w_t)


def _rotate_half(x, half, period):
    lane = lax.broadcasted_iota(I32, x.shape, 1)
    up = pltpu.roll(x, LANES - half, 1)
    dn = pltpu.roll(x, half, 1)
    return jnp.where((lane % period) < half, -up, dn)


def _prep_kernel(p_ref, sqi_ref, ski_ref, svi_ref, ca_ref, sa_ref, ci_ref, si_ref, gq_ref, gk_ref, gi_ref,
                 qT_ref, qiT_ref, wT_ref, ki_ref, ka_ref, vT_ref, sq_ref, skT_ref, sv_ref):
    ca, sa, ci, si = ca_ref[...], sa_ref[...], ci_ref[...], si_ref[...]
    a_half = A_HEAD_DIM // ROT_FRACTION_DIV // 2
    i_half = IDX_DIM // ROT_FRACTION_DIV // 2

    def head_norm(x, g, n_real):
        ms = jnp.sum(x * x, axis=-1, keepdims=True) * (1.0 / n_real)
        return x * lax.rsqrt(ms + NORM_EPS) * g

    for h in range(A_HEADS):
        sl = slice(h * A_HEAD_DIM, (h + 1) * A_HEAD_DIM)
        col = lambda off: slice(off + h * A_HEAD_DIM, off + (h + 1) * A_HEAD_DIM)
        q = head_norm(p_ref[:, col(OFF_AQ)], gq_ref[...], A_HEAD_DIM)
        q = q * ca + _rotate_half(q, a_half, A_HEAD_DIM) * sa
        qT_ref[sl, :] = (q * (A_HEAD_DIM ** -0.5 * LOG2E)).T.astype(BF16)
        k = head_norm(p_ref[:, col(OFF_AK)], gk_ref[...], A_HEAD_DIM)
        ka_ref[:, sl] = (k * ca + _rotate_half(k, a_half, A_HEAD_DIM) * sa).astype(BF16)
        vT_ref[sl, :] = p_ref[:, col(OFF_AV)].T.astype(BF16)

    row = lax.broadcasted_iota(I32, (LANES, p_ref.shape[0]), 0)
    for j in range(IDX_HEADS * IDX_DIM // LANES):
        qi = p_ref[:, OFF_IQ + j * LANES:OFF_IQ + (j + 1) * LANES]
        qt = (qi * ci + _rotate_half(qi, i_half, IDX_DIM) * si).T
        qiT_ref[(2 * j) * LANES:(2 * j + 1) * LANES, :] = jnp.where(row < IDX_DIM, qt, 0.0).astype(BF16)
        qiT_ref[(2 * j + 1) * LANES:(2 * j + 2) * LANES, :] = jnp.where(row >= IDX_DIM, qt, 0.0).astype(BF16)
    ki = head_norm(p_ref[:, OFF_IK:OFF_IK + LANES], gi_ref[...], IDX_DIM)
    ki = ki * ci + _rotate_half(ki, i_half, IDX_DIM) * si
    ki_ref[...] = (ki + pltpu.roll(ki, IDX_DIM, 1)).astype(BF16)
    w = p_ref[:, OFF_IW:OFF_IW + LANES] * (IDX_DIM ** -0.5 * IDX_HEADS ** -0.5)
    wT_ref[...] = w.T[0:SUBLANES, :]

    sq_ref[...] = (sqi_ref[...].astype(F32) * (SB_HEAD_DIM ** -0.5 * LOG2E)).astype(BF16)
    for h in range(SB_HEADS):
        k_t = ski_ref[:, h * SB_HEAD_DIM:(h + 1) * SB_HEAD_DIM].astype(F32).T.astype(BF16)
        sb_ck = skT_ref.shape[-1]
        for c in range(skT_ref.shape[1]):
            skT_ref[h, c] = k_t[:, c * sb_ck:(c + 1) * sb_ck]
    sv_ref[...] = svi_ref[...]


def _prep(proj_a, proj_b, ca, sa, ci, si, g_qa, g_ka, g_kidx, bsz, seq):
    t = proj_a.shape[0]
    ck = min(CHUNK, seq)
    sb_ck = min(SB_CK, seq)
    nck = seq // ck
    gi = jnp.concatenate([g_kidx, jnp.ones((LANES - IDX_DIM,), F32)]).reshape(1, LANES)
    tok = lambda b, i: (b * nck + i, 0)
    fixed = lambda b, i: (0, 0)
    tbl = pl.BlockSpec((ck, LANES), tok)
    gsp = pl.BlockSpec((1, LANES), fixed)
    sb_col = (OFF_SB - SPLIT_COL) // SB_WIDTH
    sb_in = lambda n: pl.BlockSpec((ck, SB_WIDTH), lambda b, i: (b * nck + i, sb_col + n))
    return pl.pallas_call(
        _prep_kernel,
        grid=(bsz, nck),
        in_specs=[pl.BlockSpec((ck, SPLIT_COL), tok), sb_in(0), sb_in(1), sb_in(2),
                  tbl, tbl, tbl, tbl, gsp, gsp, gsp],
        out_specs=[
            pl.BlockSpec((None, A_WIDTH, ck), lambda b, i: (b, 0, i)),
            pl.BlockSpec((None, IDX_HEADS * LANES, ck), lambda b, i: (b, 0, i)),
            pl.BlockSpec((None, SUBLANES, ck), lambda b, i: (b, 0, i)),
            pl.BlockSpec((None, ck, LANES), lambda b, i: (b, i, 0)),
            pl.BlockSpec((None, ck, A_WIDTH), lambda b, i: (b, i, 0)),
            pl.BlockSpec((None, None, A_WIDTH, ck), lambda b, i: (b, i, 0, 0)),
            pl.BlockSpec((ck, SB_WIDTH), tok),
            pl.BlockSpec((None, SB_HEADS, ck // sb_ck, SB_HEAD_DIM, sb_ck), lambda b, i: (b, 0, i, 0, 0)),
            pl.BlockSpec((None, ck, SB_WIDTH), lambda b, i: (b, i, 0)),
        ],
        out_shape=[
            jax.ShapeDtypeStruct((bsz, A_WIDTH, seq), BF16),
            jax.ShapeDtypeStruct((bsz, IDX_HEADS * LANES, seq), BF16),
            jax.ShapeDtypeStruct((bsz, SUBLANES, seq), F32),
            jax.ShapeDtypeStruct((bsz, seq, LANES), BF16),
            jax.ShapeDtypeStruct((bsz, seq, A_WIDTH), BF16),
            jax.ShapeDtypeStruct((bsz, nck, A_WIDTH, ck), BF16),
            jax.ShapeDtypeStruct((t, SB_WIDTH), BF16),
            jax.ShapeDtypeStruct((bsz, SB_HEADS, seq // sb_ck, SB_HEAD_DIM, sb_ck), BF16),
            jax.ShapeDtypeStruct((bsz, seq, SB_WIDTH), BF16),
        ],
        compiler_params=_cparams(("parallel", "parallel")),
        name="mixer_prep",
    )(proj_a, proj_b, proj_b, proj_b, ca, sa, ci, si, g_qa.reshape(1, LANES), g_ka.reshape(1, LANES), gi)


def _order_key(f):
    b = pltpu.bitcast(f, I32)
    return b ^ ((b >> 31) & 0x7FFFFFFF)


def _key_float(k):
    return pltpu.bitcast(k ^ ((k >> 31) & 0x7FFFFFFF), F32)


def _normal_key(k):
    k = jnp.where(jnp.logical_and(k > 0, k < KEY_MIN_NORMAL), KEY_MIN_NORMAL, k)
    return jnp.where(jnp.logical_and(k < 0, k >= -KEY_MIN_NORMAL), 0, k)


def _dsa_kernel(qT_ref, qiT_ref, wT_ref, ki_ref, ka_ref, vT_ref, tri_ref, o_ref, sc_ref, red_ref, cnt_ref, m_ref,
                acc_ref, *, tq, ck, k_sel):
    i = pl.program_id(1)
    n_keys = (i + 1) * tq
    n_chunks = (n_keys + ck - 1) // ck
    q_pos = i * tq + lax.broadcasted_iota(I32, (1, tq), 1)

    def chunk_start(c):
        return pl.multiple_of(c * ck, ck)

    def key_pos(c):
        return c * ck + lax.broadcasted_iota(I32, (ck, tq), 0)

    def score_chunk(c, diag):
        ks = chunk_start(c)
        kic = ki_ref[pl.ds(ks, ck), :]
        sc = None
        for h in range(IDX_HEADS):
            d = jnp.dot(kic, qiT_ref[h * LANES:(h + 1) * LANES, :], preferred_element_type=F32)
            term = jnp.maximum(d, 0.0) * wT_ref[h:h + 1, :]
            sc = term if sc is None else sc + term
        if diag:
            causal = key_pos(c) <= q_pos
            sc_hi = jnp.where(causal, sc, -jnp.inf)
            sc_lo = jnp.where(causal, sc, jnp.inf)
        else:
            sc_hi = sc_lo = sc
        sc_ref[pl.ds(ks, ck), :] = sc_hi
        fold = lambda a, op: op(a.reshape(ck // SUBLANES, SUBLANES, tq), axis=0)
        return fold(sc_hi, jnp.max), fold(sc_lo, jnp.min)

    def score_body(c, carry):
        mx, mn = score_chunk(c, False)
        return jnp.maximum(carry[0], mx), jnp.minimum(carry[1], mn)

    init = (jnp.full((SUBLANES, tq), -jnp.inf, F32), jnp.full((SUBLANES, tq), jnp.inf, F32))
    mx, mn = lax.fori_loop(0, n_chunks - 1, score_body, init)
    mx_d, mn_d = score_chunk(n_chunks - 1, True)
    def fold_rows(x, op):
        red_ref[...] = x
        r = red_ref[0:1, :]
        for j in range(1, SUBLANES):
            r = op(r, red_ref[j:j + 1, :])
        return r

    row_max = fold_rows(jnp.maximum(mx, mx_d), jnp.maximum)
    row_min = fold_rows(jnp.minimum(mn, mn_d), jnp.minimum)

    mxu_rows = ck - ck // COUNT_VPU_SHARE
    ones = jnp.ones((SUBLANES, mxu_rows), BF16)

    def count_ge(t):
        for nc in range(1, sc_ref.shape[0] // ck + 1):
            @pl.when(n_chunks == nc)
            def _(nc=nc):
                on_mxu = None
                on_vpu = jnp.zeros((SUBLANES, tq), F32)
                for c in range(nc):
                    ind = jnp.where(sc_ref[c * ck:c * ck + mxu_rows, :] >= t, 1.0, 0.0).astype(BF16)
                    part = jnp.dot(ones, ind, preferred_element_type=F32)
                    on_mxu = part if on_mxu is None else on_mxu + part
                    tail = jnp.where(sc_ref[c * ck + mxu_rows:(c + 1) * ck, :] >= t, 1.0, 0.0)
                    on_vpu = on_vpu + tail.reshape(-1, SUBLANES, tq).sum(axis=0)
                cnt_ref[0:SUBLANES, :] = on_vpu
                cnt_ref[SUBLANES:, :] = on_mxu
        total = cnt_ref[SUBLANES:SUBLANES + 1, :]
        for j in range(SUBLANES):
            total = total + cnt_ref[j:j + 1, :]
        return total.astype(I32)

    n_causal = q_pos + 1
    take_all = n_causal <= k_sel
    lo0 = _normal_key(_order_key(row_min))
    hi0 = _normal_key(_order_key(row_max)) + 1
    thr0 = jnp.full((1, tq), F32_LOWEST, F32)
    done0 = jnp.where(take_all, 1, 0)
    zero = jnp.zeros((1, tq), I32)

    def search_cond(st):
        it, _, _, _, _, done, _ = st
        return jnp.logical_and(it < SEARCH_PASS_CAP, jnp.min(done) == 0)

    def search_body(st):
        it, lo, hi, c_hi, thr, done, tied = st
        cross = jnp.logical_and(lo < 0, hi > 0)
        gap = hi - lo
        zero_probe = jnp.logical_and(it == 1, lo == 0)
        mid = _normal_key(jnp.where(cross, 0, jnp.where(zero_probe, KEY_MIN_NORMAL, lo + (gap >> 1))))
        v_mid = _normal_key(_order_key(0.5 * _key_float(lo) + 0.5 * _key_float(hi)))
        by_value = jnp.logical_and(jnp.logical_and(it < VALUE_PASSES, jnp.logical_not(jnp.logical_or(cross, zero_probe))),
                                   jnp.logical_and(v_mid > lo, v_mid < hi))
        mid = jnp.where(by_value, v_mid, mid)
        no_pivot = jnp.logical_or(gap == 1, jnp.logical_or(mid >= hi, mid <= lo))
        adjacent = jnp.logical_and(no_pivot, jnp.logical_not(cross))
        pivot = _key_float(mid)
        cnt = count_ge(pivot)
        live = done == 0
        hit = jnp.logical_and(live, jnp.logical_and(cnt == k_sel, jnp.logical_not(adjacent)))
        end = jnp.logical_and(live, adjacent)
        move = jnp.logical_and(live, jnp.logical_not(jnp.logical_or(hit, end)))
        up = jnp.logical_and(move, cnt > k_sel)
        dn = jnp.logical_and(move, cnt < k_sel)
        thr = jnp.where(hit, pivot, jnp.where(end, _key_float(lo), thr))
        tied = jnp.where(end, 1, tied)
        done = jnp.where(jnp.logical_or(hit, end), 1, done)
        lo = jnp.where(up, mid, lo)
        hi = jnp.where(dn, mid, hi)
        c_hi = jnp.where(dn, cnt, c_hi)
        return it + 1, lo, hi, c_hi, thr, done, tied

    st = lax.while_loop(search_cond, search_body, (jnp.int32(0), lo0, hi0, zero, thr0, done0, zero))
    _, _, _, c_hi, thr, _, tied = st

    room = jnp.where(tied > 0, k_sel - c_hi, jnp.int32(2 ** 30)).astype(F32)

    @pl.when(jnp.max(tied) > 0)
    def _():
        red_ref[0:1, :] = jnp.zeros((1, tq), F32)

        def drop_excess(c, carry):
            ks = chunk_start(c)
            s = sc_ref[pl.ds(ks, ck), :]
            is_tied = s == thr
            ind = jnp.where(is_tied, 1.0, 0.0).astype(BF16)
            pre = jnp.dot(tri_ref[...], ind, preferred_element_type=F32)
            before = red_ref[0:1, :]
            rank = pre[0:ck, :] + before
            sc_ref[pl.ds(ks, ck), :] = jnp.where(is_tied, jnp.where(rank >= room, -jnp.inf, s), s)
            red_ref[0:1, :] = before + pre[ck:ck + 1, :]
            return carry
        lax.fori_loop(0, n_chunks, drop_excess, 0)

    m_ref[...] = jnp.full(m_ref.shape, M_INIT, F32)
    acc_ref[...] = jnp.zeros(acc_ref.shape, F32)
    heads = [slice(h * A_HEAD_DIM, (h + 1) * A_HEAD_DIM) for h in range(A_HEADS)]
    ones_rows = jnp.ones((TRI_PAD, ck), BF16)

    def att_body(c, carry):
        ks = chunk_start(c)
        sel = sc_ref[pl.ds(ks, ck), :] >= thr
        logits = [jnp.dot(ka_ref[pl.ds(ks, ck), hs], qT_ref[hs, :], preferred_element_type=F32) for hs in heads]
        for h, hs in enumerate(heads):
            m_old = m_ref[h, 0:1, :]
            m_new = jnp.maximum(m_old, jnp.max(jnp.where(sel, logits[h], NEG_BIG), axis=0, keepdims=True))
            alpha = jnp.exp2(m_old - m_new)
            p = jnp.where(sel, jnp.exp2(logits[h] - m_new), 0.0).astype(BF16)
            m_ref[h, 0:1, :] = m_new
            lhs = jnp.concatenate([vT_ref[c, hs, :], ones_rows], axis=0)
            acc_ref[h] = acc_ref[h] * alpha + jnp.dot(lhs, p, preferred_element_type=F32)
        return carry

    lax.fori_loop(0, n_chunks, att_body, 0)
    for h, hs in enumerate(heads):
        denom = acc_ref[h, A_HEAD_DIM:A_HEAD_DIM + 1, :]
        o_ref[:, hs] = (acc_ref[h, 0:A_HEAD_DIM, :] / denom).T.astype(o_ref.dtype)


def _dsa_attention(qT, qiT, wT, ki, ka, vT, bsz, seq, tq):
    ck = min(CHUNK, seq)
    nck = seq // ck
    k_sel = min(TOPK_MAX, seq // 4)
    kern = functools.partial(_dsa_kernel, tq=tq, ck=ck, k_sel=k_sel)
    once = pl.Buffered(1)
    out = pl.pallas_call(
        kern,
        grid=(bsz, seq // tq),
        in_specs=[
            pl.BlockSpec((None, A_WIDTH, tq), lambda b, i: (b, 0, i)),
            pl.BlockSpec((None, IDX_HEADS * LANES, tq), lambda b, i: (b, 0, i)),
            pl.BlockSpec((None, SUBLANES, tq), lambda b, i: (b, 0, i)),
            pl.BlockSpec((None, seq, LANES), lambda b, i: (b, 0, 0), pipeline_mode=once),
            pl.BlockSpec((None, seq, A_WIDTH), lambda b, i: (b, 0, 0), pipeline_mode=once),
            pl.BlockSpec((None, nck, A_WIDTH, ck), lambda b, i: (b, 0, 0, 0), pipeline_mode=once),
            pl.BlockSpec((ck + TRI_PAD, ck), lambda b, i: (0, 0), pipeline_mode=once),
        ],
        out_specs=pl.BlockSpec((None, tq, A_WIDTH), lambda b, i: (b, i, 0)),
        out_shape=jax.ShapeDtypeStruct((bsz, seq, A_WIDTH), BF16),
        scratch_shapes=[pltpu.VMEM((seq, tq), F32), pltpu.VMEM((SUBLANES, tq), F32),
                        pltpu.VMEM((2 * SUBLANES, tq), F32),
                        pltpu.VMEM((A_HEADS, SUBLANES, tq), F32),
                        pltpu.VMEM((A_HEADS, A_HEAD_DIM + TRI_PAD, tq), F32)],
        compiler_params=_cparams(("parallel", "arbitrary")),
        name="dsa_attention",
    )(qT, qiT, wT, ki, ka, vT, _prefix_matrix(ck))
    return out.reshape(bsz * seq, A_WIDTH)


def _prefix_matrix(n):
    r = jnp.arange(n + TRI_PAD)[:, None]
    c = jnp.arange(n)[None, :]
    return jnp.logical_or(c < r, r >= n).astype(BF16)


SB_CK = 256
SB_TQ = 1024
SB_STEP = 4


def _sb_kernel(q_ref, kT_ref, v_ref, u_ref, o_ref, acc_ref, carry_ref, *, tq, ck):
    i = pl.program_id(2)
    q = q_ref[...]
    acc_ref[...] = jnp.zeros(acc_ref.shape, F32)
    carry_ref[...] = jnp.zeros(carry_ref.shape, F32)
    per = tq // ck

    def chunk(c, r0=None):
        diag = r0 is not None
        r0 = r0 or 0
        rows = tq - r0
        z = jnp.dot(q[r0:, :], kT_ref[c], preferred_element_type=F32)
        neg_abs = pltpu.bitcast(pltpu.bitcast(z, I32) | jnp.int32(-2 ** 31), F32)
        neg_soft = jnp.log(1.0 + jnp.exp2(neg_abs)) * (-LOG2E)
        lk = neg_soft - jnp.maximum(z, 0.0)
        if diag:
            q_pos = i * tq + r0 + lax.broadcasted_iota(I32, (rows, ck), 0)
            mask = (c * ck + lax.broadcasted_iota(I32, (rows, ck), 1)) < q_pos
            lk = jnp.where(mask, lk, 0.0)
        cum = jnp.dot(lk.astype(BF16), u_ref[...], preferred_element_type=F32)
        a = jnp.exp2(z + cum)
        if diag:
            a = jnp.where(mask, a, 0.0)
        vs = pl.multiple_of(c * ck, ck)
        pv = jnp.dot(a.astype(BF16), v_ref[pl.ds(vs, ck), :], preferred_element_type=F32)
        return pv, cum[:, 0:1], r0

    def fold(parts):
        for pv, total, r0 in parts:
            carry = carry_ref[r0:, :]
            acc_ref[r0:, :] += jnp.exp2(carry) * pv
            carry_ref[r0:, :] = carry + jnp.broadcast_to(total, carry.shape)

    step = min(SB_STEP, per)
    for d0 in reversed(range(0, per, step)):
        fold([chunk(i * per + d0 + d, (d0 + d) * ck) for d in reversed(range(step))])

    def body(s, carry):
        first = i * per - 1 - s * step
        fold([chunk(first - d) for d in range(step)])
        return carry

    lax.fori_loop(0, i * (per // step), body, 0)
    o_ref[...] = acc_ref[...].astype(o_ref.dtype)


def _cumsum_matrix(n):
    j = jnp.arange(n)[:, None]
    s = jnp.arange(n)[None, :]
    return (j >= s).astype(BF16)


def _sb_attention(sq, skT, sv, bsz, seq):
    t = bsz * seq
    tq = min(SB_TQ, seq)
    ck = min(SB_CK, seq)
    nq = seq // tq
    return pl.pallas_call(
        functools.partial(_sb_kernel, tq=tq, ck=ck),
        grid=(bsz, SB_HEADS, nq),
        in_specs=[
            pl.BlockSpec((tq, SB_HEAD_DIM), lambda b, h, i: (b * nq + i, h)),
            pl.BlockSpec((None, None, seq // ck, SB_HEAD_DIM, ck), lambda b, h, i: (b, h, 0, 0, 0)),
            pl.BlockSpec((None, seq, SB_HEAD_DIM), lambda b, h, i: (b, 0, h)),
            pl.BlockSpec((ck, ck), lambda b, h, i: (0, 0)),
        ],
        out_specs=pl.BlockSpec((tq, SB_HEAD_DIM), lambda b, h, i: (b * nq + i, h)),
        out_shape=jax.ShapeDtypeStruct((t, SB_WIDTH), BF16),
        scratch_shapes=[pltpu.VMEM((tq, SB_HEAD_DIM), F32), pltpu.VMEM((tq, LANES), F32)],
        compiler_params=_cparams(("parallel", "parallel", "arbitrary")),
        name="stickbreak",
    )(sq, skT, sv, _cumsum_matrix(ck))


CB_HALO = 32
CC_HALO = 16
CONV_ROWS = 64


def _conv_kernel(ga_ref, gb_ref, hga_ref, hgb_ref, cb_ref, cc_ref, cx_ref, hcc_ref, hcx_ref,
                 wb_ref, bb_ref, lng_ref, lnb_ref, wc_ref, ob_ref, oc_ref, buf_ref, u_ref, buf2_ref, *, ts, seq):
    i = pl.program_id(0)
    keep = jnp.where((i * ts) % seq == 0, 0.0, 1.0)

    f32 = lambda ref: ref[...].astype(F32)
    buf_ref[0:CB_HALO, :] = f32(hga_ref) * jax.nn.sigmoid(f32(hgb_ref)) * keep
    buf_ref[CB_HALO:, :] = f32(ga_ref) * jax.nn.sigmoid(f32(gb_ref))
    off0 = CB_HALO - (CB_CONV - 1)
    for r in range(ts // CONV_ROWS):
        t0 = r * CONV_ROWS
        for g in range(CB_WIDTH // LANES):
            ls = slice(g * LANES, (g + 1) * LANES)
            acc = jnp.broadcast_to(bb_ref[:, ls], (CONV_ROWS, LANES))
            for s in range(SUBLANES):
                rows = CONV_ROWS + (SUBLANES if s else 0)
                part = None
                for k in range(CB_CONV):
                    if (off0 + k) % SUBLANES != s:
                        continue
                    base = t0 + (off0 + k) - s
                    term = wb_ref[k:k + 1, ls] * buf_ref[base:base + rows, ls]
                    part = term if part is None else part + term
                acc = acc + part[s:s + CONV_ROWS, :]
            u_ref[t0:t0 + CONV_ROWS, ls] = acc
    u = u_ref[...]
    mu = jnp.mean(u, axis=-1, keepdims=True)
    uc = u - mu
    y = uc * lax.rsqrt(jnp.mean(uc * uc, axis=-1, keepdims=True) + NORM_EPS) * lng_ref[...] + lnb_ref[...]
    ob_ref[...] = (y * jax.nn.sigmoid(y)).astype(ob_ref.dtype)

    buf2_ref[0:CC_HALO, :] = f32(hcc_ref) * f32(hcx_ref) * keep
    buf2_ref[CC_HALO:, :] = f32(cc_ref) * f32(cx_ref)
    conv = jnp.zeros((ts, CC_WIDTH), F32)
    for k in range(CC_CONV):
        base = CC_HALO - (CC_CONV - 1) + k
        conv = conv + wc_ref[k:k + 1, :] * buf2_ref[base:base + ts, :]
    oc_ref[...] = (f32(cb_ref) * conv).astype(oc_ref.dtype)


def _conv_mixers(proj, wb, bb, lng, lnb, wc, seq, ts):
    t = proj.shape[0]
    w = CB_WIDTH
    c0 = (OFF_GLU - SPLIT_COL) // w
    c1 = (OFF_CC - SPLIT_COL) // w
    cur = lambda col: pl.BlockSpec((ts, w), lambda i: (i, col))
    halo = lambda rows, col: pl.BlockSpec((rows, w), lambda i: (jnp.maximum(i * (ts // rows) - 1, 0), col))
    full = lambda r: pl.BlockSpec((r, w), lambda i: (0, 0))
    wb_p = jnp.pad(wb, ((0, CB_HALO - CB_CONV), (0, 0)))
    wc_p = jnp.pad(wc, ((0, SUBLANES - CC_CONV), (0, 0)))
    return pl.pallas_call(
        functools.partial(_conv_kernel, ts=ts, seq=seq),
        grid=(t // ts,),
        in_specs=[cur(c0), cur(c0 + 1), halo(CB_HALO, c0), halo(CB_HALO, c0 + 1),
                  cur(c1), cur(c1 + 1), cur(c1 + 2), halo(CC_HALO, c1 + 1), halo(CC_HALO, c1 + 2),
                  full(CB_HALO), full(1), full(1), full(1), full(SUBLANES)],
        out_specs=[pl.BlockSpec((ts, w), lambda i: (i, 0))] * 2,
        out_shape=[jax.ShapeDtypeStruct((t, w), BF16)] * 2,
        scratch_shapes=[pltpu.VMEM((CB_HALO + ts, w), F32), pltpu.VMEM((ts, w), F32),
                        pltpu.VMEM((CC_HALO + ts, w), F32)],
        compiler_params=_cparams(("parallel",)),
        name="conv_mixers",
    )(proj, proj, proj, proj, proj, proj, proj, proj, proj,
      wb_p, bb.reshape(1, w), lng.reshape(1, w), lnb.reshape(1, w), wc_p)


def _merge_kernel(x_ref, oa_ref, ob_ref, oc_ref, od_ref, g0_ref, g1_ref, g2_ref, g3_ref, bg_ref,
                  wa_ref, wb_ref, wc_ref, wd_ref, wm_ref, o_ref):
    merged = None
    branches = ((oa_ref, wa_ref, g0_ref), (ob_ref, wb_ref, g1_ref), (oc_ref, wc_ref, g2_ref), (od_ref, wd_ref, g3_ref))
    for n, (b_ref, w_ref, g_ref) in enumerate(branches):
        y = jnp.dot(b_ref[...], w_ref[...], preferred_element_type=F32)
        term = jax.nn.sigmoid(g_ref[...].astype(F32) + bg_ref[n:n + 1, :]) * y
        merged = term if merged is None else merged + term
    o_ref[...] = x_ref[...] + jnp.dot(merged.astype(BF16), wm_ref[...], preferred_element_type=F32)


def _merge(x2d, oa, ob, oc, od, proj, b_gate, w_oa, w_ob, w_oc, w_od, w_merge, tm):
    t, d = x2d.shape
    g0 = (OFF_GATE - SPLIT_COL) // d
    row = lambda i: (i, 0)
    fixed = lambda i: (0, 0)
    br = pl.BlockSpec((tm, A_WIDTH), row)
    gate = lambda n: pl.BlockSpec((tm, d), lambda i: (i, g0 + n))
    wsp = pl.BlockSpec((A_WIDTH, d), fixed)
    bf = lambda w: w.astype(BF16)
    return pl.pallas_call(
        _merge_kernel,
        grid=(t // tm,),
        in_specs=[pl.BlockSpec((tm, d), row), br, br, br, br, gate(0), gate(1), gate(2), gate(3),
                  pl.BlockSpec((N_BRANCH, d), fixed), wsp, wsp, wsp, wsp, pl.BlockSpec((d, d), fixed)],
        out_specs=pl.BlockSpec((tm, d), row),
        out_shape=jax.ShapeDtypeStruct((t, d), F32),
        compiler_params=_cparams(("parallel",)),
        name="merge",
    )(x2d, oa, ob, oc, od, proj, proj, proj, proj, b_gate.reshape(N_BRANCH, d),
      bf(w_oa), bf(w_ob), bf(w_oc), bf(w_od), bf(w_merge))


FFN_HALO = 8


def _ffn_kernel(x_ref, hx_ref, g_ref, wg_ref, wu_ref, cw_ref, wd_ref, o_ref, h_ref, gp_ref, acc_ref, *, tm, seq):
    i = pl.program_id(0)
    f = pl.program_id(1)

    def norm(x):
        ms = jnp.mean(x * x, axis=-1, keepdims=True)
        return (x * lax.rsqrt(ms + NORM_EPS) * g_ref[...]).astype(BF16)

    @pl.when(f == 0)
    def _():
        h_ref[0:FFN_HALO, :] = norm(hx_ref[...])
        h_ref[FFN_HALO:, :] = norm(x_ref[...])
        acc_ref[...] = x_ref[...]

    keep = jnp.where((i * tm) % seq == 0, 0.0, 1.0)
    gp_ref[0:FFN_HALO, :] = jnp.dot(h_ref[0:FFN_HALO, :], wg_ref[...], preferred_element_type=F32) * keep
    gp_ref[FFN_HALO:, :] = jnp.dot(h_ref[FFN_HALO:, :], wg_ref[...], preferred_element_type=F32)
    gt = jnp.zeros((tm, gp_ref.shape[1]), F32)
    for k in range(FFN_CONV):
        base = FFN_HALO - (FFN_CONV - 1) + k
        gt = gt + cw_ref[k:k + 1, :] * gp_ref[base:base + tm, :]
    up = jnp.dot(h_ref[FFN_HALO:, :], wu_ref[...], preferred_element_type=F32)
    act = (gt * jax.nn.sigmoid(gt) * up).astype(BF16)
    acc_ref[...] += jnp.dot(act, wd_ref[...], preferred_element_type=F32)

    @pl.when(f == pl.num_programs(1) - 1)
    def _():
        o_ref[...] = acc_ref[...]


def _ffn(x2d, g, wg, wu, cw, wd, seq, tm, tf):
    t, d = x2d.shape
    ff = wg.shape[1]
    cw_p = jnp.pad(cw, ((0, SUBLANES - FFN_CONV), (0, 0)))
    return pl.pallas_call(
        functools.partial(_ffn_kernel, tm=tm, seq=seq),
        grid=(t // tm, ff // tf),
        in_specs=[
            pl.BlockSpec((tm, d), lambda i, f: (i, 0)),
            pl.BlockSpec((FFN_HALO, d), lambda i, f: (jnp.maximum(i * (tm // FFN_HALO) - 1, 0), 0)),
            pl.BlockSpec((1, d), lambda i, f: (0, 0)),
            pl.BlockSpec((d, tf), lambda i, f: (0, f)),
            pl.BlockSpec((d, tf), lambda i, f: (0, f)),
            pl.BlockSpec((SUBLANES, tf), lambda i, f: (0, f)),
            pl.BlockSpec((tf, d), lambda i, f: (f, 0)),
        ],
        out_specs=pl.BlockSpec((tm, d), lambda i, f: (i, 0)),
        out_shape=jax.ShapeDtypeStruct((t, d), F32),
        scratch_shapes=[pltpu.VMEM((FFN_HALO + tm, d), BF16), pltpu.VMEM((FFN_HALO + tm, tf), F32),
                        pltpu.VMEM((tm, d), F32)],
        compiler_params=_cparams(("parallel", "arbitrary")),
        name="ffn",
    )(x2d, x2d, g.reshape(1, d), wg.astype(BF16), wu.astype(BF16), cw_p, wd.astype(BF16))


def _tiles(seq):
    big = min(seq, 512)
    return dict(
        inproj_tm=min(seq, 1024), inproj_tn=SPLIT_COL,
        conv_ts=big, merge_tm=big, ffn_tm=big, ffn_tf=D_FF // 2,
        dsa_tq=min(seq, 512),
    )


def kernel(x, positions, g_mix, w_in, b_gate, g_qa, g_ka, g_kidx, w_oa, cb_conv_w, cb_conv_b, cb_ln_g, cb_ln_b,
           w_ob, cc_conv_w, w_oc, w_od, w_merge, g_ffn, w_ffn_gate, w_ffn_up, ffn_conv_w, w_ffn_down):
    bsz, seq, dm = x.shape
    depth = w_in.shape[0]
    assert dm == D_MODEL and seq % min(CHUNK, seq) == 0 and seq % LANES == 0
    tl = _tiles(seq)
    t = bsz * seq

    cos_a, sin_a = _rope_tables(positions, A_HEAD_DIM)
    cos_i, sin_i = _rope_tables(positions, IDX_DIM)
    ca, sa = _rope_lane_tables(cos_a, sin_a, A_HEAD_DIM)
    ci, si = _rope_lane_tables(cos_i, sin_i, IDX_DIM)

    x2d = x.reshape(t, dm)
    for l in range(depth):
        w_packed = _pack_w_in(jnp.swapaxes(w_in[l], 0, 1))
        tm, tn = tl["inproj_tm"], tl["inproj_tn"]
        proj_a, proj = _inproj(x2d, g_mix[l], w_packed, tm, tn)
        qT, qiT, wT, ki, ka, vT, sq, skT, sv = _prep(proj_a, proj, ca, sa, ci, si, g_qa[l], g_ka[l], g_kidx[l],
                                                     bsz, seq)
        o_a = _dsa_attention(qT, qiT, wT, ki, ka, vT, bsz, seq, tl["dsa_tq"])
        o_d = _sb_attention(sq, skT, sv, bsz, seq)
        o_b, o_c = _conv_mixers(proj, cb_conv_w[l], cb_conv_b[l], cb_ln_g[l], cb_ln_b[l], cc_conv_w[l],
                                seq, tl["conv_ts"])
        x2d = _merge(x2d, o_a, o_b, o_c, o_d, proj, b_gate[l], w_oa[l], w_ob[l], w_oc[l], w_od[l], w_merge[l],
                     tl["merge_tm"])
        x2d = _ffn(x2d, g_ffn[l], w_ffn_gate[l], w_ffn_up[l], ffn_conv_w[l], w_ffn_down[l], seq,
                   tl["ffn_tm"], tl["ffn_tf"])
    return x2d.reshape(bsz, seq, dm)
```

```python
import functools
import math

import jax
import jax.numpy as jnp
from jax import lax
from jax.experimental import pallas as pl
from jax.experimental.pallas import tpu as pltpu

F32 = jnp.float32
BF16 = jnp.bfloat16
I32 = jnp.int32

D_MODEL = 1024
N_BRANCH = 4
A_HEADS = 4
A_HEAD_DIM = 128
A_WIDTH = A_HEADS * A_HEAD_DIM
IDX_HEADS = 4
IDX_DIM = 64
TOPK_MAX = 256
CB_WIDTH = 512
CB_CONV = 31
CC_WIDTH = 512
CC_CONV = 3
SB_HEADS = 4
SB_HEAD_DIM = 128
SB_WIDTH = SB_HEADS * SB_HEAD_DIM
D_FF = 2816
FFN_CONV = 3
ROPE_THETA = 500000.0
ROT_FRACTION_DIV = 4
NORM_EPS = 1e-6

LANES = 128
SUBLANES = 8
VMEM_LIMIT = 56 * 1024 * 1024

LOG2E = math.log2(math.e)
NEG_BIG = -1e30
M_INIT = -1e20
F32_LOWEST = float(jnp.finfo(jnp.float32).min)
KEY_MIN_NORMAL = 0x00800000
TRI_PAD = 16
COUNT_VPU_SHARE = 4
VALUE_PASSES = 12
SEARCH_PASS_CAP = VALUE_PASSES + 36

OFF_AQ = 0
OFF_AK = 512
OFF_AV = 1024
OFF_IQ = 1536
OFF_IK = 1792
OFF_IW = 1920
OFF_GLU = 2048
OFF_CC = 3072
OFF_SB = 4608
OFF_GATE = 6144
N_PACK = 10240
SPLIT_COL = 2048
CHUNK = 512


def _cparams(sem):
    return pltpu.CompilerParams(dimension_semantics=sem, vmem_limit_bytes=VMEM_LIMIT)


def _trig_kernel(pos_ref, invf_ref, cos_ref, sin_ref):
    ang = pos_ref[...] * invf_ref[...]
    cos_ref[...] = jnp.cos(ang)
    sin_ref[...] = jnp.sin(ang)


def _rope_tables(positions, head_dim):
    rot = head_dim // ROT_FRACTION_DIV
    half = rot // 2
    inv_freq = ROPE_THETA ** (-(jnp.arange(half, dtype=F32) * 2.0) / rot)
    t = positions.size
    rows = t * half // LANES
    pos_rep = jnp.broadcast_to(positions.astype(F32).reshape(t, 1), (t, half)).reshape(rows, LANES)
    invf = jnp.tile(inv_freq, LANES // half).reshape(1, LANES)
    tr = min(rows, 512)
    cos, sin = pl.pallas_call(
        _trig_kernel,
        grid=(rows // tr,),
        in_specs=[pl.BlockSpec((tr, LANES), lambda i: (i, 0)), pl.BlockSpec((1, LANES), lambda i: (0, 0))],
        out_specs=[pl.BlockSpec((tr, LANES), lambda i: (i, 0))] * 2,
        out_shape=[jax.ShapeDtypeStruct((rows, LANES), F32)] * 2,
        compiler_params=_cparams(("parallel",)),
        name="rope_trig",
    )(pos_rep, invf)
    return cos.reshape(t, half), sin.reshape(t, half)


def _rope_lane_tables(cos, sin, head_dim):
    t, half = cos.shape
    rest = head_dim - 2 * half
    c = jnp.concatenate([cos, cos, jnp.ones((t, rest), F32)], axis=1)
    s = jnp.concatenate([sin, sin, jnp.zeros((t, rest), F32)], axis=1)
    reps = LANES // head_dim
    return jnp.tile(c, (1, reps)), jnp.tile(s, (1, reps))


def _inproj_kernel(x_ref, g_ref, w_ref, oa_ref, ob_ref, h_ref, *, n_a):
    j = pl.program_id(1)

    @pl.when(j == 0)
    def _():
        x = x_ref[...]
        ms = jnp.mean(x * x, axis=-1, keepdims=True)
        h_ref[...] = (x * lax.rsqrt(ms + NORM_EPS) * g_ref[...]).astype(BF16)

    y = jnp.dot(h_ref[...], w_ref[...], preferred_element_type=F32)

    @pl.when(j < n_a)
    def _():
        oa_ref[...] = y

    @pl.when(j >= n_a)
    def _():
        ob_ref[...] = y.astype(ob_ref.dtype)


def _inproj(x2d, g, w_packed, tm, tn):
    t, d = x2d.shape
    n = w_packed.shape[1]
    n_a = SPLIT_COL // tn
    return pl.pallas_call(
        functools.partial(_inproj_kernel, n_a=n_a),
        grid=(t // tm, n // tn),
        in_specs=[
            pl.BlockSpec((tm, d), lambda i, j: (i, 0)),
            pl.BlockSpec((1, d), lambda i, j: (0, 0)),
            pl.BlockSpec((d, tn), lambda i, j: (0, j)),
        ],
        out_specs=[pl.BlockSpec((tm, tn), lambda i, j: (i, jnp.minimum(j, n_a - 1))),
                   pl.BlockSpec((tm, tn), lambda i, j: (i, jnp.maximum(j - n_a, 0)))],
        out_shape=[jax.ShapeDtypeStruct((t, SPLIT_COL), F32), jax.ShapeDtypeStruct((t, n - SPLIT_COL), BF16)],
        scratch_shapes=[pltpu.VMEM((tm, d), BF16)],
        compiler_params=_cparams(("parallel", "arbitrary")),
        name="inproj",
    )(x2d, g.reshape(1, d), w_packed)


N_IN = 10052
OFF_IW_SRC = 1856
OFF_REST_SRC = 1860
PACK_FIRST = OFF_REST_SRC // LANES
PACK_ROT = (OFF_GLU - OFF_REST_SRC) % LANES


def _pack_kernel(a_ref, b_ref, o_ref):
    n = pl.program_id(0)
    row = lax.broadcasted_iota(I32, a_ref.shape, 0)
    a = a_ref[...]

    @pl.when(n < OFF_IK // LANES)
    def _():
        o_ref[...] = a.T.astype(BF16)

    @pl.when(n == OFF_IK // LANES)
    def _():
        o_ref[...] = jnp.where(row < IDX_DIM, a, 0.0).T.astype(BF16)

    @pl.when(n == OFF_IW // LANES)
    def _():
        iw = pltpu.roll(a, LANES - (OFF_IW_SRC - OFF_IK), 0)
        o_ref[...] = jnp.where(row < IDX_HEADS, iw, 0.0).T.astype(BF16)

    @pl.when(n >= OFF_GLU // LANES)
    def _():
        out = jnp.where(row < PACK_ROT, pltpu.roll(a, PACK_ROT, 0), pltpu.roll(b_ref[...], PACK_ROT, 0))
        o_ref[...] = out.T.astype(BF16)


def _pack_w_in(w_t):
    d = w_t.shape[1]
    first_rest = OFF_GLU // LANES

    def a_idx(n):
        return (jnp.where(n < first_rest, jnp.minimum(n, PACK_FIRST), n - first_rest + PACK_FIRST), 0)

    def b_idx(n):
        return (jnp.where(n < first_rest, PACK_FIRST, n - first_rest + PACK_FIRST + 1), 0)

    return pl.pallas_call(
        _pack_kernel,
        grid=(N_PACK // LANES,),
        in_specs=[pl.BlockSpec((LANES, d), a_idx), pl.BlockSpec((LANES, d), b_idx)],
        out_specs=pl.BlockSpec((d, LANES), lambda n: (0, n)),
        out_shape=jax.ShapeDtypeStruct((d, N_PACK), BF16),
        compiler_params=_cparams(("parallel",)),
        name="pack_w_in",
    )(w_t, w_t)


def _rotate_half(x, half, period):
    lane = lax.broadcasted_iota(I32, x.shape, 1)
    up = pltpu.roll(x, LANES - half, 1)
    dn = pltpu.roll(x, half, 1)
    return jnp.where((lane % period) < half, -up, dn)


def _prep_kernel(p_ref, sqi_ref, ski_ref, svi_ref, ca_ref, sa_ref, ci_ref, si_ref, gq_ref, gk_ref, gi_ref,
                 qT_ref, qiT_ref, wT_ref, ki_ref, ka_ref, vT_ref, sq_ref, skT_ref, sv_ref):
    ca, sa, ci, si = ca_ref[...], sa_ref[...], ci_ref[...], si_ref[...]
    a_half = A_HEAD_DIM // ROT_FRACTION_DIV // 2
    i_half = IDX_DIM // ROT_FRACTION_DIV // 2

    def head_norm(x, g, n_real):
        ms = jnp.sum(x * x, axis=-1, keepdims=True) * (1.0 / n_real)
        return x * lax.rsqrt(ms + NORM_EPS) * g

    for h in range(A_HEADS):
        sl = slice(h * A_HEAD_DIM, (h + 1) * A_HEAD_DIM)
        col = lambda off: slice(off + h * A_HEAD_DIM, off + (h + 1) * A_HEAD_DIM)
        q = head_norm(p_ref[:, col(OFF_AQ)], gq_ref[...], A_HEAD_DIM)
        q = q * ca + _rotate_half(q, a_half, A_HEAD_DIM) * sa
        qT_ref[sl, :] = (q * (A_HEAD_DIM ** -0.5 * LOG2E)).T.astype(BF16)
        k = head_norm(p_ref[:, col(OFF_AK)], gk_ref[...], A_HEAD_DIM)
        ka_ref[:, sl] = (k * ca + _rotate_half(k, a_half, A_HEAD_DIM) * sa).astype(BF16)
        vT_ref[sl, :] = p_ref[:, col(OFF_AV)].T.astype(BF16)

    row = lax.broadcasted_iota(I32, (LANES, p_ref.shape[0]), 0)
    for j in range(IDX_HEADS * IDX_DIM // LANES):
        qi = p_ref[:, OFF_IQ + j * LANES:OFF_IQ + (j + 1) * LANES]
        qt = (qi * ci + _rotate_half(qi, i_half, IDX_DIM) * si).T
        qiT_ref[(2 * j) * LANES:(2 * j + 1) * LANES, :] = jnp.where(row < IDX_DIM, qt, 0.0).astype(BF16)
        qiT_ref[(2 * j + 1) * LANES:(2 * j + 2) * LANES, :] = jnp.where(row >= IDX_DIM, qt, 0.0).astype(BF16)
    ki = head_norm(p_ref[:, OFF_IK:OFF_IK + LANES], gi_ref[...], IDX_DIM)
    ki = ki * ci + _rotate_half(ki, i_half, IDX_DIM) * si
    ki_ref[...] = (ki + pltpu.roll(ki, IDX_DIM, 1)).astype(BF16)
    w = p_ref[:, OFF_IW:OFF_IW + LANES] * (IDX_DIM ** -0.5 * IDX_HEADS ** -0.5)
    wT_ref[...] = w.T[0:SUBLANES, :]

    sq_ref[...] = (sqi_ref[...].astype(F32) * (SB_HEAD_DIM ** -0.5 * LOG2E)).astype(BF16)
    for h in range(SB_HEADS):
        k_t = ski_ref[:, h * SB_HEAD_DIM:(h + 1) * SB_HEAD_DIM].astype(F32).T.astype(BF16)
        sb_ck = skT_ref.shape[-1]
        for c in range(skT_ref.shape[1]):
            skT_ref[h, c] = k_t[:, c * sb_ck:(c + 1) * sb_ck]
    sv_ref[...] = svi_ref[...]


def _prep(proj_a, proj_b, ca, sa, ci, si, g_qa, g_ka, g_kidx, bsz, seq):
    t = proj_a.shape[0]
    ck = min(CHUNK, seq)
    sb_ck = min(SB_CK, seq)
    nck = seq // ck
    gi = jnp.concatenate([g_kidx, jnp.ones((LANES - IDX_DIM,), F32)]).reshape(1, LANES)
    tok = lambda b, i: (b * nck + i, 0)
    fixed = lambda b, i: (0, 0)
    tbl = pl.BlockSpec((ck, LANES), tok)
    gsp = pl.BlockSpec((1, LANES), fixed)
    sb_col = (OFF_SB - SPLIT_COL) // SB_WIDTH
    sb_in = lambda n: pl.BlockSpec((ck, SB_WIDTH), lambda b, i: (b * nck + i, sb_col + n))
    return pl.pallas_call(
        _prep_kernel,
        grid=(bsz, nck),
        in_specs=[pl.BlockSpec((ck, SPLIT_COL), tok), sb_in(0), sb_in(1), sb_in(2),
                  tbl, tbl, tbl, tbl, gsp, gsp, gsp],
        out_specs=[
            pl.BlockSpec((None, A_WIDTH, ck), lambda b, i: (b, 0, i)),
            pl.BlockSpec((None, IDX_HEADS * LANES, ck), lambda b, i: (b, 0, i)),
            pl.BlockSpec((None, SUBLANES, ck), lambda b, i: (b, 0, i)),
            pl.BlockSpec((None, ck, LANES), lambda b, i: (b, i, 0)),
            pl.BlockSpec((None, ck, A_WIDTH), lambda b, i: (b, i, 0)),
            pl.BlockSpec((None, None, A_WIDTH, ck), lambda b, i: (b, i, 0, 0)),
            pl.BlockSpec((ck, SB_WIDTH), tok),
            pl.BlockSpec((None, SB_HEADS, ck // sb_ck, SB_HEAD_DIM, sb_ck), lambda b, i: (b, 0, i, 0, 0)),
            pl.BlockSpec((None, ck, SB_WIDTH), lambda b, i: (b, i, 0)),
        ],
        out_shape=[
            jax.ShapeDtypeStruct((bsz, A_WIDTH, seq), BF16),
            jax.ShapeDtypeStruct((bsz, IDX_HEADS * LANES, seq), BF16),
            jax.ShapeDtypeStruct((bsz, SUBLANES, seq), F32),
            jax.ShapeDtypeStruct((bsz, seq, LANES), BF16),
            jax.ShapeDtypeStruct((bsz, seq, A_WIDTH), BF16),
            jax.ShapeDtypeStruct((bsz, nck, A_WIDTH, ck), BF16),
            jax.ShapeDtypeStruct((t, SB_WIDTH), BF16),
            jax.ShapeDtypeStruct((bsz, SB_HEADS, seq // sb_ck, SB_HEAD_DIM, sb_ck), BF16),
            jax.ShapeDtypeStruct((bsz, seq, SB_WIDTH), BF16),
        ],
        compiler_params=_cparams(("parallel", "parallel")),
        name="mixer_prep",
    )(proj_a, proj_b, proj_b, proj_b, ca, sa, ci, si, g_qa.reshape(1, LANES), g_ka.reshape(1, LANES), gi)


def _order_key(f):
    b = pltpu.bitcast(f, I32)
    return b ^ ((b >> 31) & 0x7FFFFFFF)


def _key_float(k):
    return pltpu.bitcast(k ^ ((k >> 31) & 0x7FFFFFFF), F32)


def _normal_key(k):
    k = jnp.where(jnp.logical_and(k > 0, k < KEY_MIN_NORMAL), KEY_MIN_NORMAL, k)
    return jnp.where(jnp.logical_and(k < 0, k >= -KEY_MIN_NORMAL), 0, k)


def _dsa_kernel(qT_ref, qiT_ref, wT_ref, ki_ref, ka_ref, vT_ref, tri_ref, o_ref, sc_ref, red_ref, cnt_ref, m_ref,
                acc_ref, *, tq, ck, k_sel):
    i = pl.program_id(1)
    n_keys = (i + 1) * tq
    n_chunks = (n_keys + ck - 1) // ck
    q_pos = i * tq + lax.broadcasted_iota(I32, (1, tq), 1)

    def chunk_start(c):
        return pl.multiple_of(c * ck, ck)

    def key_pos(c):
        return c * ck + lax.broadcasted_iota(I32, (ck, tq), 0)

    def score_chunk(c, diag):
        ks = chunk_start(c)
        kic = ki_ref[pl.ds(ks, ck), :]
        sc = None
        for h in range(IDX_HEADS):
            d = jnp.dot(kic, qiT_ref[h * LANES:(h + 1) * LANES, :], preferred_element_type=F32)
            term = jnp.maximum(d, 0.0) * wT_ref[h:h + 1, :]
            sc = term if sc is None else sc + term
        if diag:
            causal = key_pos(c) <= q_pos
            sc_hi = jnp.where(causal, sc, -jnp.inf)
            sc_lo = jnp.where(causal, sc, jnp.inf)
        else:
            sc_hi = sc_lo = sc
        sc_ref[pl.ds(ks, ck), :] = sc_hi
        fold = lambda a, op: op(a.reshape(ck // SUBLANES, SUBLANES, tq), axis=0)
        return fold(sc_hi, jnp.max), fold(sc_lo, jnp.min)

    def score_body(c, carry):
        mx, mn = score_chunk(c, False)
        return jnp.maximum(carry[0], mx), jnp.minimum(carry[1], mn)

    init = (jnp.full((SUBLANES, tq), -jnp.inf, F32), jnp.full((SUBLANES, tq), jnp.inf, F32))
    mx, mn = lax.fori_loop(0, n_chunks - 1, score_body, init)
    mx_d, mn_d = score_chunk(n_chunks - 1, True)
    def fold_rows(x, op):
        red_ref[...] = x
        r = red_ref[0:1, :]
        for j in range(1, SUBLANES):
            r = op(r, red_ref[j:j + 1, :])
        return r

    row_max = fold_rows(jnp.maximum(mx, mx_d), jnp.maximum)
    row_min = fold_rows(jnp.minimum(mn, mn_d), jnp.minimum)

    mxu_rows = ck - ck // COUNT_VPU_SHARE
    ones = jnp.ones((SUBLANES, mxu_rows), BF16)

    def count_ge(t):
        for nc in range(1, sc_ref.shape[0] // ck + 1):
            @pl.when(n_chunks == nc)
            def _(nc=nc):
                on_mxu = None
                on_vpu = jnp.zeros((SUBLANES, tq), F32)
                for c in range(nc):
                    ind = jnp.where(sc_ref[c * ck:c * ck + mxu_rows, :] >= t, 1.0, 0.0).astype(BF16)
                    part = jnp.dot(ones, ind, preferred_element_type=F32)
                    on_mxu = part if on_mxu is None else on_mxu + part
                    tail = jnp.where(sc_ref[c * ck + mxu_rows:(c + 1) * ck, :] >= t, 1.0, 0.0)
                    on_vpu = on_vpu + tail.reshape(-1, SUBLANES, tq).sum(axis=0)
                cnt_ref[0:SUBLANES, :] = on_vpu
                cnt_ref[SUBLANES:, :] = on_mxu
        total = cnt_ref[SUBLANES:SUBLANES + 1, :]
        for j in range(SUBLANES):
            total = total + cnt_ref[j:j + 1, :]
        return total.astype(I32)

    n_causal = q_pos + 1
    take_all = n_causal <= k_sel
    lo0 = _normal_key(_order_key(row_min))
    hi0 = _normal_key(_order_key(row_max)) + 1
    thr0 = jnp.full((1, tq), F32_LOWEST, F32)
    done0 = jnp.where(take_all, 1, 0)
    zero = jnp.zeros((1, tq), I32)

    def search_cond(st):
        it, _, _, _, _, done, _ = st
        return jnp.logical_and(it < SEARCH_PASS_CAP, jnp.min(done) == 0)

    def search_body(st):
        it, lo, hi, c_hi, thr, done, tied = st
        cross = jnp.logical_and(lo < 0, hi > 0)
        gap = hi - lo
        zero_probe = jnp.logical_and(it == 1, lo == 0)
        mid = _normal_key(jnp.where(cross, 0, jnp.where(zero_probe, KEY_MIN_NORMAL, lo + (gap >> 1))))
        v_mid = _normal_key(_order_key(0.5 * _key_float(lo) + 0.5 * _key_float(hi)))
        by_value = jnp.logical_and(jnp.logical_and(it < VALUE_PASSES, jnp.logical_not(jnp.logical_or(cross, zero_probe))),
                                   jnp.logical_and(v_mid > lo, v_mid < hi))
        mid = jnp.where(by_value, v_mid, mid)
        no_pivot = jnp.logical_or(gap == 1, jnp.logical_or(mid >= hi, mid <= lo))
        adjacent = jnp.logical_and(no_pivot, jnp.logical_not(cross))
        pivot = _key_float(mid)
        cnt = count_ge(pivot)
        live = done == 0
        hit = jnp.logical_and(live, jnp.logical_and(cnt == k_sel, jnp.logical_not(adjacent)))
        end = jnp.logical_and(live, adjacent)
        move = jnp.logical_and(live, jnp.logical_not(jnp.logical_or(hit, end)))
        up = jnp.logical_and(move, cnt > k_sel)
        dn = jnp.logical_and(move, cnt < k_sel)
        thr = jnp.where(hit, pivot, jnp.where(end, _key_float(lo), thr))
        tied = jnp.where(end, 1, tied)
        done = jnp.where(jnp.logical_or(hit, end), 1, done)
        lo = jnp.where(up, mid, lo)
        hi = jnp.where(dn, mid, hi)
        c_hi = jnp.where(dn, cnt, c_hi)
        return it + 1, lo, hi, c_hi, thr, done, tied

    st = lax.while_loop(search_cond, search_body, (jnp.int32(0), lo0, hi0, zero, thr0, done0, zero))
    _, _, _, c_hi, thr, _, tied = st

    room = jnp.where(tied > 0, k_sel - c_hi, jnp.int32(2 ** 30)).astype(F32)

    @pl.when(jnp.max(tied) > 0)
    def _():
        red_ref[0:1, :] = jnp.zeros((1, tq), F32)

        def drop_excess(c, carry):
            ks = chunk_start(c)
            s = sc_ref[pl.ds(ks, ck), :]
            is_tied = s == thr
            ind = jnp.where(is_tied, 1.0, 0.0).astype(BF16)
            pre = jnp.dot(tri_ref[...], ind, preferred_element_type=F32)
            before = red_ref[0:1, :]
            rank = pre[0:ck, :] + before
            sc_ref[pl.ds(ks, ck), :] = jnp.where(is_tied, jnp.where(rank >= room, -jnp.inf, s), s)
            red_ref[0:1, :] = before + pre[ck:ck + 1, :]
            return carry
        lax.fori_loop(0, n_chunks, drop_excess, 0)

    m_ref[...] = jnp.full(m_ref.shape, M_INIT, F32)
    acc_ref[...] = jnp.zeros(acc_ref.shape, F32)
    heads = [slice(h * A_HEAD_DIM, (h + 1) * A_HEAD_DIM) for h in range(A_HEADS)]
    ones_rows = jnp.ones((TRI_PAD, ck), BF16)

    def att_body(c, carry):
        ks = chunk_start(c)
        sel = sc_ref[pl.ds(ks, ck), :] >= thr
        logits = [jnp.dot(ka_ref[pl.ds(ks, ck), hs], qT_ref[hs, :], preferred_element_type=F32) for hs in heads]
        for h, hs in enumerate(heads):
            m_old = m_ref[h, 0:1, :]
            m_new = jnp.maximum(m_old, jnp.max(jnp.where(sel, logits[h], NEG_BIG), axis=0, keepdims=True))
            alpha = jnp.exp2(m_old - m_new)
            p = jnp.where(sel, jnp.exp2(logits[h] - m_new), 0.0).astype(BF16)
            m_ref[h, 0:1, :] = m_new
            lhs = jnp.concatenate([vT_ref[c, hs, :], ones_rows], axis=0)
            acc_ref[h] = acc_ref[h] * alpha + jnp.dot(lhs, p, preferred_element_type=F32)
        return carry

    lax.fori_loop(0, n_chunks, att_body, 0)
    for h, hs in enumerate(heads):
        denom = acc_ref[h, A_HEAD_DIM:A_HEAD_DIM + 1, :]
        o_ref[:, hs] = (acc_ref[h, 0:A_HEAD_DIM, :] / denom).T.astype(o_ref.dtype)


def _dsa_attention(qT, qiT, wT, ki, ka, vT, bsz, seq, tq):
    ck = min(CHUNK, seq)
    nck = seq // ck
    k_sel = min(TOPK_MAX, seq // 4)
    kern = functools.partial(_dsa_kernel, tq=tq, ck=ck, k_sel=k_sel)
    once = pl.Buffered(1)
    out = pl.pallas_call(
        kern,
        grid=(bsz, seq // tq),
        in_specs=[
            pl.BlockSpec((None, A_WIDTH, tq), lambda b, i: (b, 0, i)),
            pl.BlockSpec((None, IDX_HEADS * LANES, tq), lambda b, i: (b, 0, i)),
            pl.BlockSpec((None, SUBLANES, tq), lambda b, i: (b, 0, i)),
            pl.BlockSpec((None, seq, LANES), lambda b, i: (b, 0, 0), pipeline_mode=once),
            pl.BlockSpec((None, seq, A_WIDTH), lambda b, i: (b, 0, 0), pipeline_mode=once),
            pl.BlockSpec((None, nck, A_WIDTH, ck), lambda b, i: (b, 0, 0, 0), pipeline_mode=once),
            pl.BlockSpec((ck + TRI_PAD, ck), lambda b, i: (0, 0), pipeline_mode=once),
        ],
        out_specs=pl.BlockSpec((None, tq, A_WIDTH), lambda b, i: (b, i, 0)),
        out_shape=jax.ShapeDtypeStruct((bsz, seq, A_WIDTH), BF16),
        scratch_shapes=[pltpu.VMEM((seq, tq), F32), pltpu.VMEM((SUBLANES, tq), F32),
                        pltpu.VMEM((2 * SUBLANES, tq), F32),
                        pltpu.VMEM((A_HEADS, SUBLANES, tq), F32),
                        pltpu.VMEM((A_HEADS, A_HEAD_DIM + TRI_PAD, tq), F32)],
        compiler_params=_cparams(("parallel", "arbitrary")),
        name="dsa_attention",
    )(qT, qiT, wT, ki, ka, vT, _prefix_matrix(ck))
    return out.reshape(bsz * seq, A_WIDTH)


def _prefix_matrix(n):
    r = jnp.arange(n + TRI_PAD)[:, None]
    c = jnp.arange(n)[None, :]
    return jnp.logical_or(c < r, r >= n).astype(BF16)


SB_CK = 256
SB_TQ = 1024
SB_STEP = 4


def _sb_kernel(q_ref, kT_ref, v_ref, u_ref, o_ref, acc_ref, carry_ref, *, tq, ck):
    i = pl.program_id(2)
    q = q_ref[...]
    acc_ref[...] = jnp.zeros(acc_ref.shape, F32)
    carry_ref[...] = jnp.zeros(carry_ref.shape, F32)
    per = tq // ck

    def chunk(c, r0=None):
        diag = r0 is not None
        r0 = r0 or 0
        rows = tq - r0
        z = jnp.dot(q[r0:, :], kT_ref[c], preferred_element_type=F32)
        neg_abs = pltpu.bitcast(pltpu.bitcast(z, I32) | jnp.int32(-2 ** 31), F32)
        neg_soft = jnp.log(1.0 + jnp.exp2(neg_abs)) * (-LOG2E)
        lk = neg_soft - jnp.maximum(z, 0.0)
        if diag:
            q_pos = i * tq + r0 + lax.broadcasted_iota(I32, (rows, ck), 0)
            mask = (c * ck + lax.broadcasted_iota(I32, (rows, ck), 1)) < q_pos
            lk = jnp.where(mask, lk, 0.0)
        cum = jnp.dot(lk.astype(BF16), u_ref[...], preferred_element_type=F32)
        a = jnp.exp2(z + cum)
        if diag:
            a = jnp.where(mask, a, 0.0)
        vs = pl.multiple_of(c * ck, ck)
        pv = jnp.dot(a.astype(BF16), v_ref[pl.ds(vs, ck), :], preferred_element_type=F32)
        return pv, cum[:, 0:1], r0

    def fold(parts):
        for pv, total, r0 in parts:
            carry = carry_ref[r0:, :]
            acc_ref[r0:, :] += jnp.exp2(carry) * pv
            carry_ref[r0:, :] = carry + jnp.broadcast_to(total, carry.shape)

    step = min(SB_STEP, per)
    for d0 in reversed(range(0, per, step)):
        fold([chunk(i * per + d0 + d, (d0 + d) * ck) for d in reversed(range(step))])

    def body(s, carry):
        first = i * per - 1 - s * step
        fold([chunk(first - d) for d in range(step)])
        return carry

    lax.fori_loop(0, i * (per // step), body, 0)
    o_ref[...] = acc_ref[...].astype(o_ref.dtype)


def _cumsum_matrix(n):
    j = jnp.arange(n)[:, None]
    s = jnp.arange(n)[None, :]
    return (j >= s).astype(BF16)


def _sb_attention(sq, skT, sv, bsz, seq):
    t = bsz * seq
    tq = min(SB_TQ, seq)
    ck = min(SB_CK, seq)
    nq = seq // tq
    return pl.pallas_call(
        functools.partial(_sb_kernel, tq=tq, ck=ck),
        grid=(bsz, SB_HEADS, nq),
        in_specs=[
            pl.BlockSpec((tq, SB_HEAD_DIM), lambda b, h, i: (b * nq + i, h)),
            pl.BlockSpec((None, None, seq // ck, SB_HEAD_DIM, ck), lambda b, h, i: (b, h, 0, 0, 0)),
            pl.BlockSpec((None, seq, SB_HEAD_DIM), lambda b, h, i: (b, 0, h)),
            pl.BlockSpec((ck, ck), lambda b, h, i: (0, 0)),
        ],
        out_specs=pl.BlockSpec((tq, SB_HEAD_DIM), lambda b, h, i: (b * nq + i, h)),
        out_shape=jax.ShapeDtypeStruct((t, SB_WIDTH), BF16),
        scratch_shapes=[pltpu.VMEM((tq, SB_HEAD_DIM), F32), pltpu.VMEM((tq, LANES), F32)],
        compiler_params=_cparams(("parallel", "parallel", "arbitrary")),
        name="stickbreak",
    )(sq, skT, sv, _cumsum_matrix(ck))


CB_HALO = 32
CC_HALO = 16
CONV_ROWS = 64


def _conv_kernel(ga_ref, gb_ref, hga_ref, hgb_ref, cb_ref, cc_ref, cx_ref, hcc_ref, hcx_ref,
                 wb_ref, bb_ref, lng_ref, lnb_ref, wc_ref, ob_ref, oc_ref, buf_ref, u_ref, buf2_ref, *, ts, seq):
    i = pl.program_id(0)
    keep = jnp.where((i * ts) % seq == 0, 0.0, 1.0)

    f32 = lambda ref: ref[...].astype(F32)
    buf_ref[0:CB_HALO, :] = f32(hga_ref) * jax.nn.sigmoid(f32(hgb_ref)) * keep
    buf_ref[CB_HALO:, :] = f32(ga_ref) * jax.nn.sigmoid(f32(gb_ref))
    off0 = CB_HALO - (CB_CONV - 1)
    for r in range(ts // CONV_ROWS):
        t0 = r * CONV_ROWS
        for g in range(CB_WIDTH // LANES):
            ls = slice(g * LANES, (g + 1) * LANES)
            acc = jnp.broadcast_to(bb_ref[:, ls], (CONV_ROWS, LANES))
            for s in range(SUBLANES):
                rows = CONV_ROWS + (SUBLANES if s else 0)
                part = None
                for k in range(CB_CONV):
                    if (off0 + k) % SUBLANES != s:
                        continue
                    base = t0 + (off0 + k) - s
                    term = wb_ref[k:k + 1, ls] * buf_ref[base:base + rows, ls]
                    part = term if part is None else part + term
                acc = acc + part[s:s + CONV_ROWS, :]
            u_ref[t0:t0 + CONV_ROWS, ls] = acc
    u = u_ref[...]
    mu = jnp.mean(u, axis=-1, keepdims=True)
    uc = u - mu
    y = uc * lax.rsqrt(jnp.mean(uc * uc, axis=-1, keepdims=True) + NORM_EPS) * lng_ref[...] + lnb_ref[...]
    ob_ref[...] = (y * jax.nn.sigmoid(y)).astype(ob_ref.dtype)

    buf2_ref[0:CC_HALO, :] = f32(hcc_ref) * f32(hcx_ref) * keep
    buf2_ref[CC_HALO:, :] = f32(cc_ref) * f32(cx_ref)
    conv = jnp.zeros((ts, CC_WIDTH), F32)
    for k in range(CC_CONV):
        base = CC_HALO - (CC_CONV - 1) + k
        conv = conv + wc_ref[k:k + 1, :] * buf2_ref[base:base + ts, :]
    oc_ref[...] = (f32(cb_ref) * conv).astype(oc_ref.dtype)


def _conv_mixers(proj, wb, bb, lng, lnb, wc, seq, ts):
    t = proj.shape[0]
    w = CB_WIDTH
    c0 = (OFF_GLU - SPLIT_COL) // w
    c1 = (OFF_CC - SPLIT_COL) // w
    cur = lambda col: pl.BlockSpec((ts, w), lambda i: (i, col))
    halo = lambda rows, col: pl.BlockSpec((rows, w), lambda i: (jnp.maximum(i * (ts // rows) - 1, 0), col))
    full = lambda r: pl.BlockSpec((r, w), lambda i: (0, 0))
    wb_p = jnp.pad(wb, ((0, CB_HALO - CB_CONV), (0, 0)))
    wc_p = jnp.pad(wc, ((0, SUBLANES - CC_CONV), (0, 0)))
    return pl.pallas_call(
        functools.partial(_conv_kernel, ts=ts, seq=seq),
        grid=(t // ts,),
        in_specs=[cur(c0), cur(c0 + 1), halo(CB_HALO, c0), halo(CB_HALO, c0 + 1),
                  cur(c1), cur(c1 + 1), cur(c1 + 2), halo(CC_HALO, c1 + 1), halo(CC_HALO, c1 + 2),
                  full(CB_HALO), full(1), full(1), full(1), full(SUBLANES)],
        out_specs=[pl.BlockSpec((ts, w), lambda i: (i, 0))] * 2,
        out_shape=[jax.ShapeDtypeStruct((t, w), BF16)] * 2,
        scratch_shapes=[pltpu.VMEM((CB_HALO + ts, w), F32), pltpu.VMEM((ts, w), F32),
                        pltpu.VMEM((CC_HALO + ts, w), F32)],
        compiler_params=_cparams(("parallel",)),
        name="conv_mixers",
    )(proj, proj, proj, proj, proj, proj, proj, proj, proj,
      wb_p, bb.reshape(1, w), lng.reshape(1, w), lnb.reshape(1, w), wc_p)


def _merge_kernel(x_ref, oa_ref, ob_ref, oc_ref, od_ref, g0_ref, g1_ref, g2_ref, g3_ref, bg_ref,
                  wa_ref, wb_ref, wc_ref, wd_ref, wm_ref, o_ref):
    merged = None
    branches = ((oa_ref, wa_ref, g0_ref), (ob_ref, wb_ref, g1_ref), (oc_ref, wc_ref, g2_ref), (od_ref, wd_ref, g3_ref))
    for n, (b_ref, w_ref, g_ref) in enumerate(branches):
        y = jnp.dot(b_ref[...], w_ref[...], preferred_element_type=F32)
        term = jax.nn.sigmoid(g_ref[...].astype(F32) + bg_ref[n:n + 1, :]) * y
        merged = term if merged is None else merged + term
    o_ref[...] = x_ref[...] + jnp.dot(merged.astype(BF16), wm_ref[...], preferred_element_type=F32)


def _merge(x2d, oa, ob, oc, od, proj, b_gate, w_oa, w_ob, w_oc, w_od, w_merge, tm):
    t, d = x2d.shape
    g0 = (OFF_GATE - SPLIT_COL) // d
    row = lambda i: (i, 0)
    fixed = lambda i: (0, 0)
    br = pl.BlockSpec((tm, A_WIDTH), row)
    gate = lambda n: pl.BlockSpec((tm, d), lambda i: (i, g0 + n))
    wsp = pl.BlockSpec((A_WIDTH, d), fixed)
    bf = lambda w: w.astype(BF16)
    return pl.pallas_call(
        _merge_kernel,
        grid=(t // tm,),
        in_specs=[pl.BlockSpec((tm, d), row), br, br, br, br, gate(0), gate(1), gate(2), gate(3),
                  pl.BlockSpec((N_BRANCH, d), fixed), wsp, wsp, wsp, wsp, pl.BlockSpec((d, d), fixed)],
        out_specs=pl.BlockSpec((tm, d), row),
        out_shape=jax.ShapeDtypeStruct((t, d), F32),
        compiler_params=_cparams(("parallel",)),
        name="merge",
    )(x2d, oa, ob, oc, od, proj, proj, proj, proj, b_gate.reshape(N_BRANCH, d),
      bf(w_oa), bf(w_ob), bf(w_oc), bf(w_od), bf(w_merge))


FFN_HALO = 8


def _ffn_kernel(x_ref, hx_ref, g_ref, wg_ref, wu_ref, cw_ref, wd_ref, o_ref, h_ref, gp_ref, acc_ref, *, tm, seq):
    i = pl.program_id(0)
    f = pl.program_id(1)

    def norm(x):
        ms = jnp.mean(x * x, axis=-1, keepdims=True)
        return (x * lax.rsqrt(ms + NORM_EPS) * g_ref[...]).astype(BF16)

    @pl.when(f == 0)
    def _():
        h_ref[0:FFN_HALO, :] = norm(hx_ref[...])
        h_ref[FFN_HALO:, :] = norm(x_ref[...])
        acc_ref[...] = x_ref[...]

    keep = jnp.where((i * tm) % seq == 0, 0.0, 1.0)
    gp_ref[...] = jnp.dot(h_ref[...], wg_ref[...], preferred_element_type=F32)
    gp_ref[0:FFN_HALO, :] = gp_ref[0:FFN_HALO, :] * keep
    gt = jnp.zeros((tm, gp_ref.shape[1]), F32)
    for k in range(FFN_CONV):
        base = FFN_HALO - (FFN_CONV - 1) + k
        gt = gt + cw_ref[k:k + 1, :] * gp_ref[base:base + tm, :]
    up = jnp.dot(h_ref[FFN_HALO:, :], wu_ref[...], preferred_element_type=F32)
    act = (gt * jax.nn.sigmoid(gt) * up).astype(BF16)
    acc_ref[...] += jnp.dot(act, wd_ref[...], preferred_element_type=F32)

    @pl.when(f == pl.num_programs(1) - 1)
    def _():
        o_ref[...] = acc_ref[...]


def _ffn(x2d, g, wg, wu, cw, wd, seq, tm, tf):
    t, d = x2d.shape
    ff = wg.shape[1]
    cw_p = jnp.pad(cw, ((0, SUBLANES - FFN_CONV), (0, 0)))
    return pl.pallas_call(
        functools.partial(_ffn_kernel, tm=tm, seq=seq),
        grid=(t // tm, ff // tf),
        in_specs=[
            pl.BlockSpec((tm, d), lambda i, f: (i, 0)),
            pl.BlockSpec((FFN_HALO, d), lambda i, f: (jnp.maximum(i * (tm // FFN_HALO) - 1, 0), 0)),
            pl.BlockSpec((1, d), lambda i, f: (0, 0)),
            pl.BlockSpec((d, tf), lambda i, f: (0, f)),
            pl.BlockSpec((d, tf), lambda i, f: (0, f)),
            pl.BlockSpec((SUBLANES, tf), lambda i, f: (0, f)),
            pl.BlockSpec((tf, d), lambda i, f: (f, 0)),
        ],
        out_specs=pl.BlockSpec((tm, d), lambda i, f: (i, 0)),
        out_shape=jax.ShapeDtypeStruct((t, d), F32),
        scratch_shapes=[pltpu.VMEM((FFN_HALO + tm, d), BF16), pltpu.VMEM((FFN_HALO + tm, tf), F32),
                        pltpu.VMEM((tm, d), F32)],
        compiler_params=_cparams(("parallel", "arbitrary")),
        name="ffn",
    )(x2d, x2d, g.reshape(1, d), wg.astype(BF16), wu.astype(BF16), cw_p, wd.astype(BF16))


def _tiles(seq):
    big = min(seq, 512)
    return dict(
        inproj_tm=min(seq, 1024), inproj_tn=SPLIT_COL,
        conv_ts=big, merge_tm=big, ffn_tm=big, ffn_tf=D_FF // 2,
        dsa_tq=min(seq, 512),
    )


def kernel(x, positions, g_mix, w_in, b_gate, g_qa, g_ka, g_kidx, w_oa, cb_conv_w, cb_conv_b, cb_ln_g, cb_ln_b,
           w_ob, cc_conv_w, w_oc, w_od, w_merge, g_ffn, w_ffn_gate, w_ffn_up, ffn_conv_w, w_ffn_down):
    bsz, seq, dm = x.shape
    depth = w_in.shape[0]
    assert dm == D_MODEL and seq % min(CHUNK, seq) == 0 and seq % LANES == 0
    tl = _tiles(seq)
    t = bsz * seq

    cos_a, sin_a = _rope_tables(positions, A_HEAD_DIM)
    cos_i, sin_i = _rope_tables(positions, IDX_DIM)
    ca, sa = _rope_lane_tables(cos_a, sin_a, A_HEAD_DIM)
    ci, si = _rope_lane_tables(cos_i, sin_i, IDX_DIM)

    x2d = x.reshape(t, dm)
    for l in range(depth):
        w_packed = _pack_w_in(jnp.swapaxes(w_in[l], 0, 1))
        tm, tn = tl["inproj_tm"], tl["inproj_tn"]
        proj_a, proj = _inproj(x2d, g_mix[l], w_packed, tm, tn)
        qT, qiT, wT, ki, ka, vT, sq, skT, sv = _prep(proj_a, proj, ca, sa, ci, si, g_qa[l], g_ka[l], g_kidx[l],
                                                     bsz, seq)
        o_a = _dsa_attention(qT, qiT, wT, ki, ka, vT, bsz, seq, tl["dsa_tq"])
        o_d = _sb_attention(sq, skT, sv, bsz, seq)
        o_b, o_c = _conv_mixers(proj, cb_conv_w[l], cb_conv_b[l], cb_ln_g[l], cb_ln_b[l], cc_conv_w[l],
                                seq, tl["conv_ts"])
        x2d = _merge(x2d, o_a, o_b, o_c, o_d, proj, b_gate[l], w_oa[l], w_ob[l], w_oc[l], w_od[l], w_merge[l],
                     tl["merge_tm"])
        x2d = _ffn(x2d, g_ffn[l], w_ffn_gate[l], w_ffn_up[l], ffn_conv_w[l], w_ffn_down[l], seq,
                   tl["ffn_tm"], tl["ffn_tf"])
    return x2d.reshape(bsz, seq, dm)
```

```python
import functools
import math

import jax
import jax.numpy as jnp
from jax import lax
from jax.experimental import pallas as pl
from jax.experimental.pallas import tpu as pltpu

F32 = jnp.float32
BF16 = jnp.bfloat16
I32 = jnp.int32

D_MODEL = 1024
N_BRANCH = 4
A_HEADS = 4
A_HEAD_DIM = 128
A_WIDTH = A_HEADS * A_HEAD_DIM
IDX_HEADS = 4
IDX_DIM = 64
TOPK_MAX = 256
CB_WIDTH = 512
CB_CONV = 31
CC_WIDTH = 512
CC_CONV = 3
SB_HEADS = 4
SB_HEAD_DIM = 128
SB_WIDTH = SB_HEADS * SB_HEAD_DIM
D_FF = 2816
FFN_CONV = 3
ROPE_THETA = 500000.0
ROT_FRACTION_DIV = 4
NORM_EPS = 1e-6

LANES = 128
SUBLANES = 8
VMEM_LIMIT = 56 * 1024 * 1024

LOG2E = math.log2(math.e)
NEG_BIG = -1e30
M_INIT = -1e20
F32_LOWEST = float(jnp.finfo(jnp.float32).min)
KEY_MIN_NORMAL = 0x00800000
TRI_PAD = 16
COUNT_VPU_SHARE = 4
VALUE_PASSES = 12
SEARCH_PASS_CAP = VALUE_PASSES + 36

OFF_AQ = 0
OFF_AK = 512
OFF_AV = 1024
OFF_IQ = 1536
OFF_IK = 1792
OFF_IW = 1920
OFF_GLU = 2048
OFF_CC = 3072
OFF_SB = 4608
OFF_GATE = 6144
N_PACK = 10240
SPLIT_COL = 2048
CHUNK = 512


def _cparams(sem):
    return pltpu.CompilerParams(dimension_semantics=sem, vmem_limit_bytes=VMEM_LIMIT)


def _trig_kernel(pos_ref, invf_ref, cos_ref, sin_ref):
    ang = pos_ref[...] * invf_ref[...]
    cos_ref[...] = jnp.cos(ang)
    sin_ref[...] = jnp.sin(ang)


def _rope_tables(positions, head_dim):
    rot = head_dim // ROT_FRACTION_DIV
    half = rot // 2
    inv_freq = ROPE_THETA ** (-(jnp.arange(half, dtype=F32) * 2.0) / rot)
    t = positions.size
    rows = t * half // LANES
    pos_rep = jnp.broadcast_to(positions.astype(F32).reshape(t, 1), (t, half)).reshape(rows, LANES)
    invf = jnp.tile(inv_freq, LANES // half).reshape(1, LANES)
    tr = min(rows, 512)
    cos, sin = pl.pallas_call(
        _trig_kernel,
        grid=(rows // tr,),
        in_specs=[pl.BlockSpec((tr, LANES), lambda i: (i, 0)), pl.BlockSpec((1, LANES), lambda i: (0, 0))],
        out_specs=[pl.BlockSpec((tr, LANES), lambda i: (i, 0))] * 2,
        out_shape=[jax.ShapeDtypeStruct((rows, LANES), F32)] * 2,
        compiler_params=_cparams(("parallel",)),
        name="rope_trig",
    )(pos_rep, invf)
    return cos.reshape(t, half), sin.reshape(t, half)


def _rope_lane_tables(cos, sin, head_dim):
    t, half = cos.shape
    rest = head_dim - 2 * half
    c = jnp.concatenate([cos, cos, jnp.ones((t, rest), F32)], axis=1)
    s = jnp.concatenate([sin, sin, jnp.zeros((t, rest), F32)], axis=1)
    reps = LANES // head_dim
    return jnp.tile(c, (1, reps)), jnp.tile(s, (1, reps))


def _inproj_kernel(x_ref, g_ref, w_ref, oa_ref, ob_ref, h_ref, *, n_a):
    j = pl.program_id(1)

    @pl.when(j == 0)
    def _():
        x = x_ref[...]
        ms = jnp.mean(x * x, axis=-1, keepdims=True)
        h_ref[...] = (x * lax.rsqrt(ms + NORM_EPS) * g_ref[...]).astype(BF16)

    y = jnp.dot(h_ref[...], w_ref[...], preferred_element_type=F32)

    @pl.when(j < n_a)
    def _():
        oa_ref[...] = y

    @pl.when(j >= n_a)
    def _():
        ob_ref[...] = y.astype(ob_ref.dtype)


def _inproj(x2d, g, w_packed, tm, tn):
    t, d = x2d.shape
    n = w_packed.shape[1]
    n_a = SPLIT_COL // tn
    return pl.pallas_call(
        functools.partial(_inproj_kernel, n_a=n_a),
        grid=(t // tm, n // tn),
        in_specs=[
            pl.BlockSpec((tm, d), lambda i, j: (i, 0)),
            pl.BlockSpec((1, d), lambda i, j: (0, 0)),
            pl.BlockSpec((d, tn), lambda i, j: (0, j)),
        ],
        out_specs=[pl.BlockSpec((tm, tn), lambda i, j: (i, jnp.minimum(j, n_a - 1))),
                   pl.BlockSpec((tm, tn), lambda i, j: (i, jnp.maximum(j - n_a, 0)))],
        out_shape=[jax.ShapeDtypeStruct((t, SPLIT_COL), F32), jax.ShapeDtypeStruct((t, n - SPLIT_COL), BF16)],
        scratch_shapes=[pltpu.VMEM((tm, d), BF16)],
        compiler_params=_cparams(("parallel", "arbitrary")),
        name="inproj",
    )(x2d, g.reshape(1, d), w_packed)


N_IN = 10052
OFF_IW_SRC = 1856
OFF_REST_SRC = 1860
PACK_FIRST = OFF_REST_SRC // LANES
PACK_ROT = (OFF_GLU - OFF_REST_SRC) % LANES


def _pack_kernel(a_ref, b_ref, o_ref):
    n = pl.program_id(0)
    row = lax.broadcasted_iota(I32, a_ref.shape, 0)
    a = a_ref[...]

    @pl.when(n < OFF_IK // LANES)
    def _():
        o_ref[...] = a.T.astype(BF16)

    @pl.when(n == OFF_IK // LANES)
    def _():
        o_ref[...] = jnp.where(row < IDX_DIM, a, 0.0).T.astype(BF16)

    @pl.when(n == OFF_IW // LANES)
    def _():
        iw = pltpu.roll(a, LANES - (OFF_IW_SRC - OFF_IK), 0)
        o_ref[...] = jnp.where(row < IDX_HEADS, iw, 0.0).T.astype(BF16)

    @pl.when(n >= OFF_GLU // LANES)
    def _():
        out = jnp.where(row < PACK_ROT, pltpu.roll(a, PACK_ROT, 0), pltpu.roll(b_ref[...], PACK_ROT, 0))
        o_ref[...] = out.T.astype(BF16)


def _pack_w_in(w_t):
    d = w_t.shape[1]
    first_rest = OFF_GLU // LANES

    def a_idx(n):
        return (jnp.where(n < first_rest, jnp.minimum(n, PACK_FIRST), n - first_rest + PACK_FIRST), 0)

    def b_idx(n):
        return (jnp.where(n < first_rest, PACK_FIRST, n - first_rest + PACK_FIRST + 1), 0)

    return pl.pallas_call(
        _pack_kernel,
        grid=(N_PACK // LANES,),
        in_specs=[pl.BlockSpec((LANES, d), a_idx), pl.BlockSpec((LANES, d), b_idx)],
        out_specs=pl.BlockSpec((d, LANES), lambda n: (0, n)),
        out_shape=jax.ShapeDtypeStruct((d, N_PACK), BF16),
        compiler_params=_cparams(("parallel",)),
        name="pack_w_in",
    )(w_t, w_t)


def _rotate_half(x, half, period):
    lane = lax.broadcasted_iota(I32, x.shape, 1)
    up = pltpu.roll(x, LANES - half, 1)
    dn = pltpu.roll(x, half, 1)
    return jnp.where((lane % period) < half, -up, dn)


def _prep_kernel(p_ref, sqi_ref, ski_ref, svi_ref, ca_ref, sa_ref, ci_ref, si_ref, gq_ref, gk_ref, gi_ref,
                 qT_ref, qiT_ref, wT_ref, ki_ref, ka_ref, vT_ref, sq_ref, skT_ref, sv_ref):
    ca, sa, ci, si = ca_ref[...], sa_ref[...], ci_ref[...], si_ref[...]
    a_half = A_HEAD_DIM // ROT_FRACTION_DIV // 2
    i_half = IDX_DIM // ROT_FRACTION_DIV // 2

    def head_norm(x, g, n_real):
        ms = jnp.sum(x * x, axis=-1, keepdims=True) * (1.0 / n_real)
        return x * lax.rsqrt(ms + NORM_EPS) * g

    for h in range(A_HEADS):
        sl = slice(h * A_HEAD_DIM, (h + 1) * A_HEAD_DIM)
        col = lambda off: slice(off + h * A_HEAD_DIM, off + (h + 1) * A_HEAD_DIM)
        q = head_norm(p_ref[:, col(OFF_AQ)], gq_ref[...], A_HEAD_DIM)
        q = q * ca + _rotate_half(q, a_half, A_HEAD_DIM) * sa
        qT_ref[sl, :] = (q * (A_HEAD_DIM ** -0.5 * LOG2E)).T.astype(BF16)
        k = head_norm(p_ref[:, col(OFF_AK)], gk_ref[...], A_HEAD_DIM)
        ka_ref[:, sl] = (k * ca + _rotate_half(k, a_half, A_HEAD_DIM) * sa).astype(BF16)
        vT_ref[sl, :] = p_ref[:, col(OFF_AV)].T.astype(BF16)

    row = lax.broadcasted_iota(I32, (LANES, p_ref.shape[0]), 0)
    for j in range(IDX_HEADS * IDX_DIM // LANES):
        qi = p_ref[:, OFF_IQ + j * LANES:OFF_IQ + (j + 1) * LANES]
        qt = (qi * ci + _rotate_half(qi, i_half, IDX_DIM) * si).T
        qiT_ref[(2 * j) * LANES:(2 * j + 1) * LANES, :] = jnp.where(row < IDX_DIM, qt, 0.0).astype(BF16)
        qiT_ref[(2 * j + 1) * LANES:(2 * j + 2) * LANES, :] = jnp.where(row >= IDX_DIM, qt, 0.0).astype(BF16)
    ki = head_norm(p_ref[:, OFF_IK:OFF_IK + LANES], gi_ref[...], IDX_DIM)
    ki = ki * ci + _rotate_half(ki, i_half, IDX_DIM) * si
    ki_ref[...] = (ki + pltpu.roll(ki, IDX_DIM, 1)).astype(BF16)
    w = p_ref[:, OFF_IW:OFF_IW + LANES] * (IDX_DIM ** -0.5 * IDX_HEADS ** -0.5)
    wT_ref[...] = w.T[0:SUBLANES, :]

    sq_ref[...] = (sqi_ref[...].astype(F32) * (SB_HEAD_DIM ** -0.5 * LOG2E)).astype(BF16)
    for h in range(SB_HEADS):
        k_t = ski_ref[:, h * SB_HEAD_DIM:(h + 1) * SB_HEAD_DIM].astype(F32).T.astype(BF16)
        sb_ck = skT_ref.shape[-1]
        for c in range(skT_ref.shape[1]):
            skT_ref[h, c] = k_t[:, c * sb_ck:(c + 1) * sb_ck]
    sv_ref[...] = svi_ref[...]


def _prep(proj_a, proj_b, ca, sa, ci, si, g_qa, g_ka, g_kidx, bsz, seq):
    t = proj_a.shape[0]
    ck = min(CHUNK, seq)
    sb_ck = min(SB_CK, seq)
    nck = seq // ck
    gi = jnp.concatenate([g_kidx, jnp.ones((LANES - IDX_DIM,), F32)]).reshape(1, LANES)
    tok = lambda b, i: (b * nck + i, 0)
    fixed = lambda b, i: (0, 0)
    tbl = pl.BlockSpec((ck, LANES), tok)
    gsp = pl.BlockSpec((1, LANES), fixed)
    sb_col = (OFF_SB - SPLIT_COL) // SB_WIDTH
    sb_in = lambda n: pl.BlockSpec((ck, SB_WIDTH), lambda b, i: (b * nck + i, sb_col + n))
    return pl.pallas_call(
        _prep_kernel,
        grid=(bsz, nck),
        in_specs=[pl.BlockSpec((ck, SPLIT_COL), tok), sb_in(0), sb_in(1), sb_in(2),
                  tbl, tbl, tbl, tbl, gsp, gsp, gsp],
        out_specs=[
            pl.BlockSpec((None, A_WIDTH, ck), lambda b, i: (b, 0, i)),
            pl.BlockSpec((None, IDX_HEADS * LANES, ck), lambda b, i: (b, 0, i)),
            pl.BlockSpec((None, SUBLANES, ck), lambda b, i: (b, 0, i)),
            pl.BlockSpec((None, ck, LANES), lambda b, i: (b, i, 0)),
            pl.BlockSpec((None, ck, A_WIDTH), lambda b, i: (b, i, 0)),
            pl.BlockSpec((None, None, A_WIDTH, ck), lambda b, i: (b, i, 0, 0)),
            pl.BlockSpec((ck, SB_WIDTH), tok),
            pl.BlockSpec((None, SB_HEADS, ck // sb_ck, SB_HEAD_DIM, sb_ck), lambda b, i: (b, 0, i, 0, 0)),
            pl.BlockSpec((None, ck, SB_WIDTH), lambda b, i: (b, i, 0)),
        ],
        out_shape=[
            jax.ShapeDtypeStruct((bsz, A_WIDTH, seq), BF16),
            jax.ShapeDtypeStruct((bsz, IDX_HEADS * LANES, seq), BF16),
            jax.ShapeDtypeStruct((bsz, SUBLANES, seq), F32),
            jax.ShapeDtypeStruct((bsz, seq, LANES), BF16),
            jax.ShapeDtypeStruct((bsz, seq, A_WIDTH), BF16),
            jax.ShapeDtypeStruct((bsz, nck, A_WIDTH, ck), BF16),
            jax.ShapeDtypeStruct((t, SB_WIDTH), BF16),
            jax.ShapeDtypeStruct((bsz, SB_HEADS, seq // sb_ck, SB_HEAD_DIM, sb_ck), BF16),
            jax.ShapeDtypeStruct((bsz, seq, SB_WIDTH), BF16),
        ],
        compiler_params=_cparams(("parallel", "parallel")),
        name="mixer_prep",
    )(proj_a, proj_b, proj_b, proj_b, ca, sa, ci, si, g_qa.reshape(1, LANES), g_ka.reshape(1, LANES), gi)


def _order_key(f):
    b = pltpu.bitcast(f, I32)
    return b ^ ((b >> 31) & 0x7FFFFFFF)


def _key_float(k):
    return pltpu.bitcast(k ^ ((k >> 31) & 0x7FFFFFFF), F32)


def _normal_key(k):
    k = jnp.where(jnp.logical_and(k > 0, k < KEY_MIN_NORMAL), KEY_MIN_NORMAL, k)
    return jnp.where(jnp.logical_and(k < 0, k >= -KEY_MIN_NORMAL), 0, k)


def _dsa_kernel(qT_ref, qiT_ref, wT_ref, ki_ref, ka_ref, vT_ref, tri_ref, o_ref, sc_ref, red_ref, cnt_ref, m_ref,
                acc_ref, *, tq, ck, k_sel):
    i = pl.program_id(1)
    n_keys = (i + 1) * tq
    n_chunks = (n_keys + ck - 1) // ck
    q_pos = i * tq + lax.broadcasted_iota(I32, (1, tq), 1)

    def chunk_start(c):
        return pl.multiple_of(c * ck, ck)

    def key_pos(c):
        return c * ck + lax.broadcasted_iota(I32, (ck, tq), 0)

    def score_chunk(c, diag):
        ks = chunk_start(c)
        kic = ki_ref[pl.ds(ks, ck), :]
        sc = None
        for h in range(IDX_HEADS):
            d = jnp.dot(kic, qiT_ref[h * LANES:(h + 1) * LANES, :], preferred_element_type=F32)
            term = jnp.maximum(d, 0.0) * wT_ref[h:h + 1, :]
            sc = term if sc is None else sc + term
        if diag:
            causal = key_pos(c) <= q_pos
            sc_hi = jnp.where(causal, sc, -jnp.inf)
            sc_lo = jnp.where(causal, sc, jnp.inf)
        else:
            sc_hi = sc_lo = sc
        sc_ref[pl.ds(ks, ck), :] = sc_hi
        fold = lambda a, op: op(a.reshape(ck // SUBLANES, SUBLANES, tq), axis=0)
        return fold(sc_hi, jnp.max), fold(sc_lo, jnp.min)

    def score_body(c, carry):
        mx, mn = score_chunk(c, False)
        return jnp.maximum(carry[0], mx), jnp.minimum(carry[1], mn)

    init = (jnp.full((SUBLANES, tq), -jnp.inf, F32), jnp.full((SUBLANES, tq), jnp.inf, F32))
    mx, mn = lax.fori_loop(0, n_chunks - 1, score_body, init)
    mx_d, mn_d = score_chunk(n_chunks - 1, True)
    def fold_rows(x, op):
        red_ref[...] = x
        r = red_ref[0:1, :]
        for j in range(1, SUBLANES):
            r = op(r, red_ref[j:j + 1, :])
        return r

    row_max = fold_rows(jnp.maximum(mx, mx_d), jnp.maximum)
    row_min = fold_rows(jnp.minimum(mn, mn_d), jnp.minimum)

    mxu_rows = ck - ck // COUNT_VPU_SHARE
    ones = jnp.ones((SUBLANES, mxu_rows), BF16)

    def count_ge(t):
        for nc in range(1, sc_ref.shape[0] // ck + 1):
            @pl.when(n_chunks == nc)
            def _(nc=nc):
                on_mxu = None
                on_vpu = jnp.zeros((SUBLANES, tq), F32)
                for c in range(nc):
                    ind = jnp.where(sc_ref[c * ck:c * ck + mxu_rows, :] >= t, 1.0, 0.0).astype(BF16)
                    part = jnp.dot(ones, ind, preferred_element_type=F32)
                    on_mxu = part if on_mxu is None else on_mxu + part
                    tail = jnp.where(sc_ref[c * ck + mxu_rows:(c + 1) * ck, :] >= t, 1.0, 0.0)
                    on_vpu = on_vpu + tail.reshape(-1, SUBLANES, tq).sum(axis=0)
                cnt_ref[0:SUBLANES, :] = on_vpu
                cnt_ref[SUBLANES:, :] = on_mxu
        total = cnt_ref[SUBLANES:SUBLANES + 1, :]
        for j in range(SUBLANES):
            total = total + cnt_ref[j:j + 1, :]
        return total.astype(I32)

    n_causal = q_pos + 1
    take_all = n_causal <= k_sel
    lo0 = _normal_key(_order_key(row_min))
    hi0 = _normal_key(_order_key(row_max)) + 1
    thr0 = jnp.full((1, tq), F32_LOWEST, F32)
    done0 = jnp.where(take_all, 1, 0)
    zero = jnp.zeros((1, tq), I32)

    def search_cond(st):
        it, _, _, _, _, done, _ = st
        return jnp.logical_and(it < SEARCH_PASS_CAP, jnp.min(done) == 0)

    def search_body(st):
        it, lo, hi, c_hi, thr, done, tied = st
        cross = jnp.logical_and(lo < 0, hi > 0)
        gap = hi - lo
        zero_probe = jnp.logical_and(it == 1, lo == 0)
        mid = _normal_key(jnp.where(cross, 0, jnp.where(zero_probe, KEY_MIN_NORMAL, lo + (gap >> 1))))
        v_mid = _normal_key(_order_key(0.5 * _key_float(lo) + 0.5 * _key_float(hi)))
        by_value = jnp.logical_and(jnp.logical_and(it < VALUE_PASSES, jnp.logical_not(jnp.logical_or(cross, zero_probe))),
                                   jnp.logical_and(v_mid > lo, v_mid < hi))
        mid = jnp.where(by_value, v_mid, mid)
        no_pivot = jnp.logical_or(gap == 1, jnp.logical_or(mid >= hi, mid <= lo))
        adjacent = jnp.logical_and(no_pivot, jnp.logical_not(cross))
        pivot = _key_float(mid)
        cnt = count_ge(pivot)
        live = done == 0
        hit = jnp.logical_and(live, jnp.logical_and(cnt == k_sel, jnp.logical_not(adjacent)))
        end = jnp.logical_and(live, adjacent)
        move = jnp.logical_and(live, jnp.logical_not(jnp.logical_or(hit, end)))
        up = jnp.logical_and(move, cnt > k_sel)
        dn = jnp.logical_and(move, cnt < k_sel)
        thr = jnp.where(hit, pivot, jnp.where(end, _key_float(lo), thr))
        tied = jnp.where(end, 1, tied)
        done = jnp.where(jnp.logical_or(hit, end), 1, done)
        lo = jnp.where(up, mid, lo)
        hi = jnp.where(dn, mid, hi)
        c_hi = jnp.where(dn, cnt, c_hi)
        return it + 1, lo, hi, c_hi, thr, done, tied

    st = lax.while_loop(search_cond, search_body, (jnp.int32(0), lo0, hi0, zero, thr0, done0, zero))
    _, _, _, c_hi, thr, _, tied = st

    room = jnp.where(tied > 0, k_sel - c_hi, jnp.int32(2 ** 30)).astype(F32)

    red_ref[0:1, :] = jnp.zeros((1, tq), F32)

    m_ref[...] = jnp.full(m_ref.shape, M_INIT, F32)
    acc_ref[...] = jnp.zeros(acc_ref.shape, F32)
    heads = [slice(h * A_HEAD_DIM, (h + 1) * A_HEAD_DIM) for h in range(A_HEADS)]
    ones_rows = jnp.ones((TRI_PAD, ck), BF16)

    def att_body(c, carry):
        ks = chunk_start(c)
        s = sc_ref[pl.ds(ks, ck), :]
        is_tied = s == thr
        pre = jnp.dot(tri_ref[...], jnp.where(is_tied, 1.0, 0.0).astype(BF16), preferred_element_type=F32)
        before = red_ref[0:1, :]
        rank = pre[0:ck, :] + before
        red_ref[0:1, :] = before + pre[ck:ck + 1, :]
        sel = jnp.where(is_tied, jnp.where(rank >= room, -jnp.inf, s), s) >= thr
        logits = [jnp.dot(ka_ref[pl.ds(ks, ck), hs], qT_ref[hs, :], preferred_element_type=F32) for hs in heads]
        for h, hs in enumerate(heads):
            m_old = m_ref[h, 0:1, :]
            m_new = jnp.maximum(m_old, jnp.max(jnp.where(sel, logits[h], NEG_BIG), axis=0, keepdims=True))
            alpha = jnp.exp2(m_old - m_new)
            p = jnp.where(sel, jnp.exp2(logits[h] - m_new), 0.0).astype(BF16)
            m_ref[h, 0:1, :] = m_new
            lhs = jnp.concatenate([vT_ref[c, hs, :], ones_rows], axis=0)
            acc_ref[h] = acc_ref[h] * alpha + jnp.dot(lhs, p, preferred_element_type=F32)
        return carry

    lax.fori_loop(0, n_chunks, att_body, 0)
    for h, hs in enumerate(heads):
        denom = acc_ref[h, A_HEAD_DIM:A_HEAD_DIM + 1, :]
        o_ref[:, hs] = (acc_ref[h, 0:A_HEAD_DIM, :] / denom).T.astype(o_ref.dtype)


def _dsa_attention(qT, qiT, wT, ki, ka, vT, bsz, seq, tq):
    ck = min(CHUNK, seq)
    nck = seq // ck
    k_sel = min(TOPK_MAX, seq // 4)
    kern = functools.partial(_dsa_kernel, tq=tq, ck=ck, k_sel=k_sel)
    once = pl.Buffered(1)
    out = pl.pallas_call(
        kern,
        grid=(bsz, seq // tq),
        in_specs=[
            pl.BlockSpec((None, A_WIDTH, tq), lambda b, i: (b, 0, i)),
            pl.BlockSpec((None, IDX_HEADS * LANES, tq), lambda b, i: (b, 0, i)),
            pl.BlockSpec((None, SUBLANES, tq), lambda b, i: (b, 0, i)),
            pl.BlockSpec((None, seq, LANES), lambda b, i: (b, 0, 0), pipeline_mode=once),
            pl.BlockSpec((None, seq, A_WIDTH), lambda b, i: (b, 0, 0), pipeline_mode=once),
            pl.BlockSpec((None, nck, A_WIDTH, ck), lambda b, i: (b, 0, 0, 0), pipeline_mode=once),
            pl.BlockSpec((ck + TRI_PAD, ck), lambda b, i: (0, 0), pipeline_mode=once),
        ],
        out_specs=pl.BlockSpec((None, tq, A_WIDTH), lambda b, i: (b, i, 0)),
        out_shape=jax.ShapeDtypeStruct((bsz, seq, A_WIDTH), BF16),
        scratch_shapes=[pltpu.VMEM((seq, tq), F32), pltpu.VMEM((SUBLANES, tq), F32),
                        pltpu.VMEM((2 * SUBLANES, tq), F32),
                        pltpu.VMEM((A_HEADS, SUBLANES, tq), F32),
                        pltpu.VMEM((A_HEADS, A_HEAD_DIM + TRI_PAD, tq), F32)],
        compiler_params=_cparams(("parallel", "arbitrary")),
        name="dsa_attention",
    )(qT, qiT, wT, ki, ka, vT, _prefix_matrix(ck))
    return out.reshape(bsz * seq, A_WIDTH)


def _prefix_matrix(n):
    r = jnp.arange(n + TRI_PAD)[:, None]
    c = jnp.arange(n)[None, :]
    return jnp.logical_or(c < r, r >= n).astype(BF16)


SB_CK = 256
SB_TQ = 1024
SB_STEP = 4


def _sb_kernel(q_ref, kT_ref, v_ref, u_ref, o_ref, acc_ref, carry_ref, *, tq, ck):
    i = pl.program_id(2)
    q = q_ref[...]
    acc_ref[...] = jnp.zeros(acc_ref.shape, F32)
    carry_ref[...] = jnp.zeros(carry_ref.shape, F32)
    per = tq // ck

    def chunk(c, r0=None):
        diag = r0 is not None
        r0 = r0 or 0
        rows = tq - r0
        z = jnp.dot(q[r0:, :], kT_ref[c], preferred_element_type=F32)
        neg_abs = pltpu.bitcast(pltpu.bitcast(z, I32) | jnp.int32(-2 ** 31), F32)
        neg_soft = jnp.log(1.0 + jnp.exp2(neg_abs)) * (-LOG2E)
        lk = neg_soft - jnp.maximum(z, 0.0)
        if diag:
            q_pos = i * tq + r0 + lax.broadcasted_iota(I32, (rows, ck), 0)
            mask = (c * ck + lax.broadcasted_iota(I32, (rows, ck), 1)) < q_pos
            lk = jnp.where(mask, lk, 0.0)
        cum = jnp.dot(lk.astype(BF16), u_ref[...], preferred_element_type=F32)
        a = jnp.exp2(z + cum)
        if diag:
            a = jnp.where(mask, a, 0.0)
        vs = pl.multiple_of(c * ck, ck)
        pv = jnp.dot(a.astype(BF16), v_ref[pl.ds(vs, ck), :], preferred_element_type=F32)
        return pv, cum[:, 0:1], r0

    def fold(parts):
        for pv, total, r0 in parts:
            carry = carry_ref[r0:, :]
            acc_ref[r0:, :] += jnp.exp2(carry) * pv
            carry_ref[r0:, :] = carry + jnp.broadcast_to(total, carry.shape)

    step = min(SB_STEP, per)
    for d0 in reversed(range(0, per, step)):
        fold([chunk(i * per + d0 + d, (d0 + d) * ck) for d in reversed(range(step))])

    def body(s, carry):
        first = i * per - 1 - s * step
        fold([chunk(first - d) for d in range(step)])
        return carry

    lax.fori_loop(0, i * (per // step), body, 0)
    o_ref[...] = acc_ref[...].astype(o_ref.dtype)


def _cumsum_matrix(n):
    j = jnp.arange(n)[:, None]
    s = jnp.arange(n)[None, :]
    return (j >= s).astype(BF16)


def _sb_attention(sq, skT, sv, bsz, seq):
    t = bsz * seq
    tq = min(SB_TQ, seq)
    ck = min(SB_CK, seq)
    nq = seq // tq
    return pl.pallas_call(
        functools.partial(_sb_kernel, tq=tq, ck=ck),
        grid=(bsz, SB_HEADS, nq),
        in_specs=[
            pl.BlockSpec((tq, SB_HEAD_DIM), lambda b, h, i: (b * nq + i, h)),
            pl.BlockSpec((None, None, seq // ck, SB_HEAD_DIM, ck), lambda b, h, i: (b, h, 0, 0, 0)),
            pl.BlockSpec((None, seq, SB_HEAD_DIM), lambda b, h, i: (b, 0, h)),
            pl.BlockSpec((ck, ck), lambda b, h, i: (0, 0)),
        ],
        out_specs=pl.BlockSpec((tq, SB_HEAD_DIM), lambda b, h, i: (b * nq + i, h)),
        out_shape=jax.ShapeDtypeStruct((t, SB_WIDTH), BF16),
        scratch_shapes=[pltpu.VMEM((tq, SB_HEAD_DIM), F32), pltpu.VMEM((tq, LANES), F32)],
        compiler_params=_cparams(("parallel", "parallel", "arbitrary")),
        name="stickbreak",
    )(sq, skT, sv, _cumsum_matrix(ck))


CB_HALO = 32
CC_HALO = 16
CONV_ROWS = 64


def _conv_kernel(ga_ref, gb_ref, hga_ref, hgb_ref, cb_ref, cc_ref, cx_ref, hcc_ref, hcx_ref,
                 wb_ref, bb_ref, lng_ref, lnb_ref, wc_ref, ob_ref, oc_ref, buf_ref, u_ref, buf2_ref, *, ts, seq):
    i = pl.program_id(0)
    keep = jnp.where((i * ts) % seq == 0, 0.0, 1.0)

    f32 = lambda ref: ref[...].astype(F32)
    buf_ref[0:CB_HALO, :] = f32(hga_ref) * jax.nn.sigmoid(f32(hgb_ref)) * keep
    buf_ref[CB_HALO:, :] = f32(ga_ref) * jax.nn.sigmoid(f32(gb_ref))
    off0 = CB_HALO - (CB_CONV - 1)
    for r in range(ts // CONV_ROWS):
        t0 = r * CONV_ROWS
        for g in range(CB_WIDTH // LANES):
            ls = slice(g * LANES, (g + 1) * LANES)
            acc = jnp.broadcast_to(bb_ref[:, ls], (CONV_ROWS, LANES))
            for s in range(SUBLANES):
                rows = CONV_ROWS + (SUBLANES if s else 0)
                part = None
                for k in range(CB_CONV):
                    if (off0 + k) % SUBLANES != s:
                        continue
                    base = t0 + (off0 + k) - s
                    term = wb_ref[k:k + 1, ls] * buf_ref[base:base + rows, ls]
                    part = term if part is None else part + term
                acc = acc + part[s:s + CONV_ROWS, :]
            u_ref[t0:t0 + CONV_ROWS, ls] = acc
    u = u_ref[...]
    mu = jnp.mean(u, axis=-1, keepdims=True)
    uc = u - mu
    y = uc * lax.rsqrt(jnp.mean(uc * uc, axis=-1, keepdims=True) + NORM_EPS) * lng_ref[...] + lnb_ref[...]
    ob_ref[...] = (y * jax.nn.sigmoid(y)).astype(ob_ref.dtype)

    buf2_ref[0:CC_HALO, :] = f32(hcc_ref) * f32(hcx_ref) * keep
    buf2_ref[CC_HALO:, :] = f32(cc_ref) * f32(cx_ref)
    conv = jnp.zeros((ts, CC_WIDTH), F32)
    for k in range(CC_CONV):
        base = CC_HALO - (CC_CONV - 1) + k
        conv = conv + wc_ref[k:k + 1, :] * buf2_ref[base:base + ts, :]
    oc_ref[...] = (f32(cb_ref) * conv).astype(oc_ref.dtype)


def _conv_mixers(proj, wb, bb, lng, lnb, wc, seq, ts):
    t = proj.shape[0]
    w = CB_WIDTH
    c0 = (OFF_GLU - SPLIT_COL) // w
    c1 = (OFF_CC - SPLIT_COL) // w
    cur = lambda col: pl.BlockSpec((ts, w), lambda i: (i, col))
    halo = lambda rows, col: pl.BlockSpec((rows, w), lambda i: (jnp.maximum(i * (ts // rows) - 1, 0), col))
    full = lambda r: pl.BlockSpec((r, w), lambda i: (0, 0))
    wb_p = jnp.pad(wb, ((0, CB_HALO - CB_CONV), (0, 0)))
    wc_p = jnp.pad(wc, ((0, SUBLANES - CC_CONV), (0, 0)))
    return pl.pallas_call(
        functools.partial(_conv_kernel, ts=ts, seq=seq),
        grid=(t // ts,),
        in_specs=[cur(c0), cur(c0 + 1), halo(CB_HALO, c0), halo(CB_HALO, c0 + 1),
                  cur(c1), cur(c1 + 1), cur(c1 + 2), halo(CC_HALO, c1 + 1), halo(CC_HALO, c1 + 2),
                  full(CB_HALO), full(1), full(1), full(1), full(SUBLANES)],
        out_specs=[pl.BlockSpec((ts, w), lambda i: (i, 0))] * 2,
        out_shape=[jax.ShapeDtypeStruct((t, w), BF16)] * 2,
        scratch_shapes=[pltpu.VMEM((CB_HALO + ts, w), F32), pltpu.VMEM((ts, w), F32),
                        pltpu.VMEM((CC_HALO + ts, w), F32)],
        compiler_params=_cparams(("parallel",)),
        name="conv_mixers",
    )(proj, proj, proj, proj, proj, proj, proj, proj, proj,
      wb_p, bb.reshape(1, w), lng.reshape(1, w), lnb.reshape(1, w), wc_p)


def _merge_kernel(x_ref, oa_ref, ob_ref, oc_ref, od_ref, g0_ref, g1_ref, g2_ref, g3_ref, bg_ref,
                  wa_ref, wb_ref, wc_ref, wd_ref, wm_ref, o_ref):
    merged = None
    branches = ((oa_ref, wa_ref, g0_ref), (ob_ref, wb_ref, g1_ref), (oc_ref, wc_ref, g2_ref), (od_ref, wd_ref, g3_ref))
    for n, (b_ref, w_ref, g_ref) in enumerate(branches):
        y = jnp.dot(b_ref[...], w_ref[...], preferred_element_type=F32)
        term = jax.nn.sigmoid(g_ref[...].astype(F32) + bg_ref[n:n + 1, :]) * y
        merged = term if merged is None else merged + term
    o_ref[...] = x_ref[...] + jnp.dot(merged.astype(BF16), wm_ref[...], preferred_element_type=F32)


def _merge(x2d, oa, ob, oc, od, proj, b_gate, w_oa, w_ob, w_oc, w_od, w_merge, tm):
    t, d = x2d.shape
    g0 = (OFF_GATE - SPLIT_COL) // d
    row = lambda i: (i, 0)
    fixed = lambda i: (0, 0)
    br = pl.BlockSpec((tm, A_WIDTH), row)
    gate = lambda n: pl.BlockSpec((tm, d), lambda i: (i, g0 + n))
    wsp = pl.BlockSpec((A_WIDTH, d), fixed)
    bf = lambda w: w.astype(BF16)
    return pl.pallas_call(
        _merge_kernel,
        grid=(t // tm,),
        in_specs=[pl.BlockSpec((tm, d), row), br, br, br, br, gate(0), gate(1), gate(2), gate(3),
                  pl.BlockSpec((N_BRANCH, d), fixed), wsp, wsp, wsp, wsp, pl.BlockSpec((d, d), fixed)],
        out_specs=pl.BlockSpec((tm, d), row),
        out_shape=jax.ShapeDtypeStruct((t, d), F32),
        compiler_params=_cparams(("parallel",)),
        name="merge",
    )(x2d, oa, ob, oc, od, proj, proj, proj, proj, b_gate.reshape(N_BRANCH, d),
      bf(w_oa), bf(w_ob), bf(w_oc), bf(w_od), bf(w_merge))


FFN_HALO = 8


def _ffn_kernel(x_ref, hx_ref, g_ref, wg_ref, wu_ref, cw_ref, wd_ref, o_ref, h_ref, gp_ref, acc_ref, *, tm, seq):
    i = pl.program_id(0)
    f = pl.program_id(1)

    def norm(x):
        ms = jnp.mean(x * x, axis=-1, keepdims=True)
        return (x * lax.rsqrt(ms + NORM_EPS) * g_ref[...]).astype(BF16)

    @pl.when(f == 0)
    def _():
        h_ref[0:FFN_HALO, :] = norm(hx_ref[...])
        h_ref[FFN_HALO:, :] = norm(x_ref[...])
        acc_ref[...] = x_ref[...]

    keep = jnp.where((i * tm) % seq == 0, 0.0, 1.0)
    gp_ref[...] = jnp.dot(h_ref[...], wg_ref[...], preferred_element_type=F32)
    gp_ref[0:FFN_HALO, :] = gp_ref[0:FFN_HALO, :] * keep
    gt = jnp.zeros((tm, gp_ref.shape[1]), F32)
    for k in range(FFN_CONV):
        base = FFN_HALO - (FFN_CONV - 1) + k
        gt = gt + cw_ref[k:k + 1, :] * gp_ref[base:base + tm, :]
    up = jnp.dot(h_ref[FFN_HALO:, :], wu_ref[...], preferred_element_type=F32)
    act = (gt * jax.nn.sigmoid(gt) * up).astype(BF16)
    acc_ref[...] += jnp.dot(act, wd_ref[...], preferred_element_type=F32)

    @pl.when(f == pl.num_programs(1) - 1)
    def _():
        o_ref[...] = acc_ref[...]


def _ffn(x2d, g, wg, wu, cw, wd, seq, tm, tf):
    t, d = x2d.shape
    ff = wg.shape[1]
    cw_p = jnp.pad(cw, ((0, SUBLANES - FFN_CONV), (0, 0)))
    return pl.pallas_call(
        functools.partial(_ffn_kernel, tm=tm, seq=seq),
        grid=(t // tm, ff // tf),
        in_specs=[
            pl.BlockSpec((tm, d), lambda i, f: (i, 0)),
            pl.BlockSpec((FFN_HALO, d), lambda i, f: (jnp.maximum(i * (tm // FFN_HALO) - 1, 0), 0)),
            pl.BlockSpec((1, d), lambda i, f: (0, 0)),
            pl.BlockSpec((d, tf), lambda i, f: (0, f)),
            pl.BlockSpec((d, tf), lambda i, f: (0, f)),
            pl.BlockSpec((SUBLANES, tf), lambda i, f: (0, f)),
            pl.BlockSpec((tf, d), lambda i, f: (f, 0)),
        ],
        out_specs=pl.BlockSpec((tm, d), lambda i, f: (i, 0)),
        out_shape=jax.ShapeDtypeStruct((t, d), F32),
        scratch_shapes=[pltpu.VMEM((FFN_HALO + tm, d), BF16), pltpu.VMEM((FFN_HALO + tm, tf), F32),
                        pltpu.VMEM((tm, d), F32)],
        compiler_params=_cparams(("parallel", "arbitrary")),
        name="ffn",
    )(x2d, x2d, g.reshape(1, d), wg.astype(BF16), wu.astype(BF16), cw_p, wd.astype(BF16))


def _tiles(seq):
    big = min(seq, 512)
    return dict(
        inproj_tm=min(seq, 1024), inproj_tn=SPLIT_COL,
        conv_ts=big, merge_tm=big, ffn_tm=big, ffn_tf=D_FF // 2,
        dsa_tq=min(seq, 512),
    )


def kernel(x, positions, g_mix, w_in, b_gate, g_qa, g_ka, g_kidx, w_oa, cb_conv_w, cb_conv_b, cb_ln_g, cb_ln_b,
           w_ob, cc_conv_w, w_oc, w_od, w_merge, g_ffn, w_ffn_gate, w_ffn_up, ffn_conv_w, w_ffn_down):
    bsz, seq, dm = x.shape
    depth = w_in.shape[0]
    assert dm == D_MODEL and seq % min(CHUNK, seq) == 0 and seq % LANES == 0
    tl = _tiles(seq)
    t = bsz * seq

    cos_a, sin_a = _rope_tables(positions, A_HEAD_DIM)
    cos_i, sin_i = _rope_tables(positions, IDX_DIM)
    ca, sa = _rope_lane_tables(cos_a, sin_a, A_HEAD_DIM)
    ci, si = _rope_lane_tables(cos_i, sin_i, IDX_DIM)

    x2d = x.reshape(t, dm)
    for l in range(depth):
        w_packed = _pack_w_in(jnp.swapaxes(w_in[l], 0, 1))
        tm, tn = tl["inproj_tm"], tl["inproj_tn"]
        proj_a, proj = _inproj(x2d, g_mix[l], w_packed, tm, tn)
        qT, qiT, wT, ki, ka, vT, sq, skT, sv = _prep(proj_a, proj, ca, sa, ci, si, g_qa[l], g_ka[l], g_kidx[l],
                                                     bsz, seq)
        o_a = _dsa_attention(qT, qiT, wT, ki, ka, vT, bsz, seq, tl["dsa_tq"])
        o_d = _sb_attention(sq, skT, sv, bsz, seq)
        o_b, o_c = _conv_mixers(proj, cb_conv_w[l], cb_conv_b[l], cb_ln_g[l], cb_ln_b[l], cc_conv_w[l],
                                seq, tl["conv_ts"])
        x2d = _merge(x2d, o_a, o_b, o_c, o_d, proj, b_gate[l], w_oa[l], w_ob[l], w_oc[l], w_od[l], w_merge[l],
                     tl["merge_tm"])
        x2d = _ffn(x2d, g_ffn[l], w_ffn_gate[l], w_ffn_up[l], ffn_conv_w[l], w_ffn_down[l], seq,
                   tl["ffn_tm"], tl["ffn_tf"])
    return x2d.reshape(bsz, seq, dm)
```
